```python
import jax, jax.numpy as jnp
from jax import lax
import numpy as np

D_MODEL = 4096
BATCH = 4
SEQ = 4096
DEPTH = 1

D_MIX = D_MODEL
D_GMLP = D_MIX // 2
GMLP_CHUNK = 128
GMLP_GROUPS = 16
GMLP_GROUP_DIM = D_GMLP // GMLP_GROUPS
D_MLSTM = D_MIX - D_GMLP
MLSTM_HEADS = 4
MLSTM_DV = D_MLSTM // MLSTM_HEADS
MLSTM_DQK = MLSTM_DV // 2
MLSTM_CHUNK = 64
GATE_SOFTCAP = 15.0
N_GROUPS = 8
EXPERTS_PER_GROUP = 8
N_EXPERTS = N_GROUPS * EXPERTS_PER_GROUP
TOP_K = 2
D_EXPERT = D_MODEL // 8
EXPERT_BLOCK = 128
EPS = 1e-6

PROJ_SIZES = (D_GMLP, D_GMLP,
              MLSTM_HEADS * MLSTM_DQK, MLSTM_HEADS * MLSTM_DQK,
              D_MLSTM, D_MLSTM,
              MLSTM_HEADS, MLSTM_HEADS)
D_IN_PROJ = sum(PROJ_SIZES)
PROJ_SPLITS = tuple(int(s) for s in np.cumsum(PROJ_SIZES)[:-1])

kernel_name = "hybrid_gmlp_mlstm_hmoe_block"


def rmsnorm(x, g):
    xf = x.astype(jnp.float32)
    y = xf * lax.rsqrt(jnp.mean(xf * xf, axis=-1, keepdims=True) + EPS)
    return (y * g.astype(jnp.float32)).astype(x.dtype)


def softcap(z):
    return GATE_SOFTCAP * jnp.tanh(z / GATE_SOFTCAP)


def gmlp_mixer(u, v, g_v, w_s, b_s):
    B, S, _ = u.shape
    u = jax.nn.gelu(u)
    v = rmsnorm(jax.nn.gelu(v), g_v)
    v = v.reshape(B, S // GMLP_CHUNK, GMLP_CHUNK, GMLP_GROUPS, GMLP_GROUP_DIM)
    causal = jnp.tril(jnp.ones((GMLP_CHUNK, GMLP_CHUNK), dtype=bool))
    w = jnp.where(causal[None], w_s, 0)
    z = jnp.einsum('gts,bnsgd->bntgd', w, v) + b_s.T[None, None, :, :, None]
    return u * z.reshape(B, S, D_GMLP).astype(u.dtype)


def mlstm_mixer(q, k, v, o, ig_pre, fg_pre, ig_b, fg_b, out_g):
    B, S, _ = q.shape
    H, L = MLSTM_HEADS, MLSTM_CHUNK
    NC = S // L
    f32 = jnp.float32

    def heads(t, d):
        return t.astype(f32).reshape(B, NC, L, H, d).transpose(1, 0, 3, 2, 4)

    def gates(t):
        return t.reshape(B, NC, L, H).transpose(1, 0, 3, 2)

    qh = heads(q, MLSTM_DQK)
    kh = heads(k, MLSTM_DQK) * (MLSTM_DQK ** -0.5)
    vh = heads(v, MLSTM_DV)
    ig = gates(softcap(ig_pre.astype(f32) + ig_b.astype(f32)))
    logf = gates(jax.nn.log_sigmoid(softcap(fg_pre.astype(f32) + fg_b.astype(f32))))
    causal = jnp.tril(jnp.ones((L, L), dtype=bool))

    def step(carry, xs):
        C, n, m = carry
        qc, kc, vc, igc, lfc = xs
        b = jnp.cumsum(lfc, axis=-1)
        d_log = jnp.where(causal, b[..., :, None] - b[..., None, :] + igc[..., None, :], -jnp.inf)
        inter = b + m[..., None]
        m_row = jnp.maximum(inter, jnp.max(d_log, axis=-1))
        w_intra = jnp.exp(d_log - m_row[..., None])
        w_inter = jnp.exp(inter - m_row)
        s = jnp.einsum('bhid,bhjd->bhij', qc, kc) * w_intra
        num = (w_inter[..., None] * jnp.einsum('bhid,bhde->bhie', qc, C)
               + jnp.einsum('bhij,bhje->bhie', s, vc))
        den = w_inter * jnp.einsum('bhid,bhd->bhi', qc, n) + jnp.sum(s, axis=-1)
        h = num / jnp.maximum(jnp.abs(den), jnp.exp(-m_row))[..., None]
        b_last = b[..., -1]
        tail = b_last[..., None] - b + igc
        m_new = jnp.maximum(b_last + m, jnp.max(tail, axis=-1))
        w_tail = jnp.exp(tail - m_new[..., None])
        decay = jnp.exp(b_last + m - m_new)
        C_new = decay[..., None, None] * C + jnp.einsum('bhj,bhjd,bhje->bhde', w_tail, kc, vc)
        n_new = decay[..., None] * n + jnp.einsum('bhj,bhjd->bhd', w_tail, kc)
        return (C_new, n_new, m_new), h

    init = (jnp.zeros((B, H, MLSTM_DQK, MLSTM_DV), f32),
            jnp.zeros((B, H, MLSTM_DQK), f32),
            jnp.zeros((B, H), f32))
    _, h = lax.scan(step, init, (qh, kh, vh, ig, logf))
    h = h.transpose(1, 0, 3, 2, 4).reshape(B, S, H, MLSTM_DV)
    h = h * lax.rsqrt(jnp.mean(h * h, axis=-1, keepdims=True) + EPS)
    h = h * out_g.astype(f32).reshape(H, MLSTM_DV)
    h = h.reshape(B, S, D_MLSTM) * jax.nn.sigmoid(o.astype(f32))
    return h.astype(q.dtype)


def hierarchical_moe(xn, wr_g, br_g, wr_e, br_e, w_gate, w_up, w_down):
    B, S, D = xn.shape
    N = B * S
    A = N * TOP_K
    xt = xn.reshape(N, D)
    xf = xt.astype(jnp.float32)
    g_logits = xf @ wr_g.astype(jnp.float32) + br_g.astype(jnp.float32)
    g_prob = jax.nn.softmax(g_logits, axis=-1)
    g_sel = jnp.argmax(g_logits, axis=-1)
    e_logits = (xf @ wr_e.astype(jnp.float32) + br_e.astype(jnp.float32)).reshape(N, N_GROUPS, EXPERTS_PER_GROUP)
    e_sel_logits = jnp.take_along_axis(e_logits, g_sel[:, None, None], axis=1)[:, 0]
    top_v, top_i = lax.top_k(e_sel_logits, TOP_K)
    weights = jax.nn.softmax(top_v, axis=-1) * jnp.take_along_axis(g_prob, g_sel[:, None], axis=1)
    expert_id = (g_sel[:, None] * EXPERTS_PER_GROUP + top_i).reshape(A)
    token_id = jnp.repeat(jnp.arange(N, dtype=jnp.int32), TOP_K)
    wts = weights.reshape(A)

    order = jnp.argsort(expert_id)
    e_s, tok_s, w_s = expert_id[order], token_id[order], wts[order]
    counts = jnp.zeros((N_EXPERTS,), jnp.int32).at[expert_id].add(1)
    padded = ((counts + EXPERT_BLOCK - 1) // EXPERT_BLOCK) * EXPERT_BLOCK
    pad_end = jnp.cumsum(padded)
    pad_start = pad_end - padded
    cnt_start = jnp.cumsum(counts) - counts
    pos = pad_start[e_s] + jnp.arange(A, dtype=jnp.int32) - cnt_start[e_s]
    NB = -(-(A + N_EXPERTS * (EXPERT_BLOCK - 1)) // EXPERT_BLOCK)
    P = NB * EXPERT_BLOCK
    tok_buf = jnp.full((P,), N, jnp.int32).at[pos].set(tok_s)
    w_buf = jnp.zeros((P,), jnp.float32).at[pos].set(w_s)
    block_start = jnp.arange(NB, dtype=jnp.int32) * EXPERT_BLOCK
    block_e = jnp.minimum(jnp.sum(block_start[:, None] >= pad_end[None, :], axis=1), N_EXPERTS - 1)
    x_pad = jnp.concatenate([xt, jnp.zeros((1, D), xt.dtype)], axis=0)

    def run_block(args):
        idx, e, wb = args
        xb = x_pad[idx]
        hb = jax.nn.silu(xb @ w_gate[e]) * (xb @ w_up[e])
        return ((hb @ w_down[e]).astype(jnp.float32) * wb[:, None]).astype(xn.dtype)

    out = lax.map(run_block, (tok_buf.reshape(NB, EXPERT_BLOCK), block_e, w_buf.reshape(NB, EXPERT_BLOCK)))
    y = jax.ops.segment_sum(out.reshape(P, D), tok_buf, num_segments=N + 1)[:N]
    return y.reshape(B, S, D).astype(xn.dtype)


def setup_inputs(seed: int = 0) -> dict:
    key = jax.random.key(seed)
    ks = jax.random.split(key, 20)
    f32 = jnp.float32
    nrm = lambda k, shape, s: jax.random.normal(k, shape, f32) * s
    return {
        "x": nrm(ks[0], (BATCH, SEQ, D_MODEL), 1.0),
        "norm_mix_g": 1.0 + nrm(ks[1], (DEPTH, D_MODEL), 0.02),
        "w_in": nrm(ks[2], (DEPTH, D_MODEL, D_IN_PROJ), D_MODEL ** -0.5),
        "gmlp_v_norm_g": 1.0 + nrm(ks[3], (DEPTH, D_GMLP), 0.02),
        "gmlp_spatial_w": nrm(ks[4], (DEPTH, GMLP_GROUPS, GMLP_CHUNK, GMLP_CHUNK), GMLP_CHUNK ** -0.5),
        "gmlp_spatial_b": 1.0 + nrm(ks[5], (DEPTH, GMLP_GROUPS, GMLP_CHUNK), 0.02),
        "mlstm_igate_b": nrm(ks[6], (DEPTH, MLSTM_HEADS), 0.1),
        "mlstm_fgate_b": 3.0 + nrm(ks[7], (DEPTH, MLSTM_HEADS), 0.5),
        "mlstm_out_norm_g": 1.0 + nrm(ks[8], (DEPTH, D_MLSTM), 0.02),
        "w_out": nrm(ks[9], (DEPTH, D_MIX, D_MODEL), D_MIX ** -0.5),
        "norm_ffn_g": 1.0 + nrm(ks[10], (DEPTH, D_MODEL), 0.02),
        "router_group_w": nrm(ks[11], (DEPTH, D_MODEL, N_GROUPS), D_MODEL ** -0.5),
        "router_group_b": nrm(ks[12], (DEPTH, N_GROUPS), 0.01),
        "router_expert_w": nrm(ks[13], (DEPTH, D_MODEL, N_EXPERTS), D_MODEL ** -0.5),
        "router_expert_b": nrm(ks[14], (DEPTH, N_EXPERTS), 0.01),
        "expert_w_gate": nrm(ks[15], (DEPTH, N_EXPERTS, D_MODEL, D_EXPERT), D_MODEL ** -0.5),
        "expert_w_up": nrm(ks[16], (DEPTH, N_EXPERTS, D_MODEL, D_EXPERT), D_MODEL ** -0.5),
        "expert_w_down": nrm(ks[17], (DEPTH, N_EXPERTS, D_EXPERT, D_MODEL), D_EXPERT ** -0.5),
        "final_norm_g": 1.0 + nrm(ks[18], (D_MODEL,), 0.02),
    }


def reference(x, norm_mix_g, w_in, gmlp_v_norm_g, gmlp_spatial_w, gmlp_spatial_b,
              mlstm_igate_b, mlstm_fgate_b, mlstm_out_norm_g, w_out, norm_ffn_g,
              router_group_w, router_group_b, router_expert_w, router_expert_b,
              expert_w_gate, expert_w_up, expert_w_down, final_norm_g):
    for l in range(DEPTH):
        xn = rmsnorm(x, norm_mix_g[l])
        proj = xn @ w_in[l]
        u_a, v_a, q_b, k_b, v_b, o_b, i_b, f_b = jnp.split(proj, PROJ_SPLITS, axis=-1)
        y_a = gmlp_mixer(u_a, v_a, gmlp_v_norm_g[l], gmlp_spatial_w[l], gmlp_spatial_b[l])
        y_b = mlstm_mixer(q_b, k_b, v_b, o_b, i_b, f_b, mlstm_igate_b[l], mlstm_fgate_b[l],
                          mlstm_out_norm_g[l])
        x = x + jnp.concatenate([y_a, y_b], axis=-1) @ w_out[l]
        x = x + hierarchical_moe(rmsnorm(x, norm_ffn_g[l]), router_group_w[l], router_group_b[l],
                                 router_expert_w[l], router_expert_b[l], expert_w_gate[l],
                                 expert_w_up[l], expert_w_down[l])
    return rmsnorm(x, final_norm_g)
```

```python
import functools

import jax
import jax.numpy as jnp
from jax import lax
from jax.experimental import pallas as pl
from jax.experimental.pallas import tpu as pltpu

F32 = jnp.float32
BF16 = jnp.bfloat16
I32 = jnp.int32

D_MODEL = 4096
D_GMLP = 2048
GMLP_CHUNK = 128
GMLP_GROUPS = 16
D_MLSTM = 2048
MLSTM_HEADS = 4
MLSTM_DV = 512
MLSTM_DQK = 256
GATE_SOFTCAP = 15.0
N_GROUPS = 8
EXPERTS_PER_GROUP = 8
N_EXPERTS = 64
TOP_K = 2
D_EXPERT = 512
EPS = 1e-6
D_PROJ_MAIN = 10240
N_GATE_COLS = 8

LANES = 128
VMEM_LIMIT = 56 * 1024 * 1024
MLSTM_L = 256
EXPERT_BM = 256
HALF = D_MODEL // 2


def _cparams(sem):
    return pltpu.CompilerParams(dimension_semantics=sem, vmem_limit_bytes=VMEM_LIMIT)


def _rms(x, g):
    return x * lax.rsqrt(jnp.mean(x * x, axis=-1, keepdims=True) + EPS) * g


def _in_proj_kernel(x_ref, g_ref, w_ref, wg_ref, proj_ref, gate_ref, xn_sc):
    @pl.when(pl.program_id(1) == 0)
    def _():
        xb = _rms(x_ref[...], g_ref[...]).astype(BF16)
        xn_sc[...] = xb
        gate_ref[...] = jnp.dot(xb, wg_ref[...], preferred_element_type=F32)

    proj_ref[...] = jnp.dot(xn_sc[...], w_ref[...], preferred_element_type=F32).astype(BF16)


def _in_proj(x2, g, w_main, w_gate, tm=512, tn=1024):
    n = x2.shape[0]
    return pl.pallas_call(
        _in_proj_kernel,
        out_shape=(jax.ShapeDtypeStruct((n, D_PROJ_MAIN), BF16),
                   jax.ShapeDtypeStruct((n, LANES), F32)),
        grid=(n // tm, D_PROJ_MAIN // tn),
        in_specs=[pl.BlockSpec((tm, D_MODEL), lambda i, j: (i, 0)),
                  pl.BlockSpec((1, D_MODEL), lambda i, j: (0, 0)),
                  pl.BlockSpec((D_MODEL, tn), lambda i, j: (0, j)),
                  pl.BlockSpec((D_MODEL, LANES), lambda i, j: (0, 0))],
        out_specs=(pl.BlockSpec((tm, tn), lambda i, j: (i, j)),
                   pl.BlockSpec((tm, LANES), lambda i, j: (i, 0))),
        scratch_shapes=[pltpu.VMEM((tm, D_MODEL), BF16)],
        compiler_params=_cparams(("parallel", "arbitrary")),
        name="in_proj",
    )(x2, g, w_main, w_gate)


def _gmlp_kernel(u_ref, v_ref, gv_ref, w_ref, bt_ref, y_ref, wm_sc, *, rows):
    c = GMLP_CHUNK

    @pl.when(pl.program_id(0) == 0)
    def _():
        r = lax.broadcasted_iota(I32, (c, c), 0)
        s = lax.broadcasted_iota(I32, (c, c), 1)
        for g in range(GMLP_GROUPS):
            wm_sc[g] = jnp.where(s <= r, w_ref[g], 0.0).astype(BF16)

    for ci in range(rows // c):
        rs = slice(ci * c, (ci + 1) * c)
        gu = jax.nn.gelu(u_ref[rs, :].astype(F32))
        gv = jax.nn.gelu(v_ref[rs, :].astype(F32))
        vn = _rms(gv, gv_ref[...]).astype(BF16)
        for g in range(GMLP_GROUPS):
            cs = slice(g * c, (g + 1) * c)
            z = jnp.dot(wm_sc[g], vn[:, cs], preferred_element_type=F32) + bt_ref[:, g:g + 1]
            y_ref[rs, cs] = (gu[:, cs] * z).astype(BF16)


def _gmlp(proj, gv, w_s, b_t, rows=512):
    n = proj.shape[0]
    return pl.pallas_call(
        functools.partial(_gmlp_kernel, rows=rows),
        out_shape=jax.ShapeDtypeStruct((n, D_GMLP), BF16),
        grid=(n // rows,),
        in_specs=[pl.BlockSpec((rows, D_GMLP), lambda i: (i, 0)),
                  pl.BlockSpec((rows, D_GMLP), lambda i: (i, 1)),
                  pl.BlockSpec((1, D_GMLP), lambda i: (0, 0)),
                  pl.BlockSpec((GMLP_GROUPS, GMLP_CHUNK, GMLP_CHUNK), lambda i: (0, 0, 0)),
                  pl.BlockSpec((GMLP_CHUNK, GMLP_GROUPS), lambda i: (0, 0))],
        out_specs=pl.BlockSpec((rows, D_GMLP), lambda i: (i, 0)),
        scratch_shapes=[pltpu.VMEM((GMLP_GROUPS, GMLP_CHUNK, GMLP_CHUNK), BF16)],
        compiler_params=_cparams(("arbitrary",)),
        name="gmlp",
    )(proj, proj, gv, w_s, b_t)


def _mlstm_kernel(q_ref, k_ref, v_ref, o_ref, gp_ref, gb_ref, og_ref, y_ref, c_sc, n_sc, m_sc):
    L, H, DQK, DV = MLSTM_L, MLSTM_HEADS, MLSTM_DQK, MLSTM_DV
    scale = DQK ** -0.5

    @pl.when(pl.program_id(1) == 0)
    def _():
        c_sc[...] = jnp.zeros_like(c_sc)
        n_sc[...] = jnp.zeros_like(n_sc)
        m_sc[...] = jnp.zeros_like(m_sc)

    lane = lax.broadcasted_iota(I32, (L, LANES), 1)
    z = gp_ref[...] + gb_ref[...]
    sc = GATE_SOFTCAP * jnp.tanh(z / GATE_SOFTCAP)
    logf = -(jnp.maximum(-sc, 0.0) + jnp.log1p(jnp.exp(-jnp.abs(sc))))
    row = lax.broadcasted_iota(I32, (L, L), 0)
    col = lax.broadcasted_iota(I32, (L, L), 1)
    causal = col <= row
    bcum = jnp.dot(causal.astype(F32), logf, precision=lax.Precision.HIGHEST,
                   preferred_element_type=F32)
    gcol = jnp.where(lane < H, sc, bcum)
    grow = gcol.T

    for h in range(H):
        ig_col, b_col = gcol[:, h:h + 1], gcol[:, H + h:H + h + 1]
        ig_row, b_row = grow[h:h + 1, :], grow[H + h:H + h + 1, :]
        m = m_sc[h, 0:1, 0:1]
        q = q_ref[:, h * DQK:(h + 1) * DQK]
        k = k_ref[:, h * DQK:(h + 1) * DQK]
        v = v_ref[:, h * DV:(h + 1) * DV]

        d_log = jnp.where(causal, b_col - b_row + ig_row, -jnp.inf)
        inter = b_col + m
        m_row = jnp.maximum(inter, jnp.max(d_log, axis=-1, keepdims=True))
        w_intra = jnp.exp(d_log - m_row)
        w_inter = jnp.exp(inter - m_row)
        s = lax.dot_general(q, k, (((1,), (1,)), ((), ())), preferred_element_type=F32)
        s = s * scale * w_intra
        c_old = c_sc[h]
        n_old = n_sc[h]
        num = (w_inter * jnp.dot(q, c_old.astype(BF16), preferred_element_type=F32)
               + jnp.dot(s.astype(BF16), v, preferred_element_type=F32))
        qn = jnp.sum(q.astype(F32) * n_old, axis=-1, keepdims=True)
        den = w_inter * qn + jnp.sum(s, axis=-1, keepdims=True)
        hh = num / jnp.maximum(jnp.abs(den), jnp.exp(-m_row))

        b_last = b_row[:, L - 1:L]
        tail = b_last - b_col + ig_col
        m_new = jnp.maximum(b_last + m, jnp.max(tail, axis=0, keepdims=True))
        w_tail = jnp.exp(tail - m_new)
        decay = jnp.exp(b_last + m - m_new)
        kw = k.astype(F32) * scale * w_tail
        c_sc[h] = decay * c_old + lax.dot_general(
            kw.astype(BF16), v, (((0,), (0,)), ((), ())), preferred_element_type=F32)
        n_sc[h] = decay * n_old + jnp.sum(kw, axis=0, keepdims=True)
        m_sc[h] = jnp.broadcast_to(m_new, m_sc.shape[1:])

        hn = hh * lax.rsqrt(jnp.mean(hh * hh, axis=-1, keepdims=True) + EPS)
        hn = hn * og_ref[:, h * DV:(h + 1) * DV]
        hn = hn * jax.nn.sigmoid(o_ref[:, h * DV:(h + 1) * DV].astype(F32))
        y_ref[:, h * DV:(h + 1) * DV] = hn.astype(BF16)


def _mlstm(proj, gate_pre, gate_b, out_g, batch, seq):
    n = proj.shape[0]
    L = MLSTM_L
    nc = seq // L
    qk_w = MLSTM_HEADS * MLSTM_DQK
    q_blk, k_blk = 2 * D_GMLP // qk_w, 2 * D_GMLP // qk_w + 1
    v_blk, o_blk = (2 * D_GMLP + 2 * qk_w) // D_MLSTM, (2 * D_GMLP + 2 * qk_w) // D_MLSTM + 1
    rmap = lambda b, c: b * nc + c
    return pl.pallas_call(
        _mlstm_kernel,
        out_shape=jax.ShapeDtypeStruct((n, D_MLSTM), BF16),
        grid=(batch, nc),
        in_specs=[pl.BlockSpec((L, qk_w), lambda b, c: (rmap(b, c), q_blk)),
                  pl.BlockSpec((L, qk_w), lambda b, c: (rmap(b, c), k_blk)),
                  pl.BlockSpec((L, D_MLSTM), lambda b, c: (rmap(b, c), v_blk)),
                  pl.BlockSpec((L, D_MLSTM), lambda b, c: (rmap(b, c), o_blk)),
                  pl.BlockSpec((L, LANES), lambda b, c: (rmap(b, c), 0)),
                  pl.BlockSpec((1, LANES), lambda b, c: (0, 0)),
                  pl.BlockSpec((1, D_MLSTM), lambda b, c: (0, 0))],
        out_specs=pl.BlockSpec((L, D_MLSTM), lambda b, c: (rmap(b, c), 0)),
        scratch_shapes=[pltpu.VMEM((MLSTM_HEADS, MLSTM_DQK, MLSTM_DV), F32),
                        pltpu.VMEM((MLSTM_HEADS, 1, MLSTM_DQK), F32),
                        pltpu.VMEM((MLSTM_HEADS, 8, LANES), F32)],
        compiler_params=_cparams(("parallel", "arbitrary")),
        name="mlstm",
    )(proj, proj, proj, proj, gate_pre, gate_b, out_g)


def _out_proj_kernel(ya_ref, yb_ref, wa_ref, wb_ref, x_ref, o_ref):
    acc = jnp.dot(ya_ref[...], wa_ref[...], preferred_element_type=F32)
    acc = acc + jnp.dot(yb_ref[...], wb_ref[...], preferred_element_type=F32)
    o_ref[...] = x_ref[...] + acc


def _out_proj(y_a, y_b, w_out, x2, tm=512, tn=1024):
    n = x2.shape[0]
    return pl.pallas_call(
        _out_proj_kernel,
        out_shape=jax.ShapeDtypeStruct((n, D_MODEL), F32),
        grid=(n // tm, D_MODEL // tn),
        in_specs=[pl.BlockSpec((tm, D_GMLP), lambda i, j: (i, 0)),
                  pl.BlockSpec((tm, D_MLSTM), lambda i, j: (i, 0)),
                  pl.BlockSpec((D_GMLP, tn), lambda i, j: (0, j)),
                  pl.BlockSpec((D_MLSTM, tn), lambda i, j: (1, j)),
                  pl.BlockSpec((tm, tn), lambda i, j: (i, j))],
        out_specs=pl.BlockSpec((tm, tn), lambda i, j: (i, j)),
        compiler_params=_cparams(("parallel", "arbitrary")),
        name="out_proj",
    )(y_a, y_b, w_out, w_out, x2)


def _router_kernel(x_ref, g_ref, wr_ref, br_ref, xp_ref, meta_ref, wts_ref, cnt_ref, carry_sc, *, tm):
    i = pl.program_id(0)

    @pl.when(i == 0)
    def _():
        carry_sc[...] = jnp.zeros_like(carry_sc)

    xn = _rms(x_ref[...], g_ref[...])
    bits = lax.bitcast_convert_type(xn.astype(BF16).astype(F32), I32)
    xp_ref[...] = bits[:, :HALF] | lax.shift_right_logical(bits[:, HALF:], 16)

    logits = jnp.dot(xn, wr_ref[...], precision=lax.Precision.HIGHEST,
                     preferred_element_type=F32) + br_ref[...]
    lane = lax.broadcasted_iota(I32, (tm, LANES), 1)
    lane_f = lane.astype(F32)
    big = float(LANES)
    is_g = (lane >= N_EXPERTS) & (lane < N_EXPERTS + N_GROUPS)
    gl = jnp.where(is_g, logits, -jnp.inf)
    gmax = jnp.max(gl, axis=-1, keepdims=True)
    g_lane = jnp.min(jnp.where(gl == gmax, lane_f, big), axis=-1, keepdims=True)
    g_sum = jnp.sum(jnp.where(is_g, jnp.exp(gl - gmax), 0.0), axis=-1, keepdims=True)
    g_prob = 1.0 / g_sum
    g_idx = g_lane.astype(I32) - N_EXPERTS
    in_grp = (lane < N_EXPERTS) & (lax.shift_right_logical(lane, 3) == g_idx)
    el = jnp.where(in_grp, logits, -jnp.inf)
    t1 = jnp.max(el, axis=-1, keepdims=True)
    i1 = jnp.min(jnp.where(el == t1, lane_f, big), axis=-1, keepdims=True)
    el2 = jnp.where(lane_f == i1, -jnp.inf, el)
    t2 = jnp.max(el2, axis=-1, keepdims=True)
    i2 = jnp.min(jnp.where(el2 == t2, lane_f, big), axis=-1, keepdims=True)
    e2 = jnp.exp(t2 - t1)
    w1 = g_prob * (1.0 / (1.0 + e2))
    w2 = g_prob * (e2 / (1.0 + e2))

    sel1 = lane_f == i1
    sel2 = lane_f == i2
    onehot = jnp.where(sel1 | sel2, 1.0, 0.0)
    r = lax.broadcasted_iota(I32, (tm, tm), 0)
    c = lax.broadcasted_iota(I32, (tm, tm), 1)
    strict = jnp.where(c < r, 1.0, 0.0).astype(BF16)
    prefix = jnp.dot(strict, onehot.astype(BF16), preferred_element_type=F32) + carry_sc[0:1, :]
    rank1 = jnp.sum(jnp.where(sel1, prefix, 0.0), axis=-1, keepdims=True)
    rank2 = jnp.sum(jnp.where(sel2, prefix, 0.0), axis=-1, keepdims=True)
    total = carry_sc[0:1, :] + jnp.sum(onehot, axis=0, keepdims=True)
    carry_sc[...] = jnp.broadcast_to(total, carry_sc.shape)
    cnt_ref[...] = jnp.broadcast_to(total, cnt_ref.shape).astype(I32)

    meta = jnp.where(lane == 0, i1, jnp.where(lane == 1, i2,
                     jnp.where(lane == 2, rank1, jnp.where(lane == 3, rank2, 0.0))))
    meta_ref[...] = meta.astype(I32)
    wts_ref[...] = jnp.where(lane == 0, w1, jnp.where(lane == 1, w2, 0.0))


def _router(x1, g, wr, br, tm=512):
    n = x1.shape[0]
    return pl.pallas_call(
        functools.partial(_router_kernel, tm=tm),
        out_shape=(jax.ShapeDtypeStruct((n, HALF), I32),
                   jax.ShapeDtypeStruct((n, LANES), I32),
                   jax.ShapeDtypeStruct((n, LANES), F32),
                   jax.ShapeDtypeStruct((8, LANES), I32)),
        grid=(n // tm,),
        in_specs=[pl.BlockSpec((tm, D_MODEL), lambda i: (i, 0)),
                  pl.BlockSpec((1, D_MODEL), lambda i: (0, 0)),
                  pl.BlockSpec((D_MODEL, LANES), lambda i: (0, 0)),
                  pl.BlockSpec((1, LANES), lambda i: (0, 0))],
        out_specs=(pl.BlockSpec((tm, HALF), lambda i: (i, 0)),
                   pl.BlockSpec((tm, LANES), lambda i: (i, 0)),
                   pl.BlockSpec((tm, LANES), lambda i: (i, 0)),
                   pl.BlockSpec((8, LANES), lambda i: (0, 0))),
        scratch_shapes=[pltpu.VMEM((8, LANES), F32)],
        compiler_params=_cparams(("arbitrary",)),
        name="router",
    )(x1, g, wr, br)


def _dispatch_kernel(pos_ref, xp_ref, xs_in_ref, xs_ref, sem, *, tm):
    del xs_in_ref
    base = pl.program_id(0) * (tm * TOP_K)

    def row_copy(r, p):
        return pltpu.make_async_copy(xp_ref.at[pl.ds(r, 1), :], xs_ref.at[pl.ds(p, 1), :], sem)

    def issue(r, carry):
        for kk in range(TOP_K):
            row_copy(r, pos_ref[base + TOP_K * r + kk]).start()
        return carry

    def drain(r, carry):
        for kk in range(TOP_K):
            row_copy(r, pos_ref[base + TOP_K * r + kk]).wait()
        return carry

    lax.fori_loop(0, tm, issue, 0)
    lax.fori_loop(0, tm, drain, 0)


def _dispatch(pos, xp, xs_zero, tm=512):
    n = xp.shape[0]
    return pl.pallas_call(
        functools.partial(_dispatch_kernel, tm=tm),
        out_shape=jax.ShapeDtypeStruct(xs_zero.shape, xs_zero.dtype),
        grid_spec=pltpu.PrefetchScalarGridSpec(
            num_scalar_prefetch=1,
            grid=(n // tm,),
            in_specs=[pl.BlockSpec((tm, HALF), lambda i, pos: (i, 0)),
                      pl.BlockSpec(memory_space=pl.ANY)],
            out_specs=pl.BlockSpec(memory_space=pl.ANY),
            scratch_shapes=[pltpu.SemaphoreType.DMA(())]),
        input_output_aliases={2: 0},
        compiler_params=_cparams(("arbitrary",)),
        name="dispatch",
    )(pos, xp, xs_zero)


def _expert_kernel(be_ref, nu_ref, xs_ref, wg_ref, wu_ref, wd_ref, o_ref):
    del be_ref

    @pl.when(pl.program_id(0) < nu_ref[0])
    def _():
        w = xs_ref[...]
        xa = lax.bitcast_convert_type(w & jnp.int32(-65536), F32).astype(BF16)
        xb = lax.bitcast_convert_type(lax.shift_left(w, 16), F32).astype(BF16)
        gate = (jnp.dot(xa, wg_ref[0, :HALF, :], preferred_element_type=F32)
                + jnp.dot(xb, wg_ref[0, HALF:, :], preferred_element_type=F32))
        up = (jnp.dot(xa, wu_ref[0, :HALF, :], preferred_element_type=F32)
              + jnp.dot(xb, wu_ref[0, HALF:, :], preferred_element_type=F32))
        hb = (jax.nn.silu(gate) * up).astype(BF16)
        o_ref[...] = jnp.dot(hb, wd_ref[0], preferred_element_type=F32)

    @pl.when(pl.program_id(0) >= nu_ref[0])
    def _():
        o_ref[...] = jnp.zeros_like(o_ref)


def _experts(blk_e, n_used, xs, w_gate, w_up, w_down):
    p = xs.shape[0]
    bm = EXPERT_BM
    row_blk = lambda nb, be, nu: (jnp.minimum(nb, nu[0] - 1), 0)
    w_blk = lambda nb, be, nu: (be[nb], 0, 0)
    return pl.pallas_call(
        _expert_kernel,
        out_shape=jax.ShapeDtypeStruct((p, D_MODEL), F32),
        grid_spec=pltpu.PrefetchScalarGridSpec(
            num_scalar_prefetch=2,
            grid=(p // bm,),
            in_specs=[pl.BlockSpec((bm, HALF), row_blk),
                      pl.BlockSpec((1, D_MODEL, D_EXPERT), w_blk),
                      pl.BlockSpec((1, D_MODEL, D_EXPERT), w_blk),
                      pl.BlockSpec((1, D_EXPERT, D_MODEL), w_blk)],
            out_specs=pl.BlockSpec((bm, D_MODEL), lambda nb, be, nu: (nb, 0))),
        compiler_params=_cparams(("arbitrary",)),
        name="experts",
    )(blk_e, n_used, xs, w_gate, w_up, w_down)


def _combine_kernel(pos_ref, x_ref, wts_ref, g_ref, os_ref, y_ref, g0_sc, g1_sc, sem, *, tm):
    base = pl.program_id(0) * (tm * TOP_K)
    bufs = (g0_sc, g1_sc)

    def row_copy(r, p, kk):
        return pltpu.make_async_copy(os_ref.at[pl.ds(p, 1), :], bufs[kk].at[pl.ds(r, 1), :], sem)

    def issue(r, carry):
        for kk in range(TOP_K):
            row_copy(r, pos_ref[base + TOP_K * r + kk], kk).start()
        return carry

    def drain(r, carry):
        for kk in range(TOP_K):
            row_copy(r, pos_ref[base + TOP_K * r + kk], kk).wait()
        return carry

    lax.fori_loop(0, tm, issue, 0)
    lax.fori_loop(0, tm, drain, 0)
    moe = wts_ref[:, 0:1] * g0_sc[...] + wts_ref[:, 1:2] * g1_sc[...]
    y_ref[...] = _rms(x_ref[...] + moe, g_ref[...])


def _combine(pos, x1, wts, g, out_sorted, tm=256):
    n = x1.shape[0]
    return pl.pallas_call(
        functools.partial(_combine_kernel, tm=tm),
        out_shape=jax.ShapeDtypeStruct((n, D_MODEL), F32),
        grid_spec=pltpu.PrefetchScalarGridSpec(
            num_scalar_prefetch=1,
            grid=(n // tm,),
            in_specs=[pl.BlockSpec((tm, D_MODEL), lambda i, pos: (i, 0)),
                      pl.BlockSpec((tm, LANES), lambda i, pos: (i, 0)),
                      pl.BlockSpec((1, D_MODEL), lambda i, pos: (0, 0)),
                      pl.BlockSpec(memory_space=pl.ANY)],
            out_specs=pl.BlockSpec((tm, D_MODEL), lambda i, pos: (i, 0)),
            scratch_shapes=[pltpu.VMEM((tm, D_MODEL), F32),
                            pltpu.VMEM((tm, D_MODEL), F32),
                            pltpu.SemaphoreType.DMA(())]),
        compiler_params=_cparams(("arbitrary",)),
        name="combine",
    )(pos, x1, wts, g, out_sorted)


def _moe(x1, norm_g, wr_g, br_g, wr_e, br_e, w_gate, w_up, w_down, final_g):
    n = x1.shape[0]
    pad = LANES - N_EXPERTS - N_GROUPS
    wr = jnp.concatenate([wr_e, wr_g, jnp.zeros((D_MODEL, pad), F32)], axis=1)
    br = jnp.concatenate([br_e, br_g, jnp.zeros((pad,), F32)])[None]
    xp, meta, wts, counts = _router(x1, norm_g[None], wr, br)

    bm = EXPERT_BM
    nb_max = -(-(n * TOP_K + N_EXPERTS * (bm - 1)) // bm)
    cnt = counts[0, :N_EXPERTS]
    nblk = (cnt + bm - 1) // bm
    blk_end = jnp.cumsum(nblk)
    pad_start = (blk_end - nblk) * bm
    n_used = blk_end[-1:]
    blk_ids = jnp.arange(nb_max, dtype=I32)
    blk_e = jnp.minimum(jnp.sum(blk_ids[:, None] >= blk_end[None, :], axis=1), N_EXPERTS - 1)
    blk_e = jnp.where(blk_ids < n_used, blk_e, blk_e[n_used[0] - 1]).astype(I32)
    pos = (pad_start[meta[:, 0:TOP_K]] + meta[:, TOP_K:2 * TOP_K]).reshape(-1).astype(I32)

    xs = _dispatch(pos, xp, jnp.zeros((nb_max * bm, HALF), I32))
    out_sorted = _experts(blk_e, n_used.astype(I32), xs, w_gate.astype(BF16), w_up.astype(BF16),
                          w_down.astype(BF16))
    return _combine(pos, x1, wts, final_g[None], out_sorted)


def kernel(x, norm_mix_g, w_in, gmlp_v_norm_g, gmlp_spatial_w, gmlp_spatial_b, mlstm_igate_b,
           mlstm_fgate_b, mlstm_out_norm_g, w_out, norm_ffn_g, router_group_w, router_group_b,
           router_expert_w, router_expert_b, expert_w_gate, expert_w_up, expert_w_down, final_norm_g):
    batch, seq, d = x.shape
    n = batch * seq
    assert w_in.shape[0] == 1, "the final rmsnorm is fused after the single layer's MoE"
    l = 0
    x2 = x.reshape(n, d)
    w_main = w_in[l][:, :D_PROJ_MAIN].astype(BF16)
    w_gcol = jnp.pad(w_in[l][:, D_PROJ_MAIN:], ((0, 0), (0, LANES - N_GATE_COLS))).astype(BF16)
    proj, gate_pre = _in_proj(x2, norm_mix_g[l][None], w_main, w_gcol)
    y_a = _gmlp(proj, gmlp_v_norm_g[l][None], gmlp_spatial_w[l], gmlp_spatial_b[l].T)
    gate_b = jnp.concatenate([mlstm_igate_b[l], mlstm_fgate_b[l],
                              jnp.zeros((LANES - N_GATE_COLS,), F32)])[None]
    y_b = _mlstm(proj, gate_pre, gate_b, mlstm_out_norm_g[l][None], batch, seq)
    x1 = _out_proj(y_a, y_b, w_out[l].astype(BF16), x2)
    out = _moe(x1, norm_ffn_g[l], router_group_w[l], router_group_b[l], router_expert_w[l],
               router_expert_b[l], expert_w_gate[l], expert_w_up[l], expert_w_down[l], final_norm_g)
    return out.reshape(batch, seq, d)
```

```python
import functools

import jax
import jax.numpy as jnp
from jax import lax
from jax.experimental import pallas as pl
from jax.experimental.pallas import tpu as pltpu

F32 = jnp.float32
BF16 = jnp.bfloat16
I32 = jnp.int32

D_MODEL = 4096
D_GMLP = 2048
GMLP_CHUNK = 128
GMLP_GROUPS = 16
D_MLSTM = 2048
MLSTM_HEADS = 4
MLSTM_DV = 512
MLSTM_DQK = 256
GATE_SOFTCAP = 15.0
N_GROUPS = 8
EXPERTS_PER_GROUP = 8
N_EXPERTS = 64
TOP_K = 2
D_EXPERT = 512
EPS = 1e-6
D_PROJ_MAIN = 10240
N_GATE_COLS = 8

LANES = 128
SUBLANES = 8
VMEM_LIMIT = 56 * 1024 * 1024
MLSTM_L = 256
HALF = D_MODEL // 2
N_ASSIGN = 4 * 4096 * TOP_K
R_SUB = 256
N_SUB = 3
NSB_MAX = -(-(N_ASSIGN + N_EXPERTS * (R_SUB - 1)) // R_SUB)
NE_MAX = N_EXPERTS + N_ASSIGN // (R_SUB * N_SUB)
GU_CHUNK = 256
N_A = D_EXPERT // GU_CHUNK
DN_CHUNK = 1024
N_B = D_MODEL // DN_CHUNK


def _cparams(sem):
    return pltpu.CompilerParams(dimension_semantics=sem, vmem_limit_bytes=VMEM_LIMIT)


def _rms(x, g):
    return x * lax.rsqrt(jnp.mean(x * x, axis=-1, keepdims=True) + EPS) * g


def _in_proj_kernel(x_ref, g_ref, w_ref, wg_ref, proj_ref, gate_ref, xn_sc):
    @pl.when(pl.program_id(1) == 0)
    def _():
        xb = _rms(x_ref[...], g_ref[...]).astype(BF16)
        xn_sc[...] = xb
        gate_ref[...] = jnp.dot(xb, wg_ref[...], preferred_element_type=F32)

    proj_ref[...] = jnp.dot(xn_sc[...], w_ref[...], preferred_element_type=F32).astype(BF16)


def _in_proj(x2, g, w_main, w_gate, tm=512, tn=1024):
    n = x2.shape[0]
    return pl.pallas_call(
        _in_proj_kernel,
        out_shape=(jax.ShapeDtypeStruct((n, D_PROJ_MAIN), BF16),
                   jax.ShapeDtypeStruct((n, LANES), F32)),
        grid=(n // tm, D_PROJ_MAIN // tn),
        in_specs=[pl.BlockSpec((tm, D_MODEL), lambda i, j: (i, 0)),
                  pl.BlockSpec((1, D_MODEL), lambda i, j: (0, 0)),
                  pl.BlockSpec((D_MODEL, tn), lambda i, j: (0, j)),
                  pl.BlockSpec((D_MODEL, LANES), lambda i, j: (0, 0))],
        out_specs=(pl.BlockSpec((tm, tn), lambda i, j: (i, j)),
                   pl.BlockSpec((tm, LANES), lambda i, j: (i, 0))),
        scratch_shapes=[pltpu.VMEM((tm, D_MODEL), BF16)],
        compiler_params=_cparams(("parallel", "arbitrary")),
        name="in_proj",
    )(x2, g, w_main, w_gate)


def _gmlp_kernel(u_ref, v_ref, gv_ref, w_ref, bt_ref, y_ref, wm_sc, *, rows):
    c = GMLP_CHUNK

    @pl.when(pl.program_id(0) == 0)
    def _():
        r = lax.broadcasted_iota(I32, (c, c), 0)
        s = lax.broadcasted_iota(I32, (c, c), 1)
        for g in range(GMLP_GROUPS):
            wm_sc[g] = jnp.where(s <= r, w_ref[g], 0.0).astype(BF16)

    for ci in range(rows // c):
        rs = slice(ci * c, (ci + 1) * c)
        gu = jax.nn.gelu(u_ref[rs, :].astype(F32))
        gv = jax.nn.gelu(v_ref[rs, :].astype(F32))
        vn = _rms(gv, gv_ref[...]).astype(BF16)
        for g in range(GMLP_GROUPS):
            cs = slice(g * c, (g + 1) * c)
            z = jnp.dot(wm_sc[g], vn[:, cs], preferred_element_type=F32) + bt_ref[:, g:g + 1]
            y_ref[rs, cs] = (gu[:, cs] * z).astype(BF16)


def _gmlp(proj, gv, w_s, b_t, rows=512):
    n = proj.shape[0]
    return pl.pallas_call(
        functools.partial(_gmlp_kernel, rows=rows),
        out_shape=jax.ShapeDtypeStruct((n, D_GMLP), BF16),
        grid=(n // rows,),
        in_specs=[pl.BlockSpec((rows, D_GMLP), lambda i: (i, 0)),
                  pl.BlockSpec((rows, D_GMLP), lambda i: (i, 1)),
                  pl.BlockSpec((1, D_GMLP), lambda i: (0, 0)),
                  pl.BlockSpec((GMLP_GROUPS, GMLP_CHUNK, GMLP_CHUNK), lambda i: (0, 0, 0)),
                  pl.BlockSpec((GMLP_CHUNK, GMLP_GROUPS), lambda i: (0, 0))],
        out_specs=pl.BlockSpec((rows, D_GMLP), lambda i: (i, 0)),
        scratch_shapes=[pltpu.VMEM((GMLP_GROUPS, GMLP_CHUNK, GMLP_CHUNK), BF16)],
        compiler_params=_cparams(("arbitrary",)),
        name="gmlp",
    )(proj, proj, gv, w_s, b_t)


def _mlstm_kernel(q_ref, k_ref, v_ref, o_ref, gp_ref, gb_ref, og_ref, y_ref, c_sc, n_sc, m_sc):
    L, H, DQK, DV = MLSTM_L, MLSTM_HEADS, MLSTM_DQK, MLSTM_DV
    scale = DQK ** -0.5

    @pl.when(pl.program_id(1) == 0)
    def _():
        c_sc[...] = jnp.zeros_like(c_sc)
        n_sc[...] = jnp.zeros_like(n_sc)
        m_sc[...] = jnp.zeros_like(m_sc)

    lane = lax.broadcasted_iota(I32, (L, LANES), 1)
    z = gp_ref[...] + gb_ref[...]
    sc = GATE_SOFTCAP * jnp.tanh(z / GATE_SOFTCAP)
    logf = -(jnp.maximum(-sc, 0.0) + jnp.log1p(jnp.exp(-jnp.abs(sc))))
    row = lax.broadcasted_iota(I32, (L, L), 0)
    col = lax.broadcasted_iota(I32, (L, L), 1)
    causal = col <= row
    bcum = jnp.dot(causal.astype(F32), logf, precision=lax.Precision.HIGHEST,
                   preferred_element_type=F32)
    gcol = jnp.where(lane < H, sc, bcum)
    grow = gcol.T

    for h in range(H):
        ig_col, b_col = gcol[:, h:h + 1], gcol[:, H + h:H + h + 1]
        ig_row, b_row = grow[h:h + 1, :], grow[H + h:H + h + 1, :]
        m = m_sc[h, 0:1, 0:1]
        q = q_ref[:, h * DQK:(h + 1) * DQK]
        k = k_ref[:, h * DQK:(h + 1) * DQK]
        v = v_ref[:, h * DV:(h + 1) * DV]

        d_log = jnp.where(causal, b_col - b_row + ig_row, -jnp.inf)
        inter = b_col + m
        m_row = jnp.maximum(inter, jnp.max(d_log, axis=-1, keepdims=True))
        w_intra = jnp.exp(d_log - m_row)
        w_inter = jnp.exp(inter - m_row)
        s = lax.dot_general(q, k, (((1,), (1,)), ((), ())), preferred_element_type=F32)
        s = s * scale * w_intra
        c_old = c_sc[h]
        n_old = n_sc[h]
        num = (w_inter * jnp.dot(q, c_old.astype(BF16), preferred_element_type=F32)
               + jnp.dot(s.astype(BF16), v, preferred_element_type=F32))
        qn = jnp.sum(q.astype(F32) * n_old, axis=-1, keepdims=True)
        den = w_inter * qn + jnp.sum(s, axis=-1, keepdims=True)
        hh = num / jnp.maximum(jnp.abs(den), jnp.exp(-m_row))

        b_last = b_row[:, L - 1:L]
        tail = b_last - b_col + ig_col
        m_new = jnp.maximum(b_last + m, jnp.max(tail, axis=0, keepdims=True))
        w_tail = jnp.exp(tail - m_new)
        decay = jnp.exp(b_last + m - m_new)
        kw = k.astype(F32) * scale * w_tail
        c_sc[h] = decay * c_old + lax.dot_general(
            kw.astype(BF16), v, (((0,), (0,)), ((), ())), preferred_element_type=F32)
        n_sc[h] = decay * n_old + jnp.sum(kw, axis=0, keepdims=True)
        m_sc[h] = jnp.broadcast_to(m_new, m_sc.shape[1:])

        hn = hh * lax.rsqrt(jnp.mean(hh * hh, axis=-1, keepdims=True) + EPS)
        hn = hn * og_ref[:, h * DV:(h + 1) * DV]
        hn = hn * jax.nn.sigmoid(o_ref[:, h * DV:(h + 1) * DV].astype(F32))
        y_ref[:, h * DV:(h + 1) * DV] = hn.astype(BF16)


def _mlstm(proj, gate_pre, gate_b, out_g, batch, seq):
    n = proj.shape[0]
    L = MLSTM_L
    nc = seq // L
    qk_w = MLSTM_HEADS * MLSTM_DQK
    q_blk, k_blk = 2 * D_GMLP // qk_w, 2 * D_GMLP // qk_w + 1
    v_blk, o_blk = (2 * D_GMLP + 2 * qk_w) // D_MLSTM, (2 * D_GMLP + 2 * qk_w) // D_MLSTM + 1
    rmap = lambda b, c: b * nc + c
    return pl.pallas_call(
        _mlstm_kernel,
        out_shape=jax.ShapeDtypeStruct((n, D_MLSTM), BF16),
        grid=(batch, nc),
        in_specs=[pl.BlockSpec((L, qk_w), lambda b, c: (rmap(b, c), q_blk)),
                  pl.BlockSpec((L, qk_w), lambda b, c: (rmap(b, c), k_blk)),
                  pl.BlockSpec((L, D_MLSTM), lambda b, c: (rmap(b, c), v_blk)),
                  pl.BlockSpec((L, D_MLSTM), lambda b, c: (rmap(b, c), o_blk)),
                  pl.BlockSpec((L, LANES), lambda b, c: (rmap(b, c), 0)),
                  pl.BlockSpec((1, LANES), lambda b, c: (0, 0)),
                  pl.BlockSpec((1, D_MLSTM), lambda b, c: (0, 0))],
        out_specs=pl.BlockSpec((L, D_MLSTM), lambda b, c: (rmap(b, c), 0)),
        scratch_shapes=[pltpu.VMEM((MLSTM_HEADS, MLSTM_DQK, MLSTM_DV), F32),
                        pltpu.VMEM((MLSTM_HEADS, 1, MLSTM_DQK), F32),
                        pltpu.VMEM((MLSTM_HEADS, 8, LANES), F32)],
        compiler_params=_cparams(("parallel", "arbitrary")),
        name="mlstm",
    )(proj, proj, proj, proj, gate_pre, gate_b, out_g)


def _out_proj_kernel(ya_ref, yb_ref, wa_ref, wb_ref, x_ref, o_ref):
    acc = jnp.dot(ya_ref[...], wa_ref[...], preferred_element_type=F32)
    acc = acc + jnp.dot(yb_ref[...], wb_ref[...], preferred_element_type=F32)
    o_ref[...] = x_ref[...] + acc


def _out_proj(y_a, y_b, w_out, x2, tm=512, tn=1024):
    n = x2.shape[0]
    return pl.pallas_call(
        _out_proj_kernel,
        out_shape=jax.ShapeDtypeStruct((n, D_MODEL), F32),
        grid=(n // tm, D_MODEL // tn),
        in_specs=[pl.BlockSpec((tm, D_GMLP), lambda i, j: (i, 0)),
                  pl.BlockSpec((tm, D_MLSTM), lambda i, j: (i, 0)),
                  pl.BlockSpec((D_GMLP, tn), lambda i, j: (0, j)),
                  pl.BlockSpec((D_MLSTM, tn), lambda i, j: (1, j)),
                  pl.BlockSpec((tm, tn), lambda i, j: (i, j))],
        out_specs=pl.BlockSpec((tm, tn), lambda i, j: (i, j)),
        compiler_params=_cparams(("parallel", "arbitrary")),
        name="out_proj",
    )(y_a, y_b, w_out, w_out, x2)


def _router_kernel(x_ref, g_ref, wr_ref, br_ref, xp_ref, meta_ref, wts_ref, cnt_ref, carry_sc, *, tm):
    i = pl.program_id(0)

    @pl.when(i == 0)
    def _():
        carry_sc[...] = jnp.zeros_like(carry_sc)

    xn = _rms(x_ref[...], g_ref[...])
    bits = lax.bitcast_convert_type(xn.astype(BF16).astype(F32), I32)
    xp_ref[...] = bits[:, :HALF] | lax.shift_right_logical(bits[:, HALF:], 16)

    logits = jnp.dot(xn, wr_ref[...], precision=lax.Precision.HIGHEST,
                     preferred_element_type=F32) + br_ref[...]
    lane = lax.broadcasted_iota(I32, (tm, LANES), 1)
    lane_f = lane.astype(F32)
    big = float(LANES)
    is_g = (lane >= N_EXPERTS) & (lane < N_EXPERTS + N_GROUPS)
    gl = jnp.where(is_g, logits, -jnp.inf)
    gmax = jnp.max(gl, axis=-1, keepdims=True)
    g_lane = jnp.min(jnp.where(gl == gmax, lane_f, big), axis=-1, keepdims=True)
    g_sum = jnp.sum(jnp.where(is_g, jnp.exp(gl - gmax), 0.0), axis=-1, keepdims=True)
    g_prob = 1.0 / g_sum
    g_idx = g_lane.astype(I32) - N_EXPERTS
    in_grp = (lane < N_EXPERTS) & (lax.shift_right_logical(lane, 3) == g_idx)
    el = jnp.where(in_grp, logits, -jnp.inf)
    t1 = jnp.max(el, axis=-1, keepdims=True)
    i1 = jnp.min(jnp.where(el == t1, lane_f, big), axis=-1, keepdims=True)
    el2 = jnp.where(lane_f == i1, -jnp.inf, el)
    t2 = jnp.max(el2, axis=-1, keepdims=True)
    i2 = jnp.min(jnp.where(el2 == t2, lane_f, big), axis=-1, keepdims=True)
    e2 = jnp.exp(t2 - t1)
    w1 = g_prob * (1.0 / (1.0 + e2))
    w2 = g_prob * (e2 / (1.0 + e2))

    sel1 = lane_f == i1
    sel2 = lane_f == i2
    onehot = jnp.where(sel1 | sel2, 1.0, 0.0)
    r = lax.broadcasted_iota(I32, (tm, tm), 0)
    c = lax.broadcasted_iota(I32, (tm, tm), 1)
    strict = jnp.where(c < r, 1.0, 0.0).astype(BF16)
    prefix = jnp.dot(strict, onehot.astype(BF16), preferred_element_type=F32) + carry_sc[0:1, :]
    rank1 = jnp.sum(jnp.where(sel1, prefix, 0.0), axis=-1, keepdims=True)
    rank2 = jnp.sum(jnp.where(sel2, prefix, 0.0), axis=-1, keepdims=True)
    total = carry_sc[0:1, :] + jnp.sum(onehot, axis=0, keepdims=True)
    carry_sc[...] = jnp.broadcast_to(total, carry_sc.shape)
    cnt_ref[...] = jnp.broadcast_to(total, cnt_ref.shape).astype(I32)

    meta = jnp.where(lane == 0, i1, jnp.where(lane == 1, i2,
                     jnp.where(lane == 2, rank1, jnp.where(lane == 3, rank2, 0.0))))
    meta_ref[...] = meta.astype(I32)
    wts_ref[...] = jnp.where(lane == 0, w1, jnp.where(lane == 1, w2, 0.0))


def _router(x1, g, wr, br, tm=512):
    n = x1.shape[0]
    return pl.pallas_call(
        functools.partial(_router_kernel, tm=tm),
        out_shape=(jax.ShapeDtypeStruct((n, HALF), I32),
                   jax.ShapeDtypeStruct((n, LANES), I32),
                   jax.ShapeDtypeStruct((n, LANES), F32),
                   jax.ShapeDtypeStruct((8, LANES), I32)),
        grid=(n // tm,),
        in_specs=[pl.BlockSpec((tm, D_MODEL), lambda i: (i, 0)),
                  pl.BlockSpec((1, D_MODEL), lambda i: (0, 0)),
                  pl.BlockSpec((D_MODEL, LANES), lambda i: (0, 0)),
                  pl.BlockSpec((1, LANES), lambda i: (0, 0))],
        out_specs=(pl.BlockSpec((tm, HALF), lambda i: (i, 0)),
                   pl.BlockSpec((tm, LANES), lambda i: (i, 0)),
                   pl.BlockSpec((tm, LANES), lambda i: (i, 0)),
                   pl.BlockSpec((8, LANES), lambda i: (0, 0))),
        scratch_shapes=[pltpu.VMEM((8, LANES), F32)],
        compiler_params=_cparams(("arbitrary",)),
        name="router",
    )(x1, g, wr, br)


def _dispatch_kernel(e_ref, rk_ref, row0_ref, cnt_ref, misc_ref, xp_ref, xs_ref, zbuf, sem, zsem, *, tm):
    base = pl.program_id(0) * (tm * TOP_K)

    @pl.when(pl.program_id(0) == 0)
    def _():
        zbuf[...] = jnp.zeros_like(zbuf)

        def pad_fill(e, carry, *, wait):
            cnt = cnt_ref[e]
            start = row0_ref[e] + cnt
            head = (-cnt) & (SUBLANES - 1)
            for j in range(SUBLANES - 1):
                @pl.when(j < head)
                def _(j=j):
                    cp = pltpu.make_async_copy(zbuf.at[pl.ds(0, 1), :], xs_ref.at[pl.ds(start + j, 1), :], zsem)
                    cp.wait() if wait else cp.start()
            off = start + head
            rest = (-(cnt + head)) & (R_SUB - 1)
            bit = R_SUB // 2
            while bit >= SUBLANES:
                @pl.when((rest & bit) != 0)
                def _(off=off, bit=bit):
                    dst = xs_ref.at[pl.ds(pl.multiple_of(off, SUBLANES), bit), :]
                    cp = pltpu.make_async_copy(zbuf.at[pl.ds(0, bit), :], dst, zsem)
                    cp.wait() if wait else cp.start()
                off = off + (rest & bit)
                bit //= 2
            return carry

        def tail_fill(sb, carry, *, wait):
            r = pl.multiple_of(sb * R_SUB, R_SUB)
            cp = pltpu.make_async_copy(zbuf, xs_ref.at[pl.ds(r, R_SUB), :], zsem)
            cp.wait() if wait else cp.start()
            return carry

        for wait in (False, True):
            lax.fori_loop(0, N_EXPERTS, functools.partial(pad_fill, wait=wait), 0)
            lax.fori_loop(misc_ref[0], NSB_MAX, functools.partial(tail_fill, wait=wait), 0)

    def row_copy(r, kk):
        a = base + TOP_K * r + kk
        p = row0_ref[e_ref[a]] + rk_ref[a]
        return pltpu.make_async_copy(xp_ref.at[pl.ds(r, 1), :], xs_ref.at[pl.ds(p, 1), :], sem)

    def issue(r, carry):
        for kk in range(TOP_K):
            row_copy(r, kk).start()
        return carry

    def drain(r, carry):
        for kk in range(TOP_K):
            row_copy(r, kk).wait()
        return carry

    lax.fori_loop(0, tm, issue, 0)
    lax.fori_loop(0, tm, drain, 0)


def _dispatch(e_flat, rk_flat, row0, cnt, misc, xp, tm=512):
    n = xp.shape[0]
    return pl.pallas_call(
        functools.partial(_dispatch_kernel, tm=tm),
        out_shape=jax.ShapeDtypeStruct((NSB_MAX * R_SUB, HALF), I32),
        grid_spec=pltpu.PrefetchScalarGridSpec(
            num_scalar_prefetch=5,
            grid=(n // tm,),
            in_specs=[pl.BlockSpec((tm, HALF), lambda i, *_: (i, 0))],
            out_specs=pl.BlockSpec(memory_space=pl.ANY),
            scratch_shapes=[pltpu.VMEM((R_SUB, HALF), I32),
                            pltpu.SemaphoreType.DMA(()),
                            pltpu.SemaphoreType.DMA(())]),
        compiler_params=_cparams(("arbitrary",)),
        name="dispatch",
    )(e_flat, rk_flat, row0, cnt, misc, xp)


def _expert_kernel(ee_ref, sb0_ref, nsb_ref, misc_ref, xs_hbm, wg_ref, wu_ref, wd_ref, os_hbm,
                   xbuf, hbuf, obuf, zbuf, wgb, wub, wdb, xsem, osem, zsem):
    s = pl.program_id(0)
    c = pl.program_id(1)
    n_ent, n_ph = pl.num_programs(0), pl.num_programs(1)
    nsb = nsb_ref[s]
    slot = lax.rem(s, 2)

    def sub_row(ent, sb):
        return pl.multiple_of((sb0_ref[ent] + sb) * R_SUB, R_SUB)

    def x_copies(ent, slot_, wait):
        for sb in range(N_SUB):
            @pl.when(sb < nsb_ref[ent])
            def _(sb=sb):
                cp = pltpu.make_async_copy(xs_hbm.at[pl.ds(sub_row(ent, sb), R_SUB), :],
                                           xbuf.at[slot_, sb], xsem.at[slot_])
                cp.wait() if wait else cp.start()

    def o_copies(t, wait, only_sb=None):
        ent, nb, par = t // N_B, lax.rem(t, N_B), lax.rem(t, 2)
        col = pl.multiple_of(nb * DN_CHUNK, DN_CHUNK)
        for sb in range(N_SUB):
            if only_sb is not None and sb != only_sb:
                continue

            def go(sb=sb):
                cp = pltpu.make_async_copy(
                    obuf.at[par, sb],
                    os_hbm.at[pl.ds(sub_row(ent, sb), R_SUB), pl.ds(col, DN_CHUNK)], osem.at[par])
                cp.wait() if wait else cp.start()
            if only_sb is None:
                pl.when(sb < nsb_ref[ent])(go)
            else:
                go()

    @pl.when((s == 0) & (c == 0))
    def _():
        x_copies(0, 0, False)

    @pl.when(c == 0)
    def _():
        x_copies(s, slot, True)

    @pl.when((c == N_A) & (s + 1 < n_ent))
    def _():
        x_copies(s + 1, 1 - slot, False)

    @pl.when((c < N_A) & (nsb > 0))
    def _():
        wgb[...] = wg_ref[0].astype(BF16)
        wub[...] = wu_ref[0].astype(BF16)
        for sb in range(N_SUB):
            @pl.when(sb < nsb)
            def _(sb=sb):
                w = xbuf[slot, sb]
                xa = lax.bitcast_convert_type(w & jnp.int32(-65536), F32).astype(BF16)
                xb = lax.bitcast_convert_type(lax.shift_left(w, 16), F32).astype(BF16)
                gate = (jnp.dot(xa, wgb[:HALF, :], preferred_element_type=F32)
                        + jnp.dot(xb, wgb[HALF:, :], preferred_element_type=F32))
                up = (jnp.dot(xa, wub[:HALF, :], preferred_element_type=F32)
                      + jnp.dot(xb, wub[HALF:, :], preferred_element_type=F32))
                hbuf[c, sb] = (jax.nn.silu(gate) * up).astype(BF16)

    @pl.when(c >= N_A)
    def _():
        t = s * N_B + (c - N_A)
        par = lax.rem(t, 2)

        @pl.when(t >= 2)
        def _():
            o_copies(t - 2, True)

        @pl.when(nsb > 0)
        def _():
            wdb[...] = wd_ref[0].astype(BF16)
            for sb in range(N_SUB):
                @pl.when(sb < nsb)
                def _(sb=sb):
                    acc = jnp.dot(hbuf[0, sb], wdb[0:GU_CHUNK, :], preferred_element_type=F32)
                    for ca in range(1, N_A):
                        acc = acc + jnp.dot(hbuf[ca, sb], wdb[ca * GU_CHUNK:(ca + 1) * GU_CHUNK, :],
                                            preferred_element_type=F32)
                    obuf[par, sb] = acc
                    o_copies(t, False, only_sb=sb)

    @pl.when((s == n_ent - 1) & (c == n_ph - 1))
    def _():
        t = s * N_B + (c - N_A)
        o_copies(t - 1, True)
        o_copies(t, True)
        zbuf[...] = jnp.zeros_like(zbuf)

        def tail_fill(sb, carry, *, wait):
            r = pl.multiple_of(sb * R_SUB, R_SUB)
            for nb in range(N_B):
                cp = pltpu.make_async_copy(
                    zbuf, os_hbm.at[pl.ds(r, R_SUB), pl.ds(nb * DN_CHUNK, DN_CHUNK)], zsem)
                cp.wait() if wait else cp.start()
            return carry

        for wait in (False, True):
            lax.fori_loop(misc_ref[0], NSB_MAX, functools.partial(tail_fill, wait=wait), 0)


def _experts(ent_e, ent_sb0, ent_nsb, misc, xs, w_gate, w_up, w_down):
    def gu_map(s, c, ee, sb0, nsb, misc):
        cur = (c < N_A) & (nsb[s] > 0)
        nxt = jnp.minimum(s + 1, NE_MAX - 1)
        return (jnp.where(c < N_A, ee[s], ee[nxt]), 0, jnp.where(cur, c, 0))

    def dn_map(s, c, ee, sb0, nsb, misc):
        return (ee[s], 0, jnp.where(nsb[s] > 0, jnp.clip(c - N_A, 0, N_B - 1), 0))

    return pl.pallas_call(
        _expert_kernel,
        out_shape=jax.ShapeDtypeStruct((NSB_MAX * R_SUB, D_MODEL), F32),
        grid_spec=pltpu.PrefetchScalarGridSpec(
            num_scalar_prefetch=4,
            grid=(NE_MAX, N_A + N_B),
            in_specs=[pl.BlockSpec(memory_space=pl.ANY),
                      pl.BlockSpec((1, D_MODEL, GU_CHUNK), gu_map),
                      pl.BlockSpec((1, D_MODEL, GU_CHUNK), gu_map),
                      pl.BlockSpec((1, D_EXPERT, DN_CHUNK), dn_map)],
            out_specs=pl.BlockSpec(memory_space=pl.ANY),
            scratch_shapes=[pltpu.VMEM((2, N_SUB, R_SUB, HALF), I32),
                            pltpu.VMEM((N_A, N_SUB, R_SUB, GU_CHUNK), BF16),
                            pltpu.VMEM((2, N_SUB, R_SUB, DN_CHUNK), F32),
                            pltpu.VMEM((R_SUB, DN_CHUNK), F32),
                            pltpu.VMEM((D_MODEL, GU_CHUNK), BF16),
                            pltpu.VMEM((D_MODEL, GU_CHUNK), BF16),
                            pltpu.VMEM((D_EXPERT, DN_CHUNK), BF16),
                            pltpu.SemaphoreType.DMA((2,)),
                            pltpu.SemaphoreType.DMA((2,)),
                            pltpu.SemaphoreType.DMA(())]),
        compiler_params=_cparams(("arbitrary", "arbitrary")),
        name="experts",
    )(ent_e, ent_sb0, ent_nsb, misc, xs, w_gate, w_up, w_down)


def _combine_kernel(e_ref, rk_ref, row0_ref, x_ref, wts_ref, g_ref, os_ref, y_ref, g0_sc, g1_sc, sem, *, tm):
    base = pl.program_id(0) * (tm * TOP_K)
    bufs = (g0_sc, g1_sc)

    def row_copy(r, kk):
        a = base + TOP_K * r + kk
        p = row0_ref[e_ref[a]] + rk_ref[a]
        return pltpu.make_async_copy(os_ref.at[pl.ds(p, 1), :], bufs[kk].at[pl.ds(r, 1), :], sem)

    def issue(r, carry):
        for kk in range(TOP_K):
            row_copy(r, kk).start()
        return carry

    def drain(r, carry):
        for kk in range(TOP_K):
            row_copy(r, kk).wait()
        return carry

    lax.fori_loop(0, tm, issue, 0)
    lax.fori_loop(0, tm, drain, 0)
    moe = wts_ref[:, 0:1] * g0_sc[...] + wts_ref[:, 1:2] * g1_sc[...]
    y_ref[...] = _rms(x_ref[...] + moe, g_ref[...])


def _combine(e_flat, rk_flat, row0, x1, wts, g, out_sorted, tm=256):
    n = x1.shape[0]
    return pl.pallas_call(
        functools.partial(_combine_kernel, tm=tm),
        out_shape=jax.ShapeDtypeStruct((n, D_MODEL), F32),
        grid_spec=pltpu.PrefetchScalarGridSpec(
            num_scalar_prefetch=3,
            grid=(n // tm,),
            in_specs=[pl.BlockSpec((tm, D_MODEL), lambda i, *_: (i, 0)),
                      pl.BlockSpec((tm, LANES), lambda i, *_: (i, 0)),
                      pl.BlockSpec((1, D_MODEL), lambda i, *_: (0, 0)),
                      pl.BlockSpec(memory_space=pl.ANY)],
            out_specs=pl.BlockSpec((tm, D_MODEL), lambda i, *_: (i, 0)),
            scratch_shapes=[pltpu.VMEM((tm, D_MODEL), F32),
                            pltpu.VMEM((tm, D_MODEL), F32),
                            pltpu.SemaphoreType.DMA(())]),
        compiler_params=_cparams(("arbitrary",)),
        name="combine",
    )(e_flat, rk_flat, row0, x1, wts, g, out_sorted)


def _moe(x1, norm_g, wr_g, br_g, wr_e, br_e, w_gate, w_up, w_down, final_g):
    n = x1.shape[0]
    pad = LANES - N_EXPERTS - N_GROUPS
    wr = jnp.concatenate([wr_e, wr_g, jnp.zeros((D_MODEL, pad), F32)], axis=1)
    br = jnp.concatenate([br_e, br_g, jnp.zeros((pad,), F32)])[None]
    xp, meta, wts, counts = _router(x1, norm_g[None], wr, br)

    assert n * TOP_K == N_ASSIGN
    cnt = counts[0, :N_EXPERTS]
    nsb = (cnt + R_SUB - 1) // R_SUB
    sb_end = jnp.cumsum(nsb)
    sb_start = sb_end - nsb
    nent = (nsb + N_SUB - 1) // N_SUB
    ent_end = jnp.cumsum(nent)
    ent_start = ent_end - nent
    ids = jnp.arange(NE_MAX, dtype=I32)
    valid = ids < ent_end[-1]
    ent_e = jnp.minimum(jnp.sum(ids[:, None] >= ent_end[None, :], axis=1), N_EXPERTS - 1)
    ent_e = jnp.where(valid, ent_e, ent_e[ent_end[-1] - 1])
    within = (ids - ent_start[ent_e]) * N_SUB
    ent_sb0 = jnp.where(valid, sb_start[ent_e] + within, 0)
    ent_nsb = jnp.where(valid, jnp.clip(nsb[ent_e] - within, 0, N_SUB), 0)
    row0 = (sb_start * R_SUB).astype(I32)
    misc = sb_end[-1:].astype(I32)
    e_flat = meta[:, 0:TOP_K].reshape(-1)
    rk_flat = meta[:, TOP_K:2 * TOP_K].reshape(-1)

    xs = _dispatch(e_flat, rk_flat, row0, cnt, misc, xp)
    out_sorted = _experts(ent_e.astype(I32), ent_sb0.astype(I32), ent_nsb.astype(I32), misc, xs,
                          w_gate, w_up, w_down)
    return _combine(e_flat, rk_flat, row0, x1, wts, final_g[None], out_sorted)


def kernel(x, norm_mix_g, w_in, gmlp_v_norm_g, gmlp_spatial_w, gmlp_spatial_b, mlstm_igate_b,
           mlstm_fgate_b, mlstm_out_norm_g, w_out, norm_ffn_g, router_group_w, router_group_b,
           router_expert_w, router_expert_b, expert_w_gate, expert_w_up, expert_w_down, final_norm_g):
    batch, seq, d = x.shape
    n = batch * seq
    assert w_in.shape[0] == 1, "the final rmsnorm is fused after the single layer's MoE"
    l = 0
    x2 = x.reshape(n, d)
    w_main = w_in[l].astype(BF16)
    w_gcol = jnp.pad(w_in[l][:, D_PROJ_MAIN:], ((0, 0), (0, LANES - N_GATE_COLS))).astype(BF16)
    proj, gate_pre = _in_proj(x2, norm_mix_g[l][None], w_main, w_gcol)
    y_a = _gmlp(proj, gmlp_v_norm_g[l][None], gmlp_spatial_w[l], gmlp_spatial_b[l].T)
    gate_b = jnp.concatenate([mlstm_igate_b[l], mlstm_fgate_b[l],
                              jnp.zeros((LANES - N_GATE_COLS,), F32)])[None]
    y_b = _mlstm(proj, gate_pre, gate_b, mlstm_out_norm_g[l][None], batch, seq)
    x1 = _out_proj(y_a, y_b, w_out[l].astype(BF16), x2)
    out = _moe(x1, norm_ffn_g[l], router_group_w[l], router_group_b[l], router_expert_w[l],
               router_expert_b[l], expert_w_gate[l], expert_w_up[l], expert_w_down[l], final_norm_g)
    return out.reshape(batch, seq, d)
```

```python
import functools

import jax
import jax.numpy as jnp
from jax import lax
from jax.experimental import pallas as pl
from jax.experimental.pallas import tpu as pltpu

F32 = jnp.float32
BF16 = jnp.bfloat16
I32 = jnp.int32

D_MODEL = 4096
D_GMLP = 2048
GMLP_CHUNK = 128
GMLP_GROUPS = 16
D_MLSTM = 2048
MLSTM_HEADS = 4
MLSTM_DV = 512
MLSTM_DQK = 256
GATE_SOFTCAP = 15.0
N_GROUPS = 8
EXPERTS_PER_GROUP = 8
N_EXPERTS = 64
TOP_K = 2
D_EXPERT = 512
EPS = 1e-6
D_PROJ_MAIN = 10240
N_GATE_COLS = 8

LANES = 128
SUBLANES = 8
VMEM_LIMIT = 56 * 1024 * 1024
MLSTM_L = 256
HALF = D_MODEL // 2
N_ASSIGN = 4 * 4096 * TOP_K
R_SUB = 256
N_SUB = 3
NSB_MAX = -(-(N_ASSIGN + N_EXPERTS * (R_SUB - 1)) // R_SUB)
NE_MAX = N_EXPERTS + N_ASSIGN // (R_SUB * N_SUB)
GU_CHUNK = 256
N_A = D_EXPERT // GU_CHUNK
DN_CHUNK = 1024
N_B = D_MODEL // DN_CHUNK


def _cparams(sem):
    return pltpu.CompilerParams(dimension_semantics=sem, vmem_limit_bytes=VMEM_LIMIT)


def _rms(x, g):
    return x * lax.rsqrt(jnp.mean(x * x, axis=-1, keepdims=True) + EPS) * g


def _in_proj_kernel(x_ref, g_ref, w_ref, wg_ref, proj_ref, gate_ref, xn_sc):
    @pl.when(pl.program_id(1) == 0)
    def _():
        xb = _rms(x_ref[...], g_ref[...]).astype(BF16)
        xn_sc[...] = xb
        gate_ref[...] = jnp.dot(xb, wg_ref[...], preferred_element_type=F32)

    proj_ref[...] = jnp.dot(xn_sc[...], w_ref[...], preferred_element_type=F32).astype(BF16)


def _in_proj(x2, g, w_main, w_gate, tm=512, tn=1024):
    n = x2.shape[0]
    return pl.pallas_call(
        _in_proj_kernel,
        out_shape=(jax.ShapeDtypeStruct((n, D_PROJ_MAIN), BF16),
                   jax.ShapeDtypeStruct((n, LANES), F32)),
        grid=(n // tm, D_PROJ_MAIN // tn),
        in_specs=[pl.BlockSpec((tm, D_MODEL), lambda i, j: (i, 0)),
                  pl.BlockSpec((1, D_MODEL), lambda i, j: (0, 0)),
                  pl.BlockSpec((D_MODEL, tn), lambda i, j: (0, j)),
                  pl.BlockSpec((D_MODEL, LANES), lambda i, j: (0, 0))],
        out_specs=(pl.BlockSpec((tm, tn), lambda i, j: (i, j)),
                   pl.BlockSpec((tm, LANES), lambda i, j: (i, 0))),
        scratch_shapes=[pltpu.VMEM((tm, D_MODEL), BF16)],
        compiler_params=_cparams(("parallel", "arbitrary")),
        name="in_proj",
    )(x2, g, w_main, w_gate)


def _gmlp_kernel(u_ref, v_ref, gv_ref, w_ref, bt_ref, y_ref, wm_sc, *, rows):
    c = GMLP_CHUNK

    @pl.when(pl.program_id(0) == 0)
    def _():
        r = lax.broadcasted_iota(I32, (c, c), 0)
        s = lax.broadcasted_iota(I32, (c, c), 1)
        for g in range(GMLP_GROUPS):
            wm_sc[g] = jnp.where(s <= r, w_ref[g], 0.0).astype(BF16)

    for ci in range(rows // c):
        rs = slice(ci * c, (ci + 1) * c)
        gu = jax.nn.gelu(u_ref[rs, :].astype(F32))
        gv = jax.nn.gelu(v_ref[rs, :].astype(F32))
        vn = _rms(gv, gv_ref[...]).astype(BF16)
        for g in range(GMLP_GROUPS):
            cs = slice(g * c, (g + 1) * c)
            z = jnp.dot(wm_sc[g], vn[:, cs], preferred_element_type=F32) + bt_ref[:, g:g + 1]
            y_ref[rs, cs] = (gu[:, cs] * z).astype(BF16)


def _gmlp(proj, gv, w_s, b_t, rows=512):
    n = proj.shape[0]
    return pl.pallas_call(
        functools.partial(_gmlp_kernel, rows=rows),
        out_shape=jax.ShapeDtypeStruct((n, D_GMLP), BF16),
        grid=(n // rows,),
        in_specs=[pl.BlockSpec((rows, D_GMLP), lambda i: (i, 0)),
                  pl.BlockSpec((rows, D_GMLP), lambda i: (i, 1)),
                  pl.BlockSpec((1, D_GMLP), lambda i: (0, 0)),
                  pl.BlockSpec((GMLP_GROUPS, GMLP_CHUNK, GMLP_CHUNK), lambda i: (0, 0, 0)),
                  pl.BlockSpec((GMLP_CHUNK, GMLP_GROUPS), lambda i: (0, 0))],
        out_specs=pl.BlockSpec((rows, D_GMLP), lambda i: (i, 0)),
        scratch_shapes=[pltpu.VMEM((GMLP_GROUPS, GMLP_CHUNK, GMLP_CHUNK), BF16)],
        compiler_params=_cparams(("arbitrary",)),
        name="gmlp",
    )(proj, proj, gv, w_s, b_t)


def _mlstm_kernel(q_ref, k_ref, v_ref, o_ref, gp_ref, gb_ref, og_ref, y_ref, c_sc, n_sc, m_sc):
    L, H, DQK, DV = MLSTM_L, MLSTM_HEADS, MLSTM_DQK, MLSTM_DV
    scale = DQK ** -0.5

    @pl.when(pl.program_id(1) == 0)
    def _():
        c_sc[...] = jnp.zeros_like(c_sc)
        n_sc[...] = jnp.zeros_like(n_sc)
        m_sc[...] = jnp.zeros_like(m_sc)

    lane = lax.broadcasted_iota(I32, (L, LANES), 1)
    z = gp_ref[...] + gb_ref[...]
    sc = GATE_SOFTCAP * jnp.tanh(z / GATE_SOFTCAP)
    logf = -(jnp.maximum(-sc, 0.0) + jnp.log1p(jnp.exp(-jnp.abs(sc))))
    row = lax.broadcasted_iota(I32, (L, L), 0)
    col = lax.broadcasted_iota(I32, (L, L), 1)
    causal = col <= row
    tri = jnp.where(causal, 1.0, 0.0).astype(BF16)
    bcum, rest = None, logf
    for _ in range(3):
        part = rest.astype(BF16)
        rest = rest - part.astype(F32)
        term = jnp.dot(tri, part, preferred_element_type=F32)
        bcum = term if bcum is None else bcum + term
    gcol = jnp.where(lane < H, sc, bcum)
    grow = gcol.T

    for h in range(H):
        ig_col, b_col = gcol[:, h:h + 1], gcol[:, H + h:H + h + 1]
        ig_row, b_row = grow[h:h + 1, :], grow[H + h:H + h + 1, :]
        m = m_sc[h, 0:1, 0:1]
        q = q_ref[:, h * DQK:(h + 1) * DQK]
        k = k_ref[:, h * DQK:(h + 1) * DQK]
        v = v_ref[:, h * DV:(h + 1) * DV]

        d_log = jnp.where(causal, b_col - b_row + ig_row, -jnp.inf)
        inter = b_col + m
        m_row = jnp.maximum(inter, jnp.max(d_log, axis=-1, keepdims=True))
        w_intra = jnp.exp(d_log - m_row)
        w_inter = jnp.exp(inter - m_row)
        s = lax.dot_general(q, k, (((1,), (1,)), ((), ())), preferred_element_type=F32)
        s = s * scale * w_intra
        c_old = c_sc[h]
        n_old = n_sc[h]
        num = (w_inter * jnp.dot(q, c_old.astype(BF16), preferred_element_type=F32)
               + jnp.dot(s.astype(BF16), v, preferred_element_type=F32))
        qn = jnp.sum(q.astype(F32) * n_old, axis=-1, keepdims=True)
        den = w_inter * qn + jnp.sum(s, axis=-1, keepdims=True)
        hh = num / jnp.maximum(jnp.abs(den), jnp.exp(-m_row))

        b_last = b_row[:, L - 1:L]
        tail = b_last - b_col + ig_col
        m_new = jnp.maximum(b_last + m, jnp.max(tail, axis=0, keepdims=True))
        w_tail = jnp.exp(tail - m_new)
        decay = jnp.exp(b_last + m - m_new)
        kw = k.astype(F32) * scale * w_tail
        c_sc[h] = decay * c_old + lax.dot_general(
            kw.astype(BF16), v, (((0,), (0,)), ((), ())), preferred_element_type=F32)
        n_sc[h] = decay * n_old + jnp.sum(kw, axis=0, keepdims=True)
        m_sc[h] = jnp.broadcast_to(m_new, m_sc.shape[1:])

        hn = hh * lax.rsqrt(jnp.mean(hh * hh, axis=-1, keepdims=True) + EPS)
        hn = hn * og_ref[:, h * DV:(h + 1) * DV]
        hn = hn * jax.nn.sigmoid(o_ref[:, h * DV:(h + 1) * DV].astype(F32))
        y_ref[:, h * DV:(h + 1) * DV] = hn.astype(BF16)


def _mlstm(proj, gate_pre, gate_b, out_g, batch, seq):
    n = proj.shape[0]
    L = MLSTM_L
    nc = seq // L
    qk_w = MLSTM_HEADS * MLSTM_DQK
    q_blk, k_blk = 2 * D_GMLP // qk_w, 2 * D_GMLP // qk_w + 1
    v_blk, o_blk = (2 * D_GMLP + 2 * qk_w) // D_MLSTM, (2 * D_GMLP + 2 * qk_w) // D_MLSTM + 1
    rmap = lambda b, c: b * nc + c
    return pl.pallas_call(
        _mlstm_kernel,
        out_shape=jax.ShapeDtypeStruct((n, D_MLSTM), BF16),
        grid=(batch, nc),
        in_specs=[pl.BlockSpec((L, qk_w), lambda b, c: (rmap(b, c), q_blk)),
                  pl.BlockSpec((L, qk_w), lambda b, c: (rmap(b, c), k_blk)),
                  pl.BlockSpec((L, D_MLSTM), lambda b, c: (rmap(b, c), v_blk)),
                  pl.BlockSpec((L, D_MLSTM), lambda b, c: (rmap(b, c), o_blk)),
                  pl.BlockSpec((L, LANES), lambda b, c: (rmap(b, c), 0)),
                  pl.BlockSpec((1, LANES), lambda b, c: (0, 0)),
                  pl.BlockSpec((1, D_MLSTM), lambda b, c: (0, 0))],
        out_specs=pl.BlockSpec((L, D_MLSTM), lambda b, c: (rmap(b, c), 0)),
        scratch_shapes=[pltpu.VMEM((MLSTM_HEADS, MLSTM_DQK, MLSTM_DV), F32),
                        pltpu.VMEM((MLSTM_HEADS, 1, MLSTM_DQK), F32),
                        pltpu.VMEM((MLSTM_HEADS, 8, LANES), F32)],
        compiler_params=_cparams(("parallel", "arbitrary")),
        name="mlstm",
    )(proj, proj, proj, proj, gate_pre, gate_b, out_g)


def _out_proj_kernel(ya_ref, yb_ref, wa_ref, wb_ref, x_ref, o_ref):
    acc = jnp.dot(ya_ref[...], wa_ref[...], preferred_element_type=F32)
    acc = acc + jnp.dot(yb_ref[...], wb_ref[...], preferred_element_type=F32)
    o_ref[...] = x_ref[...] + acc


def _out_proj(y_a, y_b, w_out, x2, tm=512, tn=1024):
    n = x2.shape[0]
    return pl.pallas_call(
        _out_proj_kernel,
        out_shape=jax.ShapeDtypeStruct((n, D_MODEL), F32),
        grid=(n // tm, D_MODEL // tn),
        in_specs=[pl.BlockSpec((tm, D_GMLP), lambda i, j: (i, 0)),
                  pl.BlockSpec((tm, D_MLSTM), lambda i, j: (i, 0)),
                  pl.BlockSpec((D_GMLP, tn), lambda i, j: (0, j)),
                  pl.BlockSpec((D_MLSTM, tn), lambda i, j: (1, j)),
                  pl.BlockSpec((tm, tn), lambda i, j: (i, j))],
        out_specs=pl.BlockSpec((tm, tn), lambda i, j: (i, j)),
        compiler_params=_cparams(("parallel", "arbitrary")),
        name="out_proj",
    )(y_a, y_b, w_out, w_out, x2)


def _router_kernel(x_ref, g_ref, wr_ref, br_ref, xp_ref, meta_ref, wts_ref, cnt_ref, carry_sc, w2_sc,
                   *, tm):
    i = pl.program_id(0)

    @pl.when(i == 0)
    def _():
        carry_sc[...] = jnp.zeros_like(carry_sc)
        w_hi = wr_ref[...].astype(BF16)
        w2_sc[:, :LANES] = w_hi
        w2_sc[:, LANES:] = (wr_ref[...] - w_hi.astype(F32)).astype(BF16)

    xn = _rms(x_ref[...], g_ref[...])
    bits = lax.bitcast_convert_type(xn.astype(BF16).astype(F32), I32)
    xp_ref[...] = bits[:, :HALF] | lax.shift_right_logical(bits[:, HALF:], 16)

    x_hi = xn.astype(BF16)
    x_lo = (xn - x_hi.astype(F32)).astype(BF16)
    both = jnp.dot(x_hi, w2_sc[...], preferred_element_type=F32)
    logits = (both[:, :LANES] + both[:, LANES:]
              + jnp.dot(x_lo, w2_sc[:, :LANES], preferred_element_type=F32)) + br_ref[...]
    lane = lax.broadcasted_iota(I32, (tm, LANES), 1)
    lane_f = lane.astype(F32)
    big = float(LANES)
    is_g = (lane >= N_EXPERTS) & (lane < N_EXPERTS + N_GROUPS)
    gl = jnp.where(is_g, logits, -jnp.inf)
    gmax = jnp.max(gl, axis=-1, keepdims=True)
    g_lane = jnp.min(jnp.where(gl == gmax, lane_f, big), axis=-1, keepdims=True)
    g_sum = jnp.sum(jnp.where(is_g, jnp.exp(gl - gmax), 0.0), axis=-1, keepdims=True)
    g_prob = 1.0 / g_sum
    g_idx = g_lane.astype(I32) - N_EXPERTS
    in_grp = (lane < N_EXPERTS) & (lax.shift_right_logical(lane, 3) == g_idx)
    el = jnp.where(in_grp, logits, -jnp.inf)
    t1 = jnp.max(el, axis=-1, keepdims=True)
    i1 = jnp.min(jnp.where(el == t1, lane_f, big), axis=-1, keepdims=True)
    el2 = jnp.where(lane_f == i1, -jnp.inf, el)
    t2 = jnp.max(el2, axis=-1, keepdims=True)
    i2 = jnp.min(jnp.where(el2 == t2, lane_f, big), axis=-1, keepdims=True)
    e2 = jnp.exp(t2 - t1)
    w1 = g_prob * (1.0 / (1.0 + e2))
    w2 = g_prob * (e2 / (1.0 + e2))

    sel1 = lane_f == i1
    sel2 = lane_f == i2
    onehot = jnp.where(sel1 | sel2, 1.0, 0.0)
    r = lax.broadcasted_iota(I32, (tm, tm), 0)
    c = lax.broadcasted_iota(I32, (tm, tm), 1)
    strict = jnp.where(c < r, 1.0, 0.0).astype(BF16)
    prefix = jnp.dot(strict, onehot.astype(BF16), preferred_element_type=F32) + carry_sc[0:1, :]
    rank1 = jnp.sum(jnp.where(sel1, prefix, 0.0), axis=-1, keepdims=True)
    rank2 = jnp.sum(jnp.where(sel2, prefix, 0.0), axis=-1, keepdims=True)
    total = carry_sc[0:1, :] + jnp.sum(onehot, axis=0, keepdims=True)
    carry_sc[...] = jnp.broadcast_to(total, carry_sc.shape)
    cnt_ref[...] = jnp.broadcast_to(total, cnt_ref.shape).astype(I32)

    meta = jnp.where(lane == 0, i1, jnp.where(lane == 1, i2,
                     jnp.where(lane == 2, rank1, jnp.where(lane == 3, rank2, 0.0))))
    meta_ref[...] = meta.astype(I32)
    wts_ref[...] = jnp.where(lane == 0, w1, jnp.where(lane == 1, w2, 0.0))


def _router(x1, g, wr, br, tm=512):
    n = x1.shape[0]
    return pl.pallas_call(
        functools.partial(_router_kernel, tm=tm),
        out_shape=(jax.ShapeDtypeStruct((n, HALF), I32),
                   jax.ShapeDtypeStruct((n, LANES), I32),
                   jax.ShapeDtypeStruct((n, LANES), F32),
                   jax.ShapeDtypeStruct((8, LANES), I32)),
        grid=(n // tm,),
        in_specs=[pl.BlockSpec((tm, D_MODEL), lambda i: (i, 0)),
                  pl.BlockSpec((1, D_MODEL), lambda i: (0, 0)),
                  pl.BlockSpec((D_MODEL, LANES), lambda i: (0, 0)),
                  pl.BlockSpec((1, LANES), lambda i: (0, 0))],
        out_specs=(pl.BlockSpec((tm, HALF), lambda i: (i, 0)),
                   pl.BlockSpec((tm, LANES), lambda i: (i, 0)),
                   pl.BlockSpec((tm, LANES), lambda i: (i, 0)),
                   pl.BlockSpec((8, LANES), lambda i: (0, 0))),
        scratch_shapes=[pltpu.VMEM((8, LANES), F32),
                        pltpu.VMEM((D_MODEL, 2 * LANES), BF16)],
        compiler_params=_cparams(("arbitrary",)),
        name="router",
    )(x1, g, wr, br)


def _dispatch_kernel(e_ref, rk_ref, row0_ref, cnt_ref, misc_ref, xp_ref, xs_ref, zbuf, sem, zsem, *, tm):
    base = pl.program_id(0) * (tm * TOP_K)

    @pl.when(pl.program_id(0) == 0)
    def _():
        zbuf[...] = jnp.zeros_like(zbuf)

        def pad_fill(e, carry, *, wait):
            cnt = cnt_ref[e]
            start = row0_ref[e] + cnt
            head = (-cnt) & (SUBLANES - 1)
            for j in range(SUBLANES - 1):
                @pl.when(j < head)
                def _(j=j):
                    cp = pltpu.make_async_copy(zbuf.at[pl.ds(0, 1), :], xs_ref.at[pl.ds(start + j, 1), :], zsem)
                    cp.wait() if wait else cp.start()
            off = start + head
            rest = (-(cnt + head)) & (R_SUB - 1)
            bit = R_SUB // 2
            while bit >= SUBLANES:
                @pl.when((rest & bit) != 0)
                def _(off=off, bit=bit):
                    dst = xs_ref.at[pl.ds(pl.multiple_of(off, SUBLANES), bit), :]
                    cp = pltpu.make_async_copy(zbuf.at[pl.ds(0, bit), :], dst, zsem)
                    cp.wait() if wait else cp.start()
                off = off + (rest & bit)
                bit //= 2
            return carry

        def tail_fill(sb, carry, *, wait):
            r = pl.multiple_of(sb * R_SUB, R_SUB)
            cp = pltpu.make_async_copy(zbuf, xs_ref.at[pl.ds(r, R_SUB), :], zsem)
            cp.wait() if wait else cp.start()
            return carry

        for wait in (False, True):
            lax.fori_loop(0, N_EXPERTS, functools.partial(pad_fill, wait=wait), 0)
            lax.fori_loop(misc_ref[0], NSB_MAX, functools.partial(tail_fill, wait=wait), 0)

    def row_copy(r, kk):
        a = base + TOP_K * r + kk
        p = row0_ref[e_ref[a]] + rk_ref[a]
        return pltpu.make_async_copy(xp_ref.at[pl.ds(r, 1), :], xs_ref.at[pl.ds(p, 1), :], sem)

    def issue(r, carry):
        for kk in range(TOP_K):
            row_copy(r, kk).start()
        return carry

    lax.fori_loop(0, tm, issue, 0, unroll=8)
    for _ in range(TOP_K):
        pltpu.make_async_copy(xp_ref, xs_ref.at[pl.ds(0, tm), :], sem).wait()


def _dispatch(e_flat, rk_flat, row0, cnt, misc, xp, tm=512):
    n = xp.shape[0]
    return pl.pallas_call(
        functools.partial(_dispatch_kernel, tm=tm),
        out_shape=jax.ShapeDtypeStruct((NSB_MAX * R_SUB, HALF), I32),
        grid_spec=pltpu.PrefetchScalarGridSpec(
            num_scalar_prefetch=5,
            grid=(n // tm,),
            in_specs=[pl.BlockSpec((tm, HALF), lambda i, *_: (i, 0))],
            out_specs=pl.BlockSpec(memory_space=pl.ANY),
            scratch_shapes=[pltpu.VMEM((R_SUB, HALF), I32),
                            pltpu.SemaphoreType.DMA(()),
                            pltpu.SemaphoreType.DMA(())]),
        compiler_params=_cparams(("arbitrary",)),
        name="dispatch",
    )(e_flat, rk_flat, row0, cnt, misc, xp)


def _expert_kernel(ee_ref, sb0_ref, nsb_ref, misc_ref, xs_hbm, wg_ref, wu_ref, wd_ref, os_hbm,
                   xbuf, hbuf, obuf, zbuf, wgb, wub, wdb, xsem, osem, zsem):
    s = pl.program_id(0)
    c = pl.program_id(1)
    n_ent, n_ph = pl.num_programs(0), pl.num_programs(1)
    nsb = nsb_ref[s]
    slot = lax.rem(s, 2)

    def sub_row(ent, sb):
        return pl.multiple_of((sb0_ref[ent] + sb) * R_SUB, R_SUB)

    def x_copies(ent, slot_, wait):
        for sb in range(N_SUB):
            @pl.when(sb < nsb_ref[ent])
            def _(sb=sb):
                cp = pltpu.make_async_copy(xs_hbm.at[pl.ds(sub_row(ent, sb), R_SUB), :],
                                           xbuf.at[slot_, sb], xsem.at[slot_])
                cp.wait() if wait else cp.start()

    def o_copies(t, wait, only_sb=None):
        ent, nb, par = t // N_B, lax.rem(t, N_B), lax.rem(t, 2)
        col = pl.multiple_of(nb * DN_CHUNK, DN_CHUNK)
        for sb in range(N_SUB):
            if only_sb is not None and sb != only_sb:
                continue

            def go(sb=sb):
                cp = pltpu.make_async_copy(
                    obuf.at[par, sb],
                    os_hbm.at[pl.ds(sub_row(ent, sb), R_SUB), pl.ds(col, DN_CHUNK)], osem.at[par])
                cp.wait() if wait else cp.start()
            if only_sb is None:
                pl.when(sb < nsb_ref[ent])(go)
            else:
                go()

    @pl.when((s == 0) & (c == 0))
    def _():
        x_copies(0, 0, False)

    @pl.when(c == 0)
    def _():
        x_copies(s, slot, True)

    @pl.when((c == 0) & (s + 1 < n_ent))
    def _():
        x_copies(s + 1, 1 - slot, False)

    @pl.when((c < N_A) & (nsb > 0))
    def _():
        wgb[...] = wg_ref[0].astype(BF16)
        wub[...] = wu_ref[0].astype(BF16)
        for sb in range(N_SUB):
            @pl.when(sb < nsb)
            def _(sb=sb):
                w = xbuf[slot, sb]
                xa = lax.bitcast_convert_type(w & jnp.int32(-65536), F32).astype(BF16)
                xb = lax.bitcast_convert_type(lax.shift_left(w, 16), F32).astype(BF16)
                gate = (jnp.dot(xa, wgb[:HALF, :], preferred_element_type=F32)
                        + jnp.dot(xb, wgb[HALF:, :], preferred_element_type=F32))
                up = (jnp.dot(xa, wub[:HALF, :], preferred_element_type=F32)
                      + jnp.dot(xb, wub[HALF:, :], preferred_element_type=F32))
                hbuf[c, sb] = (jax.nn.silu(gate) * up).astype(BF16)

    @pl.when(c >= N_A)
    def _():
        t = s * N_B + (c - N_A)
        par = lax.rem(t, 2)

        @pl.when(t >= 2)
        def _():
            o_copies(t - 2, True)

        @pl.when(nsb > 0)
        def _():
            wdb[...] = wd_ref[0].astype(BF16)
            for sb in range(N_SUB):
                @pl.when(sb < nsb)
                def _(sb=sb):
                    acc = jnp.dot(hbuf[0, sb], wdb[0:GU_CHUNK, :], preferred_element_type=F32)
                    for ca in range(1, N_A):
                        acc = acc + jnp.dot(hbuf[ca, sb], wdb[ca * GU_CHUNK:(ca + 1) * GU_CHUNK, :],
                                            preferred_element_type=F32)
                    obuf[par, sb] = acc
                    o_copies(t, False, only_sb=sb)

    @pl.when((s == n_ent - 1) & (c == n_ph - 1))
    def _():
        t = s * N_B + (c - N_A)
        o_copies(t - 1, True)
        o_copies(t, True)
        zbuf[...] = jnp.zeros_like(zbuf)

        def tail_fill(sb, carry, *, wait):
            r = pl.multiple_of(sb * R_SUB, R_SUB)
            for nb in range(N_B):
                cp = pltpu.make_async_copy(
                    zbuf, os_hbm.at[pl.ds(r, R_SUB), pl.ds(nb * DN_CHUNK, DN_CHUNK)], zsem)
                cp.wait() if wait else cp.start()
            return carry

        for wait in (False, True):
            lax.fori_loop(misc_ref[0], NSB_MAX, functools.partial(tail_fill, wait=wait), 0)


def _experts(ent_e, ent_sb0, ent_nsb, misc, xs, w_gate, w_up, w_down):
    def gu_map(s, c, ee, sb0, nsb, misc):
        cur = (c < N_A) & (nsb[s] > 0)
        nxt = jnp.minimum(s + 1, NE_MAX - 1)
        return (jnp.where(c < N_A, ee[s], ee[nxt]), 0, jnp.where(cur, c, 0))

    def dn_map(s, c, ee, sb0, nsb, misc):
        return (ee[s], 0, jnp.where(nsb[s] > 0, jnp.clip(c - N_A, 0, N_B - 1), 0))

    return pl.pallas_call(
        _expert_kernel,
        out_shape=jax.ShapeDtypeStruct((NSB_MAX * R_SUB, D_MODEL), F32),
        grid_spec=pltpu.PrefetchScalarGridSpec(
            num_scalar_prefetch=4,
            grid=(NE_MAX, N_A + N_B),
            in_specs=[pl.BlockSpec(memory_space=pl.ANY),
                      pl.BlockSpec((1, D_MODEL, GU_CHUNK), gu_map),
                      pl.BlockSpec((1, D_MODEL, GU_CHUNK), gu_map),
                      pl.BlockSpec((1, D_EXPERT, DN_CHUNK), dn_map)],
            out_specs=pl.BlockSpec(memory_space=pl.ANY),
            scratch_shapes=[pltpu.VMEM((2, N_SUB, R_SUB, HALF), I32),
                            pltpu.VMEM((N_A, N_SUB, R_SUB, GU_CHUNK), BF16),
                            pltpu.VMEM((2, N_SUB, R_SUB, DN_CHUNK), F32),
                            pltpu.VMEM((R_SUB, DN_CHUNK), F32),
                            pltpu.VMEM((D_MODEL, GU_CHUNK), BF16),
                            pltpu.VMEM((D_MODEL, GU_CHUNK), BF16),
                            pltpu.VMEM((D_EXPERT, DN_CHUNK), BF16),
                            pltpu.SemaphoreType.DMA((2,)),
                            pltpu.SemaphoreType.DMA((2,)),
                            pltpu.SemaphoreType.DMA(())]),
        compiler_params=_cparams(("arbitrary", "arbitrary")),
        name="experts",
    )(ent_e, ent_sb0, ent_nsb, misc, xs, w_gate, w_up, w_down)


def _combine_kernel(e_ref, rk_ref, row0_ref, x_ref, wts_ref, g_ref, os_ref, y_ref, gbuf, sem, *, tm):
    i = pl.program_id(0)
    slot = lax.rem(i, 2)

    def issue_tile(tile, slot_):
        base = tile * (tm * TOP_K)

        def issue(r, carry):
            for kk in range(TOP_K):
                a = base + TOP_K * r + kk
                p = row0_ref[e_ref[a]] + rk_ref[a]
                pltpu.make_async_copy(os_ref.at[pl.ds(p, 1), :], gbuf.at[slot_, kk, pl.ds(r, 1), :],
                                      sem.at[slot_]).start()
            return carry

        lax.fori_loop(0, tm, issue, 0, unroll=8)

    @pl.when(i == 0)
    def _():
        issue_tile(0, 0)

    @pl.when(i + 1 < pl.num_programs(0))
    def _():
        issue_tile(i + 1, 1 - slot)

    for kk in range(TOP_K):
        pltpu.make_async_copy(os_ref.at[pl.ds(0, tm), :], gbuf.at[slot, kk], sem.at[slot]).wait()
    moe = wts_ref[:, 0:1] * gbuf[slot, 0] + wts_ref[:, 1:2] * gbuf[slot, 1]
    y_ref[...] = _rms(x_ref[...] + moe, g_ref[...])


def _combine(e_flat, rk_flat, row0, x1, wts, g, out_sorted, tm=256):
    n = x1.shape[0]
    return pl.pallas_call(
        functools.partial(_combine_kernel, tm=tm),
        out_shape=jax.ShapeDtypeStruct((n, D_MODEL), F32),
        grid_spec=pltpu.PrefetchScalarGridSpec(
            num_scalar_prefetch=3,
            grid=(n // tm,),
            in_specs=[pl.BlockSpec((tm, D_MODEL), lambda i, *_: (i, 0)),
                      pl.BlockSpec((tm, LANES), lambda i, *_: (i, 0)),
                      pl.BlockSpec((1, D_MODEL), lambda i, *_: (0, 0)),
                      pl.BlockSpec(memory_space=pl.ANY)],
            out_specs=pl.BlockSpec((tm, D_MODEL), lambda i, *_: (i, 0)),
            scratch_shapes=[pltpu.VMEM((2, TOP_K, tm, D_MODEL), F32),
                            pltpu.SemaphoreType.DMA((2,))]),
        compiler_params=_cparams(("arbitrary",)),
        name="combine",
    )(e_flat, rk_flat, row0, x1, wts, g, out_sorted)


def _moe(x1, norm_g, wr_g, br_g, wr_e, br_e, w_gate, w_up, w_down, final_g):
    n = x1.shape[0]
    pad = LANES - N_EXPERTS - N_GROUPS
    wr = jnp.concatenate([wr_e, wr_g, jnp.zeros((D_MODEL, pad), F32)], axis=1)
    br = jnp.concatenate([br_e, br_g, jnp.zeros((pad,), F32)])[None]
    xp, meta, wts, counts = _router(x1, norm_g[None], wr, br)

    assert n * TOP_K == N_ASSIGN
    cnt = counts[0, :N_EXPERTS]
    nsb = (cnt + R_SUB - 1) // R_SUB
    sb_end = jnp.cumsum(nsb)
    sb_start = sb_end - nsb
    nent = (nsb + N_SUB - 1) // N_SUB
    ent_end = jnp.cumsum(nent)
    ent_start = ent_end - nent
    ids = jnp.arange(NE_MAX, dtype=I32)
    valid = ids < ent_end[-1]
    ent_e = jnp.minimum(jnp.sum(ids[:, None] >= ent_end[None, :], axis=1), N_EXPERTS - 1)
    ent_e = jnp.where(valid, ent_e, ent_e[ent_end[-1] - 1])
    within = (ids - ent_start[ent_e]) * N_SUB
    ent_sb0 = jnp.where(valid, sb_start[ent_e] + within, 0)
    ent_nsb = jnp.where(valid, jnp.clip(nsb[ent_e] - within, 0, N_SUB), 0)
    row0 = (sb_start * R_SUB).astype(I32)
    misc = sb_end[-1:].astype(I32)
    e_flat = meta[:, 0:TOP_K].reshape(-1)
    rk_flat = meta[:, TOP_K:2 * TOP_K].reshape(-1)

    xs = _dispatch(e_flat, rk_flat, row0, cnt, misc, xp)
    out_sorted = _experts(ent_e.astype(I32), ent_sb0.astype(I32), ent_nsb.astype(I32), misc, xs,
                          w_gate, w_up, w_down)
    return _combine(e_flat, rk_flat, row0, x1, wts, final_g[None], out_sorted)


def kernel(x, norm_mix_g, w_in, gmlp_v_norm_g, gmlp_spatial_w, gmlp_spatial_b, mlstm_igate_b,
           mlstm_fgate_b, mlstm_out_norm_g, w_out, norm_ffn_g, router_group_w, router_group_b,
           router_expert_w, router_expert_b, expert_w_gate, expert_w_up, expert_w_down, final_norm_g):
    batch, seq, d = x.shape
    n = batch * seq
    assert w_in.shape[0] == 1, "the final rmsnorm is fused after the single layer's MoE"
    l = 0
    x2 = x.reshape(n, d)
    w_main = w_in[l].astype(BF16)
    w_gcol = jnp.pad(w_in[l][:, D_PROJ_MAIN:], ((0, 0), (0, LANES - N_GATE_COLS))).astype(BF16)
    proj, gate_pre = _in_proj(x2, norm_mix_g[l][None], w_main, w_gcol)
    y_a = _gmlp(proj, gmlp_v_norm_g[l][None], gmlp_spatial_w[l], gmlp_spatial_b[l].T)
    gate_b = jnp.concatenate([mlstm_igate_b[l], mlstm_fgate_b[l],
                              jnp.zeros((LANES - N_GATE_COLS,), F32)])[None]
    y_b = _mlstm(proj, gate_pre, gate_b, mlstm_out_norm_g[l][None], batch, seq)
    x1 = _out_proj(y_a, y_b, w_out[l].astype(BF16), x2)
    out = _moe(x1, norm_ffn_g[l], router_group_w[l], router_group_b[l], router_expert_w[l],
               router_expert_b[l], expert_w_gate[l], expert_w_up[l], expert_w_down[l], final_norm_g)
    return out.reshape(batch, seq, d)
```

```python
import functools

import jax
import jax.numpy as jnp
from jax import lax
from jax.experimental import pallas as pl
from jax.experimental.pallas import tpu as pltpu

F32 = jnp.float32
BF16 = jnp.bfloat16
I32 = jnp.int32

D_MODEL = 4096
D_GMLP = 2048
GMLP_CHUNK = 128
GMLP_GROUPS = 16
D_MLSTM = 2048
MLSTM_HEADS = 4
MLSTM_DV = 512
MLSTM_DQK = 256
GATE_SOFTCAP = 15.0
N_GROUPS = 8
EXPERTS_PER_GROUP = 8
N_EXPERTS = 64
TOP_K = 2
D_EXPERT = 512
EPS = 1e-6
D_PROJ_MAIN = 10240
N_GATE_COLS = 8

LANES = 128
SUBLANES = 8
VMEM_LIMIT = 56 * 1024 * 1024
MLSTM_L = 256
HALF = D_MODEL // 2
N_ASSIGN = 4 * 4096 * TOP_K
R_SUB = 256
N_SUB = 3
NSB_MAX = -(-(N_ASSIGN + N_EXPERTS * (R_SUB - 1)) // R_SUB)
NE_MAX = N_EXPERTS + N_ASSIGN // (R_SUB * N_SUB)
GU_CHUNK = 256
N_A = D_EXPERT // GU_CHUNK
DN_CHUNK = 1024
N_B = D_MODEL // DN_CHUNK


def _cparams(sem):
    return pltpu.CompilerParams(dimension_semantics=sem, vmem_limit_bytes=VMEM_LIMIT)


def _rms(x, g):
    return x * lax.rsqrt(jnp.mean(x * x, axis=-1, keepdims=True) + EPS) * g


def _in_proj_kernel(x_ref, g_ref, w_ref, wg_ref, proj_ref, gate_ref, xn_sc):
    @pl.when(pl.program_id(1) == 0)
    def _():
        xb = _rms(x_ref[...], g_ref[...]).astype(BF16)
        xn_sc[...] = xb
        gate_ref[...] = jnp.dot(xb, wg_ref[...], preferred_element_type=F32)

    proj_ref[...] = jnp.dot(xn_sc[...], w_ref[...], preferred_element_type=F32).astype(BF16)


def _in_proj(x2, g, w_main, w_gate, tm=512, tn=1024):
    n = x2.shape[0]
    return pl.pallas_call(
        _in_proj_kernel,
        out_shape=(jax.ShapeDtypeStruct((n, D_PROJ_MAIN), BF16),
                   jax.ShapeDtypeStruct((n, LANES), F32)),
        grid=(n // tm, D_PROJ_MAIN // tn),
        in_specs=[pl.BlockSpec((tm, D_MODEL), lambda i, j: (i, 0)),
                  pl.BlockSpec((1, D_MODEL), lambda i, j: (0, 0)),
                  pl.BlockSpec((D_MODEL, tn), lambda i, j: (0, j)),
                  pl.BlockSpec((D_MODEL, LANES), lambda i, j: (0, 0))],
        out_specs=(pl.BlockSpec((tm, tn), lambda i, j: (i, j)),
                   pl.BlockSpec((tm, LANES), lambda i, j: (i, 0))),
        scratch_shapes=[pltpu.VMEM((tm, D_MODEL), BF16)],
        compiler_params=_cparams(("parallel", "arbitrary")),
        name="in_proj",
    )(x2, g, w_main, w_gate)


def _gmlp_kernel(u_ref, v_ref, gv_ref, w_ref, bt_ref, y_ref, wm_sc, *, rows):
    c = GMLP_CHUNK

    @pl.when(pl.program_id(0) == 0)
    def _():
        r = lax.broadcasted_iota(I32, (c, c), 0)
        s = lax.broadcasted_iota(I32, (c, c), 1)
        for g in range(GMLP_GROUPS):
            wm_sc[g] = jnp.where(s <= r, w_ref[g], 0.0).astype(BF16)

    for ci in range(rows // c):
        rs = slice(ci * c, (ci + 1) * c)
        gu = jax.nn.gelu(u_ref[rs, :].astype(F32))
        gv = jax.nn.gelu(v_ref[rs, :].astype(F32))
        vn = _rms(gv, gv_ref[...]).astype(BF16)
        for g in range(GMLP_GROUPS):
            cs = slice(g * c, (g + 1) * c)
            z = jnp.dot(wm_sc[g], vn[:, cs], preferred_element_type=F32) + bt_ref[:, g:g + 1]
            y_ref[rs, cs] = (gu[:, cs] * z).astype(BF16)


def _gmlp(proj, gv, w_s, b_t, rows=512):
    n = proj.shape[0]
    return pl.pallas_call(
        functools.partial(_gmlp_kernel, rows=rows),
        out_shape=jax.ShapeDtypeStruct((n, D_GMLP), BF16),
        grid=(n // rows,),
        in_specs=[pl.BlockSpec((rows, D_GMLP), lambda i: (i, 0)),
                  pl.BlockSpec((rows, D_GMLP), lambda i: (i, 1)),
                  pl.BlockSpec((1, D_GMLP), lambda i: (0, 0)),
                  pl.BlockSpec((GMLP_GROUPS, GMLP_CHUNK, GMLP_CHUNK), lambda i: (0, 0, 0)),
                  pl.BlockSpec((GMLP_CHUNK, GMLP_GROUPS), lambda i: (0, 0))],
        out_specs=pl.BlockSpec((rows, D_GMLP), lambda i: (i, 0)),
        scratch_shapes=[pltpu.VMEM((GMLP_GROUPS, GMLP_CHUNK, GMLP_CHUNK), BF16)],
        compiler_params=_cparams(("arbitrary",)),
        name="gmlp",
    )(proj, proj, gv, w_s, b_t)


def _mlstm_kernel(q_ref, k_ref, v_ref, o_ref, gp_ref, gb_ref, og_ref, y_ref, c_sc, n_sc, m_sc):
    L, H, DQK, DV = MLSTM_L, MLSTM_HEADS, MLSTM_DQK, MLSTM_DV
    scale = DQK ** -0.5

    @pl.when(pl.program_id(1) == 0)
    def _():
        c_sc[...] = jnp.zeros_like(c_sc)
        n_sc[...] = jnp.zeros_like(n_sc)
        m_sc[...] = jnp.zeros_like(m_sc)

    lane = lax.broadcasted_iota(I32, (L, LANES), 1)
    z = gp_ref[...] + gb_ref[...]
    sc = GATE_SOFTCAP * jnp.tanh(z / GATE_SOFTCAP)
    logf = -(jnp.maximum(-sc, 0.0) + jnp.log1p(jnp.exp(-jnp.abs(sc))))
    row = lax.broadcasted_iota(I32, (L, L), 0)
    col = lax.broadcasted_iota(I32, (L, L), 1)
    causal = col <= row
    tri = jnp.where(causal, 1.0, 0.0).astype(BF16)
    bcum, rest = None, logf
    for _ in range(3):
        part = rest.astype(BF16)
        rest = rest - part.astype(F32)
        term = jnp.dot(tri, part, preferred_element_type=F32)
        bcum = term if bcum is None else bcum + term
    gcol = jnp.where(lane < H, sc, bcum)
    grow = gcol.T

    for h in range(H):
        ig_col, b_col = gcol[:, h:h + 1], gcol[:, H + h:H + h + 1]
        ig_row, b_row = grow[h:h + 1, :], grow[H + h:H + h + 1, :]
        m = m_sc[h, 0:1, 0:1]
        q = q_ref[:, h * DQK:(h + 1) * DQK]
        k = k_ref[:, h * DQK:(h + 1) * DQK]
        v = v_ref[:, h * DV:(h + 1) * DV]

        d_log = jnp.where(causal, b_col - b_row + ig_row, -jnp.inf)
        inter = b_col + m
        m_row = jnp.maximum(inter, jnp.max(d_log, axis=-1, keepdims=True))
        w_intra = jnp.exp(d_log - m_row)
        w_inter = jnp.exp(inter - m_row)
        s = lax.dot_general(q, k, (((1,), (1,)), ((), ())), preferred_element_type=F32)
        s = s * scale * w_intra
        c_old = c_sc[h]
        n_old = n_sc[h]
        num = (w_inter * jnp.dot(q, c_old.astype(BF16), preferred_element_type=F32)
               + jnp.dot(s.astype(BF16), v, preferred_element_type=F32))
        qn = jnp.sum(q.astype(F32) * n_old, axis=-1, keepdims=True)
        den = w_inter * qn + jnp.sum(s, axis=-1, keepdims=True)
        hh = num / jnp.maximum(jnp.abs(den), jnp.exp(-m_row))

        b_last = b_row[:, L - 1:L]
        tail = b_last - b_col + ig_col
        m_new = jnp.maximum(b_last + m, jnp.max(tail, axis=0, keepdims=True))
        w_tail = jnp.exp(tail - m_new)
        decay = jnp.exp(b_last + m - m_new)
        kw = k.astype(F32) * scale * w_tail
        c_sc[h] = decay * c_old + lax.dot_general(
            kw.astype(BF16), v, (((0,), (0,)), ((), ())), preferred_element_type=F32)
        n_sc[h] = decay * n_old + jnp.sum(kw, axis=0, keepdims=True)
        m_sc[h] = jnp.broadcast_to(m_new, m_sc.shape[1:])

        hn = hh * lax.rsqrt(jnp.mean(hh * hh, axis=-1, keepdims=True) + EPS)
        hn = hn * og_ref[:, h * DV:(h + 1) * DV]
        hn = hn * jax.nn.sigmoid(o_ref[:, h * DV:(h + 1) * DV].astype(F32))
        y_ref[:, h * DV:(h + 1) * DV] = hn.astype(BF16)


def _mlstm(proj, gate_pre, gate_b, out_g, batch, seq):
    n = proj.shape[0]
    L = MLSTM_L
    nc = seq // L
    qk_w = MLSTM_HEADS * MLSTM_DQK
    q_blk, k_blk = 2 * D_GMLP // qk_w, 2 * D_GMLP // qk_w + 1
    v_blk, o_blk = (2 * D_GMLP + 2 * qk_w) // D_MLSTM, (2 * D_GMLP + 2 * qk_w) // D_MLSTM + 1
    rmap = lambda b, c: b * nc + c
    return pl.pallas_call(
        _mlstm_kernel,
        out_shape=jax.ShapeDtypeStruct((n, D_MLSTM), BF16),
        grid=(batch, nc),
        in_specs=[pl.BlockSpec((L, qk_w), lambda b, c: (rmap(b, c), q_blk)),
                  pl.BlockSpec((L, qk_w), lambda b, c: (rmap(b, c), k_blk)),
                  pl.BlockSpec((L, D_MLSTM), lambda b, c: (rmap(b, c), v_blk)),
                  pl.BlockSpec((L, D_MLSTM), lambda b, c: (rmap(b, c), o_blk)),
                  pl.BlockSpec((L, LANES), lambda b, c: (rmap(b, c), 0)),
                  pl.BlockSpec((1, LANES), lambda b, c: (0, 0)),
                  pl.BlockSpec((1, D_MLSTM), lambda b, c: (0, 0))],
        out_specs=pl.BlockSpec((L, D_MLSTM), lambda b, c: (rmap(b, c), 0)),
        scratch_shapes=[pltpu.VMEM((MLSTM_HEADS, MLSTM_DQK, MLSTM_DV), F32),
                        pltpu.VMEM((MLSTM_HEADS, 1, MLSTM_DQK), F32),
                        pltpu.VMEM((MLSTM_HEADS, 8, LANES), F32)],
        compiler_params=_cparams(("parallel", "arbitrary")),
        name="mlstm",
    )(proj, proj, proj, proj, gate_pre, gate_b, out_g)


def _out_proj_kernel(ya_ref, yb_ref, wa_ref, wb_ref, x_ref, o_ref):
    acc = jnp.dot(ya_ref[...], wa_ref[...], preferred_element_type=F32)
    acc = acc + jnp.dot(yb_ref[...], wb_ref[...], preferred_element_type=F32)
    o_ref[...] = x_ref[...] + acc


def _out_proj(y_a, y_b, w_out, x2, tm=512, tn=1024):
    n = x2.shape[0]
    return pl.pallas_call(
        _out_proj_kernel,
        out_shape=jax.ShapeDtypeStruct((n, D_MODEL), F32),
        grid=(n // tm, D_MODEL // tn),
        in_specs=[pl.BlockSpec((tm, D_GMLP), lambda i, j: (i, 0)),
                  pl.BlockSpec((tm, D_MLSTM), lambda i, j: (i, 0)),
                  pl.BlockSpec((D_GMLP, tn), lambda i, j: (0, j)),
                  pl.BlockSpec((D_MLSTM, tn), lambda i, j: (1, j)),
                  pl.BlockSpec((tm, tn), lambda i, j: (i, j))],
        out_specs=pl.BlockSpec((tm, tn), lambda i, j: (i, j)),
        compiler_params=_cparams(("parallel", "arbitrary")),
        name="out_proj",
    )(y_a, y_b, w_out, w_out, x2)


def _router_kernel(x_ref, g_ref, wr_ref, br_ref, xp_ref, meta_ref, wts_ref, cnt_ref, carry_sc, w2_sc,
                   *, tm):
    i = pl.program_id(0)

    @pl.when(i == 0)
    def _():
        carry_sc[...] = jnp.zeros_like(carry_sc)
        w_hi = wr_ref[...].astype(BF16)
        w2_sc[:, :LANES] = w_hi
        w2_sc[:, LANES:] = (wr_ref[...] - w_hi.astype(F32)).astype(BF16)

    xn = _rms(x_ref[...], g_ref[...])
    bits = lax.bitcast_convert_type(xn.astype(BF16).astype(F32), I32)
    xp_ref[...] = bits[:, :HALF] | lax.shift_right_logical(bits[:, HALF:], 16)

    x_hi = xn.astype(BF16)
    x_lo = (xn - x_hi.astype(F32)).astype(BF16)
    both = jnp.dot(x_hi, w2_sc[...], preferred_element_type=F32)
    logits = (both[:, :LANES] + both[:, LANES:]
              + jnp.dot(x_lo, w2_sc[:, :LANES], preferred_element_type=F32)) + br_ref[...]
    lane = lax.broadcasted_iota(I32, (tm, LANES), 1)
    lane_f = lane.astype(F32)
    big = float(LANES)
    is_g = (lane >= N_EXPERTS) & (lane < N_EXPERTS + N_GROUPS)
    gl = jnp.where(is_g, logits, -jnp.inf)
    gmax = jnp.max(gl, axis=-1, keepdims=True)
    g_lane = jnp.min(jnp.where(gl == gmax, lane_f, big), axis=-1, keepdims=True)
    g_sum = jnp.sum(jnp.where(is_g, jnp.exp(gl - gmax), 0.0), axis=-1, keepdims=True)
    g_prob = 1.0 / g_sum
    g_idx = g_lane.astype(I32) - N_EXPERTS
    in_grp = (lane < N_EXPERTS) & (lax.shift_right_logical(lane, 3) == g_idx)
    el = jnp.where(in_grp, logits, -jnp.inf)
    t1 = jnp.max(el, axis=-1, keepdims=True)
    i1 = jnp.min(jnp.where(el == t1, lane_f, big), axis=-1, keepdims=True)
    el2 = jnp.where(lane_f == i1, -jnp.inf, el)
    t2 = jnp.max(el2, axis=-1, keepdims=True)
    i2 = jnp.min(jnp.where(el2 == t2, lane_f, big), axis=-1, keepdims=True)
    e2 = jnp.exp(t2 - t1)
    w1 = g_prob * (1.0 / (1.0 + e2))
    w2 = g_prob * (e2 / (1.0 + e2))

    sel1 = lane_f == i1
    sel2 = lane_f == i2
    onehot = jnp.where(sel1 | sel2, 1.0, 0.0)
    r = lax.broadcasted_iota(I32, (tm, tm), 0)
    c = lax.broadcasted_iota(I32, (tm, tm), 1)
    strict = jnp.where(c < r, 1.0, 0.0).astype(BF16)
    prefix = jnp.dot(strict, onehot.astype(BF16), preferred_element_type=F32) + carry_sc[0:1, :]
    rank1 = jnp.sum(jnp.where(sel1, prefix, 0.0), axis=-1, keepdims=True)
    rank2 = jnp.sum(jnp.where(sel2, prefix, 0.0), axis=-1, keepdims=True)
    total = carry_sc[0:1, :] + jnp.sum(onehot, axis=0, keepdims=True)
    carry_sc[...] = jnp.broadcast_to(total, carry_sc.shape)
    cnt_ref[...] = jnp.broadcast_to(total, cnt_ref.shape).astype(I32)

    meta = jnp.where(lane == 0, i1, jnp.where(lane == 1, i2,
                     jnp.where(lane == 2, rank1, jnp.where(lane == 3, rank2, 0.0))))
    meta_ref[...] = meta.astype(I32)
    wts_ref[...] = jnp.where(lane == 0, w1, jnp.where(lane == 1, w2, 0.0))


def _router(x1, g, wr, br, tm=512):
    n = x1.shape[0]
    return pl.pallas_call(
        functools.partial(_router_kernel, tm=tm),
        out_shape=(jax.ShapeDtypeStruct((n, HALF), I32),
                   jax.ShapeDtypeStruct((n, LANES), I32),
                   jax.ShapeDtypeStruct((n, LANES), F32),
                   jax.ShapeDtypeStruct((8, LANES), I32)),
        grid=(n // tm,),
        in_specs=[pl.BlockSpec((tm, D_MODEL), lambda i: (i, 0)),
                  pl.BlockSpec((1, D_MODEL), lambda i: (0, 0)),
                  pl.BlockSpec((D_MODEL, LANES), lambda i: (0, 0)),
                  pl.BlockSpec((1, LANES), lambda i: (0, 0))],
        out_specs=(pl.BlockSpec((tm, HALF), lambda i: (i, 0)),
                   pl.BlockSpec((tm, LANES), lambda i: (i, 0)),
                   pl.BlockSpec((tm, LANES), lambda i: (i, 0)),
                   pl.BlockSpec((8, LANES), lambda i: (0, 0))),
        scratch_shapes=[pltpu.VMEM((8, LANES), F32),
                        pltpu.VMEM((D_MODEL, 2 * LANES), BF16)],
        compiler_params=_cparams(("arbitrary",)),
        name="router",
    )(x1, g, wr, br)


def _dispatch_kernel(e_ref, rk_ref, row0_ref, cnt_ref, misc_ref, xp_ref, xs_ref, zbuf, sem, zsem, *, tm):
    base = pl.program_id(0) * (tm * TOP_K)

    @pl.when(pl.program_id(0) == 0)
    def _():
        zbuf[...] = jnp.zeros_like(zbuf)

        def pad_fill(e, carry, *, wait):
            cnt = cnt_ref[e]
            start = row0_ref[e] + cnt
            head = (-cnt) & (SUBLANES - 1)
            for j in range(SUBLANES - 1):
                @pl.when(j < head)
                def _(j=j):
                    cp = pltpu.make_async_copy(zbuf.at[pl.ds(0, 1), :], xs_ref.at[pl.ds(start + j, 1), :], zsem)
                    cp.wait() if wait else cp.start()
            off = start + head
            rest = (-(cnt + head)) & (R_SUB - 1)
            bit = R_SUB // 2
            while bit >= SUBLANES:
                @pl.when((rest & bit) != 0)
                def _(off=off, bit=bit):
                    dst = xs_ref.at[pl.ds(pl.multiple_of(off, SUBLANES), bit), :]
                    cp = pltpu.make_async_copy(zbuf.at[pl.ds(0, bit), :], dst, zsem)
                    cp.wait() if wait else cp.start()
                off = off + (rest & bit)
                bit //= 2
            return carry

        def tail_fill(sb, carry, *, wait):
            r = pl.multiple_of(sb * R_SUB, R_SUB)
            cp = pltpu.make_async_copy(zbuf, xs_ref.at[pl.ds(r, R_SUB), :], zsem)
            cp.wait() if wait else cp.start()
            return carry

        for wait in (False, True):
            lax.fori_loop(0, N_EXPERTS, functools.partial(pad_fill, wait=wait), 0)
            lax.fori_loop(misc_ref[0], NSB_MAX, functools.partial(tail_fill, wait=wait), 0)

    def row_copy(r, kk):
        a = base + TOP_K * r + kk
        p = row0_ref[e_ref[a]] + rk_ref[a]
        return pltpu.make_async_copy(xp_ref.at[pl.ds(r, 1), :], xs_ref.at[pl.ds(p, 1), :], sem)

    def issue(r, carry):
        for kk in range(TOP_K):
            row_copy(r, kk).start()
        return carry

    lax.fori_loop(0, tm, issue, 0, unroll=8)
    for _ in range(TOP_K):
        pltpu.make_async_copy(xp_ref, xs_ref.at[pl.ds(0, tm), :], sem).wait()


def _dispatch(e_flat, rk_flat, row0, cnt, misc, xp, tm=512):
    n = xp.shape[0]
    return pl.pallas_call(
        functools.partial(_dispatch_kernel, tm=tm),
        out_shape=jax.ShapeDtypeStruct((NSB_MAX * R_SUB, HALF), I32),
        grid_spec=pltpu.PrefetchScalarGridSpec(
            num_scalar_prefetch=5,
            grid=(n // tm,),
            in_specs=[pl.BlockSpec((tm, HALF), lambda i, *_: (i, 0))],
            out_specs=pl.BlockSpec(memory_space=pl.ANY),
            scratch_shapes=[pltpu.VMEM((R_SUB, HALF), I32),
                            pltpu.SemaphoreType.DMA(()),
                            pltpu.SemaphoreType.DMA(())]),
        compiler_params=_cparams(("arbitrary",)),
        name="dispatch",
    )(e_flat, rk_flat, row0, cnt, misc, xp)


def _expert_kernel(ee_ref, sb0_ref, nsb_ref, misc_ref, xs_hbm, wg_hbm, wu_hbm, wd_hbm, os_hbm,
                   xbuf, hbuf, obuf, zbuf, gbuf, ubuf, dbuf, xsem, osem, zsem, gsem, dsem):
    s = pl.program_id(0)
    n_ent = pl.num_programs(0)
    nsb = nsb_ref[s]
    slot = lax.rem(s, 2)
    nxt = jnp.minimum(s + 1, n_ent - 1)
    has_next = (s + 1 < n_ent) & (nsb_ref[nxt] > 0)

    def sub_row(ent, sb):
        return pl.multiple_of((sb0_ref[ent] + sb) * R_SUB, R_SUB)

    def x_copies(ent, slot_, wait):
        for sb in range(N_SUB):
            @pl.when(sb < nsb_ref[ent])
            def _(sb=sb):
                cp = pltpu.make_async_copy(xs_hbm.at[pl.ds(sub_row(ent, sb), R_SUB), :],
                                           xbuf.at[slot_, sb], xsem.at[slot_])
                cp.wait() if wait else cp.start()

    def gu_copies(ent, c, wait):
        e = ee_ref[ent]
        for w_hbm, buf in ((wg_hbm, gbuf), (wu_hbm, ubuf)):
            cp = pltpu.make_async_copy(w_hbm.at[e, :, pl.ds(c * GU_CHUNK, GU_CHUNK)], buf.at[c % 2],
                                       gsem.at[c % 2])
            cp.wait() if wait else cp.start()

    def dn_copy(ent, nb, wait):
        cp = pltpu.make_async_copy(wd_hbm.at[ee_ref[ent], :, pl.ds(nb * DN_CHUNK, DN_CHUNK)],
                                   dbuf.at[nb % 2], dsem.at[nb % 2])
        cp.wait() if wait else cp.start()

    def o_copy(ent, sb, nb, wait):
        cp = pltpu.make_async_copy(
            obuf.at[nb % 2, sb],
            os_hbm.at[pl.ds(sub_row(ent, sb), R_SUB), pl.ds(nb * DN_CHUNK, DN_CHUNK)], osem.at[nb % 2])
        cp.wait() if wait else cp.start()

    def o_waits(ent, nb):
        for sb in range(N_SUB):
            @pl.when(sb < nsb_ref[ent])
            def _(sb=sb):
                o_copy(ent, sb, nb, True)

    @pl.when(s == 0)
    def _():
        x_copies(0, 0, False)
        gu_copies(0, 0, False)
        dn_copy(0, 0, False)

    @pl.when(nsb == 0)
    def _():
        @pl.when(s >= 1)
        def _():
            for nb in range(N_B - 2, N_B):
                o_waits(s - 1, nb)

    @pl.when(nsb > 0)
    def _():
        x_copies(s, slot, True)

        @pl.when(s + 1 < n_ent)
        def _():
            x_copies(s + 1, 1 - slot, False)

        for c in range(N_A):
            gu_copies(s, c, True)
            if c + 1 < N_A:
                gu_copies(s, c + 1, False)
            else:
                @pl.when(has_next)
                def _():
                    gu_copies(nxt, 0, False)
            for sb in range(N_SUB):
                @pl.when(sb < nsb)
                def _(sb=sb, c=c):
                    w = xbuf[slot, sb]
                    xa = lax.bitcast_convert_type(w & jnp.int32(-65536), F32).astype(BF16)
                    xb = lax.bitcast_convert_type(lax.shift_left(w, 16), F32).astype(BF16)
                    wg, wu = gbuf.at[c % 2], ubuf.at[c % 2]
                    gate = (jnp.dot(xa, wg[:HALF, :].astype(BF16), preferred_element_type=F32)
                            + jnp.dot(xb, wg[HALF:, :].astype(BF16), preferred_element_type=F32))
                    up = (jnp.dot(xa, wu[:HALF, :].astype(BF16), preferred_element_type=F32)
                          + jnp.dot(xb, wu[HALF:, :].astype(BF16), preferred_element_type=F32))
                    hbuf[c, sb] = (jax.nn.silu(gate) * up).astype(BF16)

        for nb in range(N_B):
            if nb >= 2:
                o_waits(s, nb - 2)
            else:
                @pl.when(s >= 1)
                def _(nb=nb):
                    o_waits(s - 1, nb + N_B - 2)
            dn_copy(s, nb, True)
            if nb + 1 < N_B:
                dn_copy(s, nb + 1, False)
            else:
                @pl.when(has_next)
                def _():
                    dn_copy(nxt, 0, False)
            for sb in range(N_SUB):
                @pl.when(sb < nsb)
                def _(sb=sb, nb=nb):
                    wd = dbuf.at[nb % 2]
                    acc = jnp.dot(hbuf[0, sb], wd[0:GU_CHUNK, :].astype(BF16), preferred_element_type=F32)
                    for ca in range(1, N_A):
                        acc = acc + jnp.dot(hbuf[ca, sb],
                                            wd[ca * GU_CHUNK:(ca + 1) * GU_CHUNK, :].astype(BF16),
                                            preferred_element_type=F32)
                    obuf[nb % 2, sb] = acc
                    o_copy(s, sb, nb, False)

    @pl.when(s == n_ent - 1)
    def _():
        @pl.when(nsb > 0)
        def _():
            for nb in range(N_B - 2, N_B):
                o_waits(s, nb)
        zbuf[...] = jnp.zeros_like(zbuf)

        def tail_fill(sb, carry, *, wait):
            r = pl.multiple_of(sb * R_SUB, R_SUB)
            for nb in range(N_B):
                cp = pltpu.make_async_copy(
                    zbuf, os_hbm.at[pl.ds(r, R_SUB), pl.ds(nb * DN_CHUNK, DN_CHUNK)], zsem)
                cp.wait() if wait else cp.start()
            return carry

        for wait in (False, True):
            lax.fori_loop(misc_ref[0], NSB_MAX, functools.partial(tail_fill, wait=wait), 0)


def _experts(ent_e, ent_sb0, ent_nsb, misc, xs, w_gate, w_up, w_down):
    assert N_A % 2 == 0 and N_B % 2 == 0, "weight chunks alternate between two staging slots"
    any_spec = pl.BlockSpec(memory_space=pl.ANY)
    return pl.pallas_call(
        _expert_kernel,
        out_shape=jax.ShapeDtypeStruct((NSB_MAX * R_SUB, D_MODEL), F32),
        grid_spec=pltpu.PrefetchScalarGridSpec(
            num_scalar_prefetch=4,
            grid=(NE_MAX,),
            in_specs=[any_spec, any_spec, any_spec, any_spec],
            out_specs=any_spec,
            scratch_shapes=[pltpu.VMEM((2, N_SUB, R_SUB, HALF), I32),
                            pltpu.VMEM((N_A, N_SUB, R_SUB, GU_CHUNK), BF16),
                            pltpu.VMEM((2, N_SUB, R_SUB, DN_CHUNK), F32),
                            pltpu.VMEM((R_SUB, DN_CHUNK), F32),
                            pltpu.VMEM((2, D_MODEL, GU_CHUNK), F32),
                            pltpu.VMEM((2, D_MODEL, GU_CHUNK), F32),
                            pltpu.VMEM((2, D_EXPERT, DN_CHUNK), F32),
                            pltpu.SemaphoreType.DMA((2,)),
                            pltpu.SemaphoreType.DMA((2,)),
                            pltpu.SemaphoreType.DMA(()),
                            pltpu.SemaphoreType.DMA((2,)),
                            pltpu.SemaphoreType.DMA((2,))]),
        compiler_params=_cparams(("arbitrary",)),
        name="experts",
    )(ent_e, ent_sb0, ent_nsb, misc, xs, w_gate, w_up, w_down)


def _combine_kernel(e_ref, rk_ref, row0_ref, x_ref, wts_ref, g_ref, os_ref, y_ref, gbuf, sem, *, tm):
    i = pl.program_id(0)
    slot = lax.rem(i, 2)

    def issue_tile(tile, slot_):
        base = tile * (tm * TOP_K)

        def issue(r, carry):
            for kk in range(TOP_K):
                a = base + TOP_K * r + kk
                p = row0_ref[e_ref[a]] + rk_ref[a]
                pltpu.make_async_copy(os_ref.at[pl.ds(p, 1), :], gbuf.at[slot_, kk, pl.ds(r, 1), :],
                                      sem.at[slot_]).start()
            return carry

        lax.fori_loop(0, tm, issue, 0, unroll=8)

    @pl.when(i == 0)
    def _():
        issue_tile(0, 0)

    @pl.when(i + 1 < pl.num_programs(0))
    def _():
        issue_tile(i + 1, 1 - slot)

    for kk in range(TOP_K):
        pltpu.make_async_copy(os_ref.at[pl.ds(0, tm), :], gbuf.at[slot, kk], sem.at[slot]).wait()
    moe = wts_ref[:, 0:1] * gbuf[slot, 0] + wts_ref[:, 1:2] * gbuf[slot, 1]
    y_ref[...] = _rms(x_ref[...] + moe, g_ref[...])


def _combine(e_flat, rk_flat, row0, x1, wts, g, out_sorted, tm=256):
    n = x1.shape[0]
    return pl.pallas_call(
        functools.partial(_combine_kernel, tm=tm),
        out_shape=jax.ShapeDtypeStruct((n, D_MODEL), F32),
        grid_spec=pltpu.PrefetchScalarGridSpec(
            num_scalar_prefetch=3,
            grid=(n // tm,),
            in_specs=[pl.BlockSpec((tm, D_MODEL), lambda i, *_: (i, 0)),
                      pl.BlockSpec((tm, LANES), lambda i, *_: (i, 0)),
                      pl.BlockSpec((1, D_MODEL), lambda i, *_: (0, 0)),
                      pl.BlockSpec(memory_space=pl.ANY)],
            out_specs=pl.BlockSpec((tm, D_MODEL), lambda i, *_: (i, 0)),
            scratch_shapes=[pltpu.VMEM((2, TOP_K, tm, D_MODEL), F32),
                            pltpu.SemaphoreType.DMA((2,))]),
        compiler_params=_cparams(("arbitrary",)),
        name="combine",
    )(e_flat, rk_flat, row0, x1, wts, g, out_sorted)


def _moe(x1, norm_g, wr_g, br_g, wr_e, br_e, w_gate, w_up, w_down, final_g):
    n = x1.shape[0]
    pad = LANES - N_EXPERTS - N_GROUPS
    wr = jnp.concatenate([wr_e, wr_g, jnp.zeros((D_MODEL, pad), F32)], axis=1)
    br = jnp.concatenate([br_e, br_g, jnp.zeros((pad,), F32)])[None]
    xp, meta, wts, counts = _router(x1, norm_g[None], wr, br)

    assert n * TOP_K == N_ASSIGN
    cnt = counts[0, :N_EXPERTS]
    nsb = (cnt + R_SUB - 1) // R_SUB
    sb_end = jnp.cumsum(nsb)
    sb_start = sb_end - nsb
    nent = (nsb + N_SUB - 1) // N_SUB
    ent_end = jnp.cumsum(nent)
    ent_start = ent_end - nent
    ids = jnp.arange(NE_MAX, dtype=I32)
    valid = ids < ent_end[-1]
    ent_e = jnp.minimum(jnp.sum(ids[:, None] >= ent_end[None, :], axis=1), N_EXPERTS - 1)
    ent_e = jnp.where(valid, ent_e, ent_e[ent_end[-1] - 1])
    within = (ids - ent_start[ent_e]) * N_SUB
    ent_sb0 = jnp.where(valid, sb_start[ent_e] + within, 0)
    ent_nsb = jnp.where(valid, jnp.clip(nsb[ent_e] - within, 0, N_SUB), 0)
    row0 = (sb_start * R_SUB).astype(I32)
    misc = sb_end[-1:].astype(I32)
    e_flat = meta[:, 0:TOP_K].reshape(-1)
    rk_flat = meta[:, TOP_K:2 * TOP_K].reshape(-1)

    xs = _dispatch(e_flat, rk_flat, row0, cnt, misc, xp)
    out_sorted = _experts(ent_e.astype(I32), ent_sb0.astype(I32), ent_nsb.astype(I32), misc, xs,
                          w_gate, w_up, w_down)
    return _combine(e_flat, rk_flat, row0, x1, wts, final_g[None], out_sorted)


def kernel(x, norm_mix_g, w_in, gmlp_v_norm_g, gmlp_spatial_w, gmlp_spatial_b, mlstm_igate_b,
           mlstm_fgate_b, mlstm_out_norm_g, w_out, norm_ffn_g, router_group_w, router_group_b,
           router_expert_w, router_expert_b, expert_w_gate, expert_w_up, expert_w_down, final_norm_g):
    batch, seq, d = x.shape
    n = batch * seq
    assert w_in.shape[0] == 1, "the final rmsnorm is fused after the single layer's MoE"
    l = 0
    x2 = x.reshape(n, d)
    w_main = w_in[l].astype(BF16)
    w_gcol = jnp.pad(w_in[l][:, D_PROJ_MAIN:], ((0, 0), (0, LANES - N_GATE_COLS))).astype(BF16)
    proj, gate_pre = _in_proj(x2, norm_mix_g[l][None], w_main, w_gcol)
    y_a = _gmlp(proj, gmlp_v_norm_g[l][None], gmlp_spatial_w[l], gmlp_spatial_b[l].T)
    gate_b = jnp.concatenate([mlstm_igate_b[l], mlstm_fgate_b[l],
                              jnp.zeros((LANES - N_GATE_COLS,), F32)])[None]
    y_b = _mlstm(proj, gate_pre, gate_b, mlstm_out_norm_g[l][None], batch, seq)
    x1 = _out_proj(y_a, y_b, w_out[l].astype(BF16), x2)
    out = _moe(x1, norm_ffn_g[l], router_group_w[l], router_group_b[l], router_expert_w[l],
               router_expert_b[l], expert_w_gate[l], expert_w_up[l], expert_w_down[l], final_norm_g)
    return out.reshape(batch, seq, d)
```

```python
import functools

import jax
import jax.numpy as jnp
from jax import lax
from jax.experimental import pallas as pl
from jax.experimental.pallas import tpu as pltpu

F32 = jnp.float32
BF16 = jnp.bfloat16
I32 = jnp.int32

D_MODEL = 4096
D_GMLP = 2048
GMLP_CHUNK = 128
GMLP_GROUPS = 16
D_MLSTM = 2048
MLSTM_HEADS = 4
MLSTM_DV = 512
MLSTM_DQK = 256
GATE_SOFTCAP = 15.0
N_GROUPS = 8
EXPERTS_PER_GROUP = 8
N_EXPERTS = 64
TOP_K = 2
D_EXPERT = 512
EPS = 1e-6
D_PROJ_MAIN = 10240
N_GATE_COLS = 8

LANES = 128
SUBLANES = 8
VMEM_LIMIT = 56 * 1024 * 1024
MLSTM_L = 256
HALF = D_MODEL // 2
N_ASSIGN = 4 * 4096 * TOP_K
R_SUB = 256
N_SUB = 3
NSB_MAX = -(-(N_ASSIGN + N_EXPERTS * (R_SUB - 1)) // R_SUB)
NE_MAX = N_EXPERTS + N_ASSIGN // (R_SUB * N_SUB)
GU_CHUNK = 256
N_A = D_EXPERT // GU_CHUNK
DN_CHUNK = 1024
N_B = D_MODEL // DN_CHUNK
GU_SLOTS = N_A + 1


def _cparams(sem):
    return pltpu.CompilerParams(dimension_semantics=sem, vmem_limit_bytes=VMEM_LIMIT)


def _rms(x, g):
    return x * lax.rsqrt(jnp.mean(x * x, axis=-1, keepdims=True) + EPS) * g


def _in_proj_kernel(x_ref, g_ref, w_ref, wg_ref, proj_ref, gate_ref, xn_sc):
    @pl.when(pl.program_id(1) == 0)
    def _():
        xb = _rms(x_ref[...], g_ref[...]).astype(BF16)
        xn_sc[...] = xb
        gate_ref[...] = jnp.dot(xb, wg_ref[...], preferred_element_type=F32)

    proj_ref[...] = jnp.dot(xn_sc[...], w_ref[...], preferred_element_type=F32).astype(BF16)


def _in_proj(x2, g, w_main, w_gate, tm=512, tn=1024):
    n = x2.shape[0]
    return pl.pallas_call(
        _in_proj_kernel,
        out_shape=(jax.ShapeDtypeStruct((n, D_PROJ_MAIN), BF16),
                   jax.ShapeDtypeStruct((n, LANES), F32)),
        grid=(n // tm, D_PROJ_MAIN // tn),
        in_specs=[pl.BlockSpec((tm, D_MODEL), lambda i, j: (i, 0)),
                  pl.BlockSpec((1, D_MODEL), lambda i, j: (0, 0)),
                  pl.BlockSpec((D_MODEL, tn), lambda i, j: (0, j)),
                  pl.BlockSpec((D_MODEL, LANES), lambda i, j: (0, 0))],
        out_specs=(pl.BlockSpec((tm, tn), lambda i, j: (i, j)),
                   pl.BlockSpec((tm, LANES), lambda i, j: (i, 0))),
        scratch_shapes=[pltpu.VMEM((tm, D_MODEL), BF16)],
        compiler_params=_cparams(("parallel", "arbitrary")),
        name="in_proj",
    )(x2, g, w_main, w_gate)


def _gmlp_kernel(u_ref, v_ref, gv_ref, w_ref, bt_ref, y_ref, wm_sc, *, rows):
    c = GMLP_CHUNK

    @pl.when(pl.program_id(0) == 0)
    def _():
        r = lax.broadcasted_iota(I32, (c, c), 0)
        s = lax.broadcasted_iota(I32, (c, c), 1)
        for g in range(GMLP_GROUPS):
            wm_sc[g] = jnp.where(s <= r, w_ref[g], 0.0).astype(BF16)

    for ci in range(rows // c):
        rs = slice(ci * c, (ci + 1) * c)
        gu = jax.nn.gelu(u_ref[rs, :].astype(F32))
        gv = jax.nn.gelu(v_ref[rs, :].astype(F32))
        vn = _rms(gv, gv_ref[...]).astype(BF16)
        for g in range(GMLP_GROUPS):
            cs = slice(g * c, (g + 1) * c)
            z = jnp.dot(wm_sc[g], vn[:, cs], preferred_element_type=F32) + bt_ref[:, g:g + 1]
            y_ref[rs, cs] = (gu[:, cs] * z).astype(BF16)


def _gmlp(proj, gv, w_s, b_t, rows=512):
    n = proj.shape[0]
    return pl.pallas_call(
        functools.partial(_gmlp_kernel, rows=rows),
        out_shape=jax.ShapeDtypeStruct((n, D_GMLP), BF16),
        grid=(n // rows,),
        in_specs=[pl.BlockSpec((rows, D_GMLP), lambda i: (i, 0)),
                  pl.BlockSpec((rows, D_GMLP), lambda i: (i, 1)),
                  pl.BlockSpec((1, D_GMLP), lambda i: (0, 0)),
                  pl.BlockSpec((GMLP_GROUPS, GMLP_CHUNK, GMLP_CHUNK), lambda i: (0, 0, 0)),
                  pl.BlockSpec((GMLP_CHUNK, GMLP_GROUPS), lambda i: (0, 0))],
        out_specs=pl.BlockSpec((rows, D_GMLP), lambda i: (i, 0)),
        scratch_shapes=[pltpu.VMEM((GMLP_GROUPS, GMLP_CHUNK, GMLP_CHUNK), BF16)],
        compiler_params=_cparams(("arbitrary",)),
        name="gmlp",
    )(proj, proj, gv, w_s, b_t)


def _mlstm_kernel(q_ref, k_ref, v_ref, o_ref, gp_ref, gb_ref, og_ref, y_ref, c_sc, n_sc, m_sc):
    L, H, DQK, DV = MLSTM_L, MLSTM_HEADS, MLSTM_DQK, MLSTM_DV
    scale = DQK ** -0.5

    @pl.when(pl.program_id(1) == 0)
    def _():
        c_sc[...] = jnp.zeros_like(c_sc)
        n_sc[...] = jnp.zeros_like(n_sc)
        m_sc[...] = jnp.zeros_like(m_sc)

    lane = lax.broadcasted_iota(I32, (L, LANES), 1)
    z = gp_ref[...] + gb_ref[...]
    sc = GATE_SOFTCAP * jnp.tanh(z / GATE_SOFTCAP)
    logf = -(jnp.maximum(-sc, 0.0) + jnp.log1p(jnp.exp(-jnp.abs(sc))))
    row = lax.broadcasted_iota(I32, (L, L), 0)
    col = lax.broadcasted_iota(I32, (L, L), 1)
    causal = col <= row
    tri = jnp.where(causal, 1.0, 0.0).astype(BF16)
    bcum, rest = None, logf
    for _ in range(3):
        part = rest.astype(BF16)
        rest = rest - part.astype(F32)
        term = jnp.dot(tri, part, preferred_element_type=F32)
        bcum = term if bcum is None else bcum + term
    gcol = jnp.where(lane < H, sc, bcum)
    grow = gcol.T

    for h in range(H):
        ig_col, b_col = gcol[:, h:h + 1], gcol[:, H + h:H + h + 1]
        ig_row, b_row = grow[h:h + 1, :], grow[H + h:H + h + 1, :]
        m = m_sc[h, 0:1, 0:1]
        q = q_ref[:, h * DQK:(h + 1) * DQK]
        k = k_ref[:, h * DQK:(h + 1) * DQK]
        v = v_ref[:, h * DV:(h + 1) * DV]

        d_log = jnp.where(causal, b_col - b_row + ig_row, -jnp.inf)
        inter = b_col + m
        m_row = jnp.maximum(inter, jnp.max(d_log, axis=-1, keepdims=True))
        w_intra = jnp.exp(d_log - m_row)
        w_inter = jnp.exp(inter - m_row)
        s = lax.dot_general(q, k, (((1,), (1,)), ((), ())), preferred_element_type=F32)
        s = s * scale * w_intra
        c_old = c_sc[h]
        n_old = n_sc[h]
        num = (w_inter * jnp.dot(q, c_old.astype(BF16), preferred_element_type=F32)
               + jnp.dot(s.astype(BF16), v, preferred_element_type=F32))
        qn = jnp.sum(q.astype(F32) * n_old, axis=-1, keepdims=True)
        den = w_inter * qn + jnp.sum(s, axis=-1, keepdims=True)
        hh = num / jnp.maximum(jnp.abs(den), jnp.exp(-m_row))

        b_last = b_row[:, L - 1:L]
        tail = b_last - b_col + ig_col
        m_new = jnp.maximum(b_last + m, jnp.max(tail, axis=0, keepdims=True))
        w_tail = jnp.exp(tail - m_new)
        decay = jnp.exp(b_last + m - m_new)
        kw = k.astype(F32) * scale * w_tail
        c_sc[h] = decay * c_old + lax.dot_general(
            kw.astype(BF16), v, (((0,), (0,)), ((), ())), preferred_element_type=F32)
        n_sc[h] = decay * n_old + jnp.sum(kw, axis=0, keepdims=True)
        m_sc[h] = jnp.broadcast_to(m_new, m_sc.shape[1:])

        hn = hh * lax.rsqrt(jnp.mean(hh * hh, axis=-1, keepdims=True) + EPS)
        hn = hn * og_ref[:, h * DV:(h + 1) * DV]
        hn = hn * jax.nn.sigmoid(o_ref[:, h * DV:(h + 1) * DV].astype(F32))
        y_ref[:, h * DV:(h + 1) * DV] = hn.astype(BF16)


def _mlstm(proj, gate_pre, gate_b, out_g, batch, seq):
    n = proj.shape[0]
    L = MLSTM_L
    nc = seq // L
    qk_w = MLSTM_HEADS * MLSTM_DQK
    q_blk, k_blk = 2 * D_GMLP // qk_w, 2 * D_GMLP // qk_w + 1
    v_blk, o_blk = (2 * D_GMLP + 2 * qk_w) // D_MLSTM, (2 * D_GMLP + 2 * qk_w) // D_MLSTM + 1
    rmap = lambda b, c: b * nc + c
    return pl.pallas_call(
        _mlstm_kernel,
        out_shape=jax.ShapeDtypeStruct((n, D_MLSTM), BF16),
        grid=(batch, nc),
        in_specs=[pl.BlockSpec((L, qk_w), lambda b, c: (rmap(b, c), q_blk)),
                  pl.BlockSpec((L, qk_w), lambda b, c: (rmap(b, c), k_blk)),
                  pl.BlockSpec((L, D_MLSTM), lambda b, c: (rmap(b, c), v_blk)),
                  pl.BlockSpec((L, D_MLSTM), lambda b, c: (rmap(b, c), o_blk)),
                  pl.BlockSpec((L, LANES), lambda b, c: (rmap(b, c), 0)),
                  pl.BlockSpec((1, LANES), lambda b, c: (0, 0)),
                  pl.BlockSpec((1, D_MLSTM), lambda b, c: (0, 0))],
        out_specs=pl.BlockSpec((L, D_MLSTM), lambda b, c: (rmap(b, c), 0)),
        scratch_shapes=[pltpu.VMEM((MLSTM_HEADS, MLSTM_DQK, MLSTM_DV), F32),
                        pltpu.VMEM((MLSTM_HEADS, 1, MLSTM_DQK), F32),
                        pltpu.VMEM((MLSTM_HEADS, 8, LANES), F32)],
        compiler_params=_cparams(("parallel", "arbitrary")),
        name="mlstm",
    )(proj, proj, proj, proj, gate_pre, gate_b, out_g)


def _out_proj_kernel(ya_ref, yb_ref, wa_ref, wb_ref, x_ref, o_ref):
    acc = jnp.dot(ya_ref[...], wa_ref[...], preferred_element_type=F32)
    acc = acc + jnp.dot(yb_ref[...], wb_ref[...], preferred_element_type=F32)
    o_ref[...] = x_ref[...] + acc


def _out_proj(y_a, y_b, w_out, x2, tm=512, tn=1024):
    n = x2.shape[0]
    return pl.pallas_call(
        _out_proj_kernel,
        out_shape=jax.ShapeDtypeStruct((n, D_MODEL), F32),
        grid=(n // tm, D_MODEL // tn),
        in_specs=[pl.BlockSpec((tm, D_GMLP), lambda i, j: (i, 0)),
                  pl.BlockSpec((tm, D_MLSTM), lambda i, j: (i, 0)),
                  pl.BlockSpec((D_GMLP, tn), lambda i, j: (0, j)),
                  pl.BlockSpec((D_MLSTM, tn), lambda i, j: (1, j)),
                  pl.BlockSpec((tm, tn), lambda i, j: (i, j))],
        out_specs=pl.BlockSpec((tm, tn), lambda i, j: (i, j)),
        compiler_params=_cparams(("parallel", "arbitrary")),
        name="out_proj",
    )(y_a, y_b, w_out, w_out, x2)


def _router_kernel(x_ref, g_ref, wr_ref, br_ref, xp_ref, meta_ref, wts_ref, cnt_ref, carry_sc, w2_sc,
                   *, tm):
    i = pl.program_id(0)

    @pl.when(i == 0)
    def _():
        carry_sc[...] = jnp.zeros_like(carry_sc)
        w_hi = wr_ref[...].astype(BF16)
        w2_sc[:, :LANES] = w_hi
        w2_sc[:, LANES:] = (wr_ref[...] - w_hi.astype(F32)).astype(BF16)

    xn = _rms(x_ref[...], g_ref[...])
    bits = lax.bitcast_convert_type(xn.astype(BF16).astype(F32), I32)
    xp_ref[...] = bits[:, :HALF] | lax.shift_right_logical(bits[:, HALF:], 16)

    x_hi = xn.astype(BF16)
    x_lo = (xn - x_hi.astype(F32)).astype(BF16)
    both = jnp.dot(x_hi, w2_sc[...], preferred_element_type=F32)
    logits = (both[:, :LANES] + both[:, LANES:]
              + jnp.dot(x_lo, w2_sc[:, :LANES], preferred_element_type=F32)) + br_ref[...]
    lane = lax.broadcasted_iota(I32, (tm, LANES), 1)
    lane_f = lane.astype(F32)
    big = float(LANES)
    is_g = (lane >= N_EXPERTS) & (lane < N_EXPERTS + N_GROUPS)
    gl = jnp.where(is_g, logits, -jnp.inf)
    gmax = jnp.max(gl, axis=-1, keepdims=True)
    g_lane = jnp.min(jnp.where(gl == gmax, lane_f, big), axis=-1, keepdims=True)
    g_sum = jnp.sum(jnp.where(is_g, jnp.exp(gl - gmax), 0.0), axis=-1, keepdims=True)
    g_prob = 1.0 / g_sum
    g_idx = g_lane.astype(I32) - N_EXPERTS
    in_grp = (lane < N_EXPERTS) & (lax.shift_right_logical(lane, 3) == g_idx)
    el = jnp.where(in_grp, logits, -jnp.inf)
    t1 = jnp.max(el, axis=-1, keepdims=True)
    i1 = jnp.min(jnp.where(el == t1, lane_f, big), axis=-1, keepdims=True)
    el2 = jnp.where(lane_f == i1, -jnp.inf, el)
    t2 = jnp.max(el2, axis=-1, keepdims=True)
    i2 = jnp.min(jnp.where(el2 == t2, lane_f, big), axis=-1, keepdims=True)
    e2 = jnp.exp(t2 - t1)
    w1 = g_prob * (1.0 / (1.0 + e2))
    w2 = g_prob * (e2 / (1.0 + e2))

    sel1 = lane_f == i1
    sel2 = lane_f == i2
    onehot = jnp.where(sel1 | sel2, 1.0, 0.0)
    r = lax.broadcasted_iota(I32, (tm, tm), 0)
    c = lax.broadcasted_iota(I32, (tm, tm), 1)
    strict = jnp.where(c < r, 1.0, 0.0).astype(BF16)
    prefix = jnp.dot(strict, onehot.astype(BF16), preferred_element_type=F32) + carry_sc[0:1, :]
    rank1 = jnp.sum(jnp.where(sel1, prefix, 0.0), axis=-1, keepdims=True)
    rank2 = jnp.sum(jnp.where(sel2, prefix, 0.0), axis=-1, keepdims=True)
    total = carry_sc[0:1, :] + jnp.sum(onehot, axis=0, keepdims=True)
    carry_sc[...] = jnp.broadcast_to(total, carry_sc.shape)
    cnt_ref[...] = jnp.broadcast_to(total, cnt_ref.shape).astype(I32)

    meta = jnp.where(lane == 0, i1, jnp.where(lane == 1, i2,
                     jnp.where(lane == 2, rank1, jnp.where(lane == 3, rank2, 0.0))))
    meta_ref[...] = meta.astype(I32)
    wts_ref[...] = jnp.where(lane == 0, w1, jnp.where(lane == 1, w2, 0.0))


def _router(x1, g, wr, br, tm=512):
    n = x1.shape[0]
    return pl.pallas_call(
        functools.partial(_router_kernel, tm=tm),
        out_shape=(jax.ShapeDtypeStruct((n, HALF), I32),
                   jax.ShapeDtypeStruct((n, LANES), I32),
                   jax.ShapeDtypeStruct((n, LANES), F32),
                   jax.ShapeDtypeStruct((8, LANES), I32)),
        grid=(n // tm,),
        in_specs=[pl.BlockSpec((tm, D_MODEL), lambda i: (i, 0)),
                  pl.BlockSpec((1, D_MODEL), lambda i: (0, 0)),
                  pl.BlockSpec((D_MODEL, LANES), lambda i: (0, 0)),
                  pl.BlockSpec((1, LANES), lambda i: (0, 0))],
        out_specs=(pl.BlockSpec((tm, HALF), lambda i: (i, 0)),
                   pl.BlockSpec((tm, LANES), lambda i: (i, 0)),
                   pl.BlockSpec((tm, LANES), lambda i: (i, 0)),
                   pl.BlockSpec((8, LANES), lambda i: (0, 0))),
        scratch_shapes=[pltpu.VMEM((8, LANES), F32),
                        pltpu.VMEM((D_MODEL, 2 * LANES), BF16)],
        compiler_params=_cparams(("arbitrary",)),
        name="router",
    )(x1, g, wr, br)


def _dispatch_kernel(e_ref, rk_ref, row0_ref, cnt_ref, misc_ref, xp_ref, xs_ref, zbuf, sem, zsem, *, tm):
    base = pl.program_id(0) * (tm * TOP_K)

    @pl.when(pl.program_id(0) == 0)
    def _():
        zbuf[...] = jnp.zeros_like(zbuf)

        def pad_fill(e, carry, *, wait):
            cnt = cnt_ref[e]
            start = row0_ref[e] + cnt
            head = (-cnt) & (SUBLANES - 1)
            for j in range(SUBLANES - 1):
                @pl.when(j < head)
                def _(j=j):
                    cp = pltpu.make_async_copy(zbuf.at[pl.ds(0, 1), :], xs_ref.at[pl.ds(start + j, 1), :], zsem)
                    cp.wait() if wait else cp.start()
            off = start + head
            rest = (-(cnt + head)) & (R_SUB - 1)
            bit = R_SUB // 2
            while bit >= SUBLANES:
                @pl.when((rest & bit) != 0)
                def _(off=off, bit=bit):
                    dst = xs_ref.at[pl.ds(pl.multiple_of(off, SUBLANES), bit), :]
                    cp = pltpu.make_async_copy(zbuf.at[pl.ds(0, bit), :], dst, zsem)
                    cp.wait() if wait else cp.start()
                off = off + (rest & bit)
                bit //= 2
            return carry

        def tail_fill(sb, carry, *, wait):
            r = pl.multiple_of(sb * R_SUB, R_SUB)
            cp = pltpu.make_async_copy(zbuf, xs_ref.at[pl.ds(r, R_SUB), :], zsem)
            cp.wait() if wait else cp.start()
            return carry

        for wait in (False, True):
            lax.fori_loop(0, N_EXPERTS, functools.partial(pad_fill, wait=wait), 0)
            lax.fori_loop(misc_ref[0], NSB_MAX, functools.partial(tail_fill, wait=wait), 0)

    def row_copy(r, kk):
        a = base + TOP_K * r + kk
        p = row0_ref[e_ref[a]] + rk_ref[a]
        return pltpu.make_async_copy(xp_ref.at[pl.ds(r, 1), :], xs_ref.at[pl.ds(p, 1), :], sem)

    def issue(r, carry):
        for kk in range(TOP_K):
            row_copy(r, kk).start()
        return carry

    lax.fori_loop(0, tm, issue, 0, unroll=8)
    for _ in range(TOP_K):
        pltpu.make_async_copy(xp_ref, xs_ref.at[pl.ds(0, tm), :], sem).wait()


def _dispatch(e_flat, rk_flat, row0, cnt, misc, xp, tm=512):
    n = xp.shape[0]
    return pl.pallas_call(
        functools.partial(_dispatch_kernel, tm=tm),
        out_shape=jax.ShapeDtypeStruct((NSB_MAX * R_SUB, HALF), I32),
        grid_spec=pltpu.PrefetchScalarGridSpec(
            num_scalar_prefetch=5,
            grid=(n // tm,),
            in_specs=[pl.BlockSpec((tm, HALF), lambda i, *_: (i, 0))],
            out_specs=pl.BlockSpec(memory_space=pl.ANY),
            scratch_shapes=[pltpu.VMEM((R_SUB, HALF), I32),
                            pltpu.SemaphoreType.DMA(()),
                            pltpu.SemaphoreType.DMA(())]),
        compiler_params=_cparams(("arbitrary",)),
        name="dispatch",
    )(e_flat, rk_flat, row0, cnt, misc, xp)


def _expert_kernel(ee_ref, sb0_ref, nsb_ref, misc_ref, xs_hbm, wg_hbm, wu_hbm, wd_hbm, os_hbm,
                   xbuf, hbuf, obuf, zbuf, gbuf, ubuf, dbuf, xsem, osem, zsem, gsem, dsem):
    s = pl.program_id(0)
    n_ent = pl.num_programs(0)
    nsb = nsb_ref[s]
    slot = lax.rem(s, 2)
    nxt = jnp.minimum(s + 1, n_ent - 1)
    has_next = (s + 1 < n_ent) & (nsb_ref[nxt] > 0)

    def sub_row(ent, sb):
        return pl.multiple_of((sb0_ref[ent] + sb) * R_SUB, R_SUB)

    def x_copies(ent, slot_, wait):
        for sb in range(N_SUB):
            @pl.when(sb < nsb_ref[ent])
            def _(sb=sb):
                cp = pltpu.make_async_copy(xs_hbm.at[pl.ds(sub_row(ent, sb), R_SUB), :],
                                           xbuf.at[slot_, sb], xsem.at[slot_])
                cp.wait() if wait else cp.start()

    def gu_slot(ent, c):
        return lax.rem(ent * N_A + c, GU_SLOTS)

    def gu_copies(ent, c, wait):
        e, sl = ee_ref[ent], gu_slot(ent, c)
        for w_hbm, buf in ((wg_hbm, gbuf), (wu_hbm, ubuf)):
            cp = pltpu.make_async_copy(w_hbm.at[e, :, pl.ds(c * GU_CHUNK, GU_CHUNK)], buf.at[sl], gsem.at[sl])
            cp.wait() if wait else cp.start()

    def dn_copy(ent, nb, wait):
        cp = pltpu.make_async_copy(wd_hbm.at[ee_ref[ent], :, pl.ds(nb * DN_CHUNK, DN_CHUNK)],
                                   dbuf.at[nb], dsem.at[nb])
        cp.wait() if wait else cp.start()

    def o_copy(ent, sb, nb, wait):
        cp = pltpu.make_async_copy(
            obuf.at[nb % 2, sb],
            os_hbm.at[pl.ds(sub_row(ent, sb), R_SUB), pl.ds(nb * DN_CHUNK, DN_CHUNK)], osem.at[nb % 2])
        cp.wait() if wait else cp.start()

    def o_waits(ent, nb):
        for sb in range(N_SUB):
            @pl.when(sb < nsb_ref[ent])
            def _(sb=sb):
                o_copy(ent, sb, nb, True)

    @pl.when(s == 0)
    def _():
        x_copies(0, 0, False)
        for c in range(N_A):
            gu_copies(0, c, False)
        for nb in range(N_B):
            dn_copy(0, nb, False)

    @pl.when(nsb == 0)
    def _():
        @pl.when(s >= 1)
        def _():
            for nb in range(N_B - 2, N_B):
                o_waits(s - 1, nb)

    @pl.when(nsb > 0)
    def _():
        x_copies(s, slot, True)

        @pl.when(s + 1 < n_ent)
        def _():
            x_copies(s + 1, 1 - slot, False)

        for c in range(N_A):
            gu_copies(s, c, True)

            @pl.when(has_next)
            def _(c=c):
                gu_copies(nxt, c, False)
            for sb in range(N_SUB):
                @pl.when(sb < nsb)
                def _(sb=sb, c=c):
                    w = xbuf[slot, sb]
                    xa = lax.bitcast_convert_type(w & jnp.int32(-65536), F32).astype(BF16)
                    xb = lax.bitcast_convert_type(lax.shift_left(w, 16), F32).astype(BF16)
                    wg, wu = gbuf.at[gu_slot(s, c)], ubuf.at[gu_slot(s, c)]
                    gate = (jnp.dot(xa, wg[:HALF, :].astype(BF16), preferred_element_type=F32)
                            + jnp.dot(xb, wg[HALF:, :].astype(BF16), preferred_element_type=F32))
                    up = (jnp.dot(xa, wu[:HALF, :].astype(BF16), preferred_element_type=F32)
                          + jnp.dot(xb, wu[HALF:, :].astype(BF16), preferred_element_type=F32))
                    hbuf[c, sb] = (jax.nn.silu(gate) * up).astype(BF16)

        for nb in range(N_B):
            if nb >= 2:
                o_waits(s, nb - 2)
            else:
                @pl.when(s >= 1)
                def _(nb=nb):
                    o_waits(s - 1, nb + N_B - 2)
            dn_copy(s, nb, True)
            for sb in range(N_SUB):
                @pl.when(sb < nsb)
                def _(sb=sb, nb=nb):
                    wd = dbuf.at[nb]
                    acc = jnp.dot(hbuf[0, sb], wd[0:GU_CHUNK, :].astype(BF16), preferred_element_type=F32)
                    for ca in range(1, N_A):
                        acc = acc + jnp.dot(hbuf[ca, sb],
                                            wd[ca * GU_CHUNK:(ca + 1) * GU_CHUNK, :].astype(BF16),
                                            preferred_element_type=F32)
                    obuf[nb % 2, sb] = acc
                    o_copy(s, sb, nb, False)

            @pl.when(has_next)
            def _(nb=nb):
                dn_copy(nxt, nb, False)

    @pl.when(s == n_ent - 1)
    def _():
        @pl.when(nsb > 0)
        def _():
            for nb in range(N_B - 2, N_B):
                o_waits(s, nb)
        zbuf[...] = jnp.zeros_like(zbuf)

        def tail_fill(sb, carry, *, wait):
            r = pl.multiple_of(sb * R_SUB, R_SUB)
            for nb in range(N_B):
                cp = pltpu.make_async_copy(
                    zbuf, os_hbm.at[pl.ds(r, R_SUB), pl.ds(nb * DN_CHUNK, DN_CHUNK)], zsem)
                cp.wait() if wait else cp.start()
            return carry

        for wait in (False, True):
            lax.fori_loop(misc_ref[0], NSB_MAX, functools.partial(tail_fill, wait=wait), 0)


def _experts(ent_e, ent_sb0, ent_nsb, misc, xs, w_gate, w_up, w_down):
    assert N_B % 2 == 0, "output column chunks alternate between two staging sets"
    any_spec = pl.BlockSpec(memory_space=pl.ANY)
    return pl.pallas_call(
        _expert_kernel,
        out_shape=jax.ShapeDtypeStruct((NSB_MAX * R_SUB, D_MODEL), F32),
        grid_spec=pltpu.PrefetchScalarGridSpec(
            num_scalar_prefetch=4,
            grid=(NE_MAX,),
            in_specs=[any_spec, any_spec, any_spec, any_spec],
            out_specs=any_spec,
            scratch_shapes=[pltpu.VMEM((2, N_SUB, R_SUB, HALF), I32),
                            pltpu.VMEM((N_A, N_SUB, R_SUB, GU_CHUNK), BF16),
                            pltpu.VMEM((2, N_SUB, R_SUB, DN_CHUNK), F32),
                            pltpu.VMEM((R_SUB, DN_CHUNK), F32),
                            pltpu.VMEM((GU_SLOTS, D_MODEL, GU_CHUNK), F32),
                            pltpu.VMEM((GU_SLOTS, D_MODEL, GU_CHUNK), F32),
                            pltpu.VMEM((N_B, D_EXPERT, DN_CHUNK), F32),
                            pltpu.SemaphoreType.DMA((2,)),
                            pltpu.SemaphoreType.DMA((2,)),
                            pltpu.SemaphoreType.DMA(()),
                            pltpu.SemaphoreType.DMA((GU_SLOTS,)),
                            pltpu.SemaphoreType.DMA((N_B,))]),
        compiler_params=_cparams(("arbitrary",)),
        name="experts",
    )(ent_e, ent_sb0, ent_nsb, misc, xs, w_gate, w_up, w_down)


def _combine_kernel(e_ref, rk_ref, row0_ref, x_ref, wts_ref, g_ref, os_ref, y_ref, gbuf, sem, *, tm):
    i = pl.program_id(0)
    slot = lax.rem(i, 2)

    def issue_tile(tile, slot_):
        base = tile * (tm * TOP_K)

        def issue(r, carry):
            for kk in range(TOP_K):
                a = base + TOP_K * r + kk
                p = row0_ref[e_ref[a]] + rk_ref[a]
                pltpu.make_async_copy(os_ref.at[pl.ds(p, 1), :], gbuf.at[slot_, kk, pl.ds(r, 1), :],
                                      sem.at[slot_]).start()
            return carry

        lax.fori_loop(0, tm, issue, 0, unroll=8)

    @pl.when(i == 0)
    def _():
        issue_tile(0, 0)

    @pl.when(i + 1 < pl.num_programs(0))
    def _():
        issue_tile(i + 1, 1 - slot)

    for kk in range(TOP_K):
        pltpu.make_async_copy(os_ref.at[pl.ds(0, tm), :], gbuf.at[slot, kk], sem.at[slot]).wait()
    moe = wts_ref[:, 0:1] * gbuf[slot, 0] + wts_ref[:, 1:2] * gbuf[slot, 1]
    y_ref[...] = _rms(x_ref[...] + moe, g_ref[...])


def _combine(e_flat, rk_flat, row0, x1, wts, g, out_sorted, tm=256):
    n = x1.shape[0]
    return pl.pallas_call(
        functools.partial(_combine_kernel, tm=tm),
        out_shape=jax.ShapeDtypeStruct((n, D_MODEL), F32),
        grid_spec=pltpu.PrefetchScalarGridSpec(
            num_scalar_prefetch=3,
            grid=(n // tm,),
            in_specs=[pl.BlockSpec((tm, D_MODEL), lambda i, *_: (i, 0)),
                      pl.BlockSpec((tm, LANES), lambda i, *_: (i, 0)),
                      pl.BlockSpec((1, D_MODEL), lambda i, *_: (0, 0)),
                      pl.BlockSpec(memory_space=pl.ANY)],
            out_specs=pl.BlockSpec((tm, D_MODEL), lambda i, *_: (i, 0)),
            scratch_shapes=[pltpu.VMEM((2, TOP_K, tm, D_MODEL), F32),
                            pltpu.SemaphoreType.DMA((2,))]),
        compiler_params=_cparams(("arbitrary",)),
        name="combine",
    )(e_flat, rk_flat, row0, x1, wts, g, out_sorted)


def _moe(x1, norm_g, wr_g, br_g, wr_e, br_e, w_gate, w_up, w_down, final_g):
    n = x1.shape[0]
    pad = LANES - N_EXPERTS - N_GROUPS
    wr = jnp.concatenate([wr_e, wr_g, jnp.zeros((D_MODEL, pad), F32)], axis=1)
    br = jnp.concatenate([br_e, br_g, jnp.zeros((pad,), F32)])[None]
    xp, meta, wts, counts = _router(x1, norm_g[None], wr, br)

    assert n * TOP_K == N_ASSIGN
    cnt = counts[0, :N_EXPERTS]
    nsb = (cnt + R_SUB - 1) // R_SUB
    sb_end = jnp.cumsum(nsb)
    sb_start = sb_end - nsb
    nent = (nsb + N_SUB - 1) // N_SUB
    ent_end = jnp.cumsum(nent)
    ent_start = ent_end - nent
    ids = jnp.arange(NE_MAX, dtype=I32)
    valid = ids < ent_end[-1]
    ent_e = jnp.minimum(jnp.sum(ids[:, None] >= ent_end[None, :], axis=1), N_EXPERTS - 1)
    ent_e = jnp.where(valid, ent_e, ent_e[ent_end[-1] - 1])
    within = (ids - ent_start[ent_e]) * N_SUB
    ent_sb0 = jnp.where(valid, sb_start[ent_e] + within, 0)
    ent_nsb = jnp.where(valid, jnp.clip(nsb[ent_e] - within, 0, N_SUB), 0)
    row0 = (sb_start * R_SUB).astype(I32)
    misc = sb_end[-1:].astype(I32)
    e_flat = meta[:, 0:TOP_K].reshape(-1)
    rk_flat = meta[:, TOP_K:2 * TOP_K].reshape(-1)

    xs = _dispatch(e_flat, rk_flat, row0, cnt, misc, xp)
    out_sorted = _experts(ent_e.astype(I32), ent_sb0.astype(I32), ent_nsb.astype(I32), misc, xs,
                          w_gate, w_up, w_down)
    return _combine(e_flat, rk_flat, row0, x1, wts, final_g[None], out_sorted)


def kernel(x, norm_mix_g, w_in, gmlp_v_norm_g, gmlp_spatial_w, gmlp_spatial_b, mlstm_igate_b,
           mlstm_fgate_b, mlstm_out_norm_g, w_out, norm_ffn_g, router_group_w, router_group_b,
           router_expert_w, router_expert_b, expert_w_gate, expert_w_up, expert_w_down, final_norm_g):
    batch, seq, d = x.shape
    n = batch * seq
    assert w_in.shape[0] == 1, "the final rmsnorm is fused after the single layer's MoE"
    l = 0
    x2 = x.reshape(n, d)
    w_main = w_in[l].astype(BF16)
    w_gcol = jnp.pad(w_in[l][:, D_PROJ_MAIN:], ((0, 0), (0, LANES - N_GATE_COLS))).astype(BF16)
    proj, gate_pre = _in_proj(x2, norm_mix_g[l][None], w_main, w_gcol)
    y_a = _gmlp(proj, gmlp_v_norm_g[l][None], gmlp_spatial_w[l], gmlp_spatial_b[l].T)
    gate_b = jnp.concatenate([mlstm_igate_b[l], mlstm_fgate_b[l],
                              jnp.zeros((LANES - N_GATE_COLS,), F32)])[None]
    y_b = _mlstm(proj, gate_pre, gate_b, mlstm_out_norm_g[l][None], batch, seq)
    x1 = _out_proj(y_a, y_b, w_out[l].astype(BF16), x2)
    out = _moe(x1, norm_ffn_g[l], router_group_w[l], router_group_b[l], router_expert_w[l],
               router_expert_b[l], expert_w_gate[l], expert_w_up[l], expert_w_down[l], final_norm_g)
    return out.reshape(batch, seq, d)
```

```python
import functools

import jax
import jax.numpy as jnp
from jax import lax
from jax.experimental import pallas as pl
from jax.experimental.pallas import tpu as pltpu

F32 = jnp.float32
BF16 = jnp.bfloat16
I32 = jnp.int32

D_MODEL = 4096
D_GMLP = 2048
GMLP_CHUNK = 128
GMLP_GROUPS = 16
D_MLSTM = 2048
MLSTM_HEADS = 4
MLSTM_DV = 512
MLSTM_DQK = 256
GATE_SOFTCAP = 15.0
N_GROUPS = 8
EXPERTS_PER_GROUP = 8
N_EXPERTS = 64
TOP_K = 2
D_EXPERT = 512
EPS = 1e-6
D_PROJ_MAIN = 10240
N_GATE_COLS = 8

LANES = 128
SUBLANES = 8
VMEM_LIMIT = 56 * 1024 * 1024
MLSTM_L = 256
HALF = D_MODEL // 2
N_ASSIGN = 4 * 4096 * TOP_K
R_SUB = 256
N_SUB = 3
NSB_MAX = -(-(N_ASSIGN + N_EXPERTS * (R_SUB - 1)) // R_SUB)
NE_MAX = N_EXPERTS + N_ASSIGN // (R_SUB * N_SUB)
GU_CHUNK = 256
N_A = D_EXPERT // GU_CHUNK
DN_CHUNK = 1024
N_B = D_MODEL // DN_CHUNK
GU_SLOTS = N_A + 1


def _cparams(sem):
    return pltpu.CompilerParams(dimension_semantics=sem, vmem_limit_bytes=VMEM_LIMIT)


def _rms(x, g):
    return x * lax.rsqrt(jnp.mean(x * x, axis=-1, keepdims=True) + EPS) * g


def _in_proj_kernel(x_hbm, g_ref, w_ref, wg_ref, proj_ref, gate_ref, x_buf, xn_sc, sem, *, tm, strip):
    i = pl.program_id(0)

    def x_copy(tile):
        return pltpu.make_async_copy(x_hbm.at[pl.ds(pl.multiple_of(tile * tm, tm), tm), :], x_buf, sem)

    @pl.when(pl.program_id(1) == 0)
    def _():
        @pl.when(i == 0)
        def _():
            x_copy(0).start()

        x_copy(i).wait()

        def norm(r, carry):
            rows = pl.ds(pl.multiple_of(r * strip, strip), strip)
            xn_sc[rows, :] = _rms(x_buf[rows, :], g_ref[...]).astype(BF16)
            return carry

        lax.fori_loop(0, tm // strip, norm, 0)

        @pl.when(i + 1 < pl.num_programs(0))
        def _():
            x_copy(i + 1).start()

        gate_ref[...] = jnp.dot(xn_sc[...], wg_ref[...], preferred_element_type=F32)

    proj_ref[...] = jnp.dot(xn_sc[...], w_ref[...], preferred_element_type=F32).astype(BF16)


def _in_proj(x2, g, w_main, w_gate, tm=1024, tn=1024, strip=32):
    n = x2.shape[0]
    return pl.pallas_call(
        functools.partial(_in_proj_kernel, tm=tm, strip=strip),
        out_shape=(jax.ShapeDtypeStruct((n, D_PROJ_MAIN), BF16),
                   jax.ShapeDtypeStruct((n, LANES), F32)),
        grid=(n // tm, D_PROJ_MAIN // tn),
        in_specs=[pl.BlockSpec(memory_space=pl.ANY),
                  pl.BlockSpec((1, D_MODEL), lambda i, j: (0, 0)),
                  pl.BlockSpec((D_MODEL, tn), lambda i, j: (0, j)),
                  pl.BlockSpec((D_MODEL, LANES), lambda i, j: (0, 0))],
        out_specs=(pl.BlockSpec((tm, tn), lambda i, j: (i, j)),
                   pl.BlockSpec((tm, LANES), lambda i, j: (i, 0))),
        scratch_shapes=[pltpu.VMEM((tm, D_MODEL), F32),
                        pltpu.VMEM((tm, D_MODEL), BF16),
                        pltpu.SemaphoreType.DMA(())],
        compiler_params=_cparams(("arbitrary", "arbitrary")),
        name="in_proj",
    )(x2, g, w_main, w_gate)


def _gmlp_kernel(u_ref, v_ref, gv_ref, w_ref, bt_ref, y_ref, wm_sc, *, rows):
    c = GMLP_CHUNK

    @pl.when(pl.program_id(0) == 0)
    def _():
        r = lax.broadcasted_iota(I32, (c, c), 0)
        s = lax.broadcasted_iota(I32, (c, c), 1)
        for g in range(GMLP_GROUPS):
            wm_sc[g] = jnp.where(s <= r, w_ref[g], 0.0).astype(BF16)

    for ci in range(rows // c):
        rs = slice(ci * c, (ci + 1) * c)
        gu = jax.nn.gelu(u_ref[rs, :].astype(F32))
        gv = jax.nn.gelu(v_ref[rs, :].astype(F32))
        vn = _rms(gv, gv_ref[...]).astype(BF16)
        for g in range(GMLP_GROUPS):
            cs = slice(g * c, (g + 1) * c)
            z = jnp.dot(wm_sc[g], vn[:, cs], preferred_element_type=F32) + bt_ref[:, g:g + 1]
            y_ref[rs, cs] = (gu[:, cs] * z).astype(BF16)


def _gmlp(proj, gv, w_s, b_t, rows=512):
    n = proj.shape[0]
    return pl.pallas_call(
        functools.partial(_gmlp_kernel, rows=rows),
        out_shape=jax.ShapeDtypeStruct((n, D_GMLP), BF16),
        grid=(n // rows,),
        in_specs=[pl.BlockSpec((rows, D_GMLP), lambda i: (i, 0)),
                  pl.BlockSpec((rows, D_GMLP), lambda i: (i, 1)),
                  pl.BlockSpec((1, D_GMLP), lambda i: (0, 0)),
                  pl.BlockSpec((GMLP_GROUPS, GMLP_CHUNK, GMLP_CHUNK), lambda i: (0, 0, 0)),
                  pl.BlockSpec((GMLP_CHUNK, GMLP_GROUPS), lambda i: (0, 0))],
        out_specs=pl.BlockSpec((rows, D_GMLP), lambda i: (i, 0)),
        scratch_shapes=[pltpu.VMEM((GMLP_GROUPS, GMLP_CHUNK, GMLP_CHUNK), BF16)],
        compiler_params=_cparams(("arbitrary",)),
        name="gmlp",
    )(proj, proj, gv, w_s, b_t)


def _mlstm_kernel(q_ref, k_ref, v_ref, o_ref, gp_ref, gb_ref, og_ref, y_ref, c_sc, n_sc, m_sc):
    L, H, DQK, DV = MLSTM_L, MLSTM_HEADS, MLSTM_DQK, MLSTM_DV
    scale = DQK ** -0.5

    @pl.when(pl.program_id(1) == 0)
    def _():
        c_sc[...] = jnp.zeros_like(c_sc)
        n_sc[...] = jnp.zeros_like(n_sc)
        m_sc[...] = jnp.zeros_like(m_sc)

    lane = lax.broadcasted_iota(I32, (L, LANES), 1)
    z = gp_ref[...] + gb_ref[...]
    sc = GATE_SOFTCAP * jnp.tanh(z / GATE_SOFTCAP)
    logf = -(jnp.maximum(-sc, 0.0) + jnp.log1p(jnp.exp(-jnp.abs(sc))))
    row = lax.broadcasted_iota(I32, (L, L), 0)
    col = lax.broadcasted_iota(I32, (L, L), 1)
    causal = col <= row
    tri = jnp.where(causal, 1.0, 0.0).astype(BF16)
    bcum, rest = None, logf
    for _ in range(3):
        part = rest.astype(BF16)
        rest = rest - part.astype(F32)
        term = jnp.dot(tri, part, preferred_element_type=F32)
        bcum = term if bcum is None else bcum + term
    gcol = jnp.where(lane < H, sc, bcum)
    grow = gcol.T

    for h in range(H):
        ig_col, b_col = gcol[:, h:h + 1], gcol[:, H + h:H + h + 1]
        ig_row, b_row = grow[h:h + 1, :], grow[H + h:H + h + 1, :]
        m = m_sc[h, 0:1, 0:1]
        q = q_ref[:, h * DQK:(h + 1) * DQK]
        k = k_ref[:, h * DQK:(h + 1) * DQK]
        v = v_ref[:, h * DV:(h + 1) * DV]

        d_log = jnp.where(causal, b_col - b_row + ig_row, -jnp.inf)
        inter = b_col + m
        m_row = jnp.maximum(inter, jnp.max(d_log, axis=-1, keepdims=True))
        w_intra = jnp.exp(d_log - m_row)
        w_inter = jnp.exp(inter - m_row)
        s = lax.dot_general(q, k, (((1,), (1,)), ((), ())), preferred_element_type=F32)
        s = s * scale * w_intra
        c_old = c_sc[h]
        n_old = n_sc[h]
        num = (w_inter * jnp.dot(q, c_old.astype(BF16), preferred_element_type=F32)
               + jnp.dot(s.astype(BF16), v, preferred_element_type=F32))
        qn = jnp.sum(q.astype(F32) * n_old, axis=-1, keepdims=True)
        den = w_inter * qn + jnp.sum(s, axis=-1, keepdims=True)
        hh = num / jnp.maximum(jnp.abs(den), jnp.exp(-m_row))

        b_last = b_row[:, L - 1:L]
        tail = b_last - b_col + ig_col
        m_new = jnp.maximum(b_last + m, jnp.max(tail, axis=0, keepdims=True))
        w_tail = jnp.exp(tail - m_new)
        decay = jnp.exp(b_last + m - m_new)
        kw = k.astype(F32) * scale * w_tail
        c_sc[h] = decay * c_old + lax.dot_general(
            kw.astype(BF16), v, (((0,), (0,)), ((), ())), preferred_element_type=F32)
        n_sc[h] = decay * n_old + jnp.sum(kw, axis=0, keepdims=True)
        m_sc[h] = jnp.broadcast_to(m_new, m_sc.shape[1:])

        hn = hh * lax.rsqrt(jnp.mean(hh * hh, axis=-1, keepdims=True) + EPS)
        hn = hn * og_ref[:, h * DV:(h + 1) * DV]
        hn = hn * jax.nn.sigmoid(o_ref[:, h * DV:(h + 1) * DV].astype(F32))
        y_ref[:, h * DV:(h + 1) * DV] = hn.astype(BF16)


def _mlstm(proj, gate_pre, gate_b, out_g, batch, seq):
    n = proj.shape[0]
    L = MLSTM_L
    nc = seq // L
    qk_w = MLSTM_HEADS * MLSTM_DQK
    q_blk, k_blk = 2 * D_GMLP // qk_w, 2 * D_GMLP // qk_w + 1
    v_blk, o_blk = (2 * D_GMLP + 2 * qk_w) // D_MLSTM, (2 * D_GMLP + 2 * qk_w) // D_MLSTM + 1
    rmap = lambda b, c: b * nc + c
    return pl.pallas_call(
        _mlstm_kernel,
        out_shape=jax.ShapeDtypeStruct((n, D_MLSTM), BF16),
        grid=(batch, nc),
        in_specs=[pl.BlockSpec((L, qk_w), lambda b, c: (rmap(b, c), q_blk)),
                  pl.BlockSpec((L, qk_w), lambda b, c: (rmap(b, c), k_blk)),
                  pl.BlockSpec((L, D_MLSTM), lambda b, c: (rmap(b, c), v_blk)),
                  pl.BlockSpec((L, D_MLSTM), lambda b, c: (rmap(b, c), o_blk)),
                  pl.BlockSpec((L, LANES), lambda b, c: (rmap(b, c), 0)),
                  pl.BlockSpec((1, LANES), lambda b, c: (0, 0)),
                  pl.BlockSpec((1, D_MLSTM), lambda b, c: (0, 0))],
        out_specs=pl.BlockSpec((L, D_MLSTM), lambda b, c: (rmap(b, c), 0)),
        scratch_shapes=[pltpu.VMEM((MLSTM_HEADS, MLSTM_DQK, MLSTM_DV), F32),
                        pltpu.VMEM((MLSTM_HEADS, 1, MLSTM_DQK), F32),
                        pltpu.VMEM((MLSTM_HEADS, 8, LANES), F32)],
        compiler_params=_cparams(("parallel", "arbitrary")),
        name="mlstm",
    )(proj, proj, proj, proj, gate_pre, gate_b, out_g)


def _out_proj_kernel(ya_ref, yb_ref, wa_ref, wb_ref, x_ref, o_ref):
    acc = jnp.dot(ya_ref[...], wa_ref[...], preferred_element_type=F32)
    acc = acc + jnp.dot(yb_ref[...], wb_ref[...], preferred_element_type=F32)
    o_ref[...] = x_ref[...] + acc


def _out_proj(y_a, y_b, w_out, x2, tm=1024, tn=1024):
    n = x2.shape[0]
    return pl.pallas_call(
        _out_proj_kernel,
        out_shape=jax.ShapeDtypeStruct((n, D_MODEL), F32),
        grid=(n // tm, D_MODEL // tn),
        in_specs=[pl.BlockSpec((tm, D_GMLP), lambda i, j: (i, 0)),
                  pl.BlockSpec((tm, D_MLSTM), lambda i, j: (i, 0)),
                  pl.BlockSpec((D_GMLP, tn), lambda i, j: (0, j)),
                  pl.BlockSpec((D_MLSTM, tn), lambda i, j: (1, j)),
                  pl.BlockSpec((tm, tn), lambda i, j: (i, j))],
        out_specs=pl.BlockSpec((tm, tn), lambda i, j: (i, j)),
        compiler_params=_cparams(("parallel", "arbitrary")),
        name="out_proj",
    )(y_a, y_b, w_out, w_out, x2)


def _router_kernel(x_ref, g_ref, wr_ref, br_ref, xp_ref, meta_ref, wts_ref, cnt_ref, carry_sc, w2_sc,
                   *, tm):
    i = pl.program_id(0)

    @pl.when(i == 0)
    def _():
        carry_sc[...] = jnp.zeros_like(carry_sc)
        w_hi = wr_ref[...].astype(BF16)
        w2_sc[:, :LANES] = w_hi
        w2_sc[:, LANES:] = (wr_ref[...] - w_hi.astype(F32)).astype(BF16)

    xn = _rms(x_ref[...], g_ref[...])
    bits = lax.bitcast_convert_type(xn.astype(BF16).astype(F32), I32)
    xp_ref[...] = bits[:, :HALF] | lax.shift_right_logical(bits[:, HALF:], 16)

    x_hi = xn.astype(BF16)
    x_lo = (xn - x_hi.astype(F32)).astype(BF16)
    both = jnp.dot(x_hi, w2_sc[...], preferred_element_type=F32)
    logits = (both[:, :LANES] + both[:, LANES:]
              + jnp.dot(x_lo, w2_sc[:, :LANES], preferred_element_type=F32)) + br_ref[...]
    lane = lax.broadcasted_iota(I32, (tm, LANES), 1)
    lane_f = lane.astype(F32)
    big = float(LANES)
    is_g = (lane >= N_EXPERTS) & (lane < N_EXPERTS + N_GROUPS)
    gl = jnp.where(is_g, logits, -jnp.inf)
    gmax = jnp.max(gl, axis=-1, keepdims=True)
    g_lane = jnp.min(jnp.where(gl == gmax, lane_f, big), axis=-1, keepdims=True)
    g_sum = jnp.sum(jnp.where(is_g, jnp.exp(gl - gmax), 0.0), axis=-1, keepdims=True)
    g_prob = 1.0 / g_sum
    g_idx = g_lane.astype(I32) - N_EXPERTS
    in_grp = (lane < N_EXPERTS) & (lax.shift_right_logical(lane, 3) == g_idx)
    el = jnp.where(in_grp, logits, -jnp.inf)
    t1 = jnp.max(el, axis=-1, keepdims=True)
    i1 = jnp.min(jnp.where(el == t1, lane_f, big), axis=-1, keepdims=True)
    el2 = jnp.where(lane_f == i1, -jnp.inf, el)
    t2 = jnp.max(el2, axis=-1, keepdims=True)
    i2 = jnp.min(jnp.where(el2 == t2, lane_f, big), axis=-1, keepdims=True)
    e2 = jnp.exp(t2 - t1)
    w1 = g_prob * (1.0 / (1.0 + e2))
    w2 = g_prob * (e2 / (1.0 + e2))

    sel1 = lane_f == i1
    sel2 = lane_f == i2
    onehot = jnp.where(sel1 | sel2, 1.0, 0.0)
    r = lax.broadcasted_iota(I32, (tm, tm), 0)
    c = lax.broadcasted_iota(I32, (tm, tm), 1)
    strict = jnp.where(c < r, 1.0, 0.0).astype(BF16)
    prefix = jnp.dot(strict, onehot.astype(BF16), preferred_element_type=F32) + carry_sc[0:1, :]
    rank1 = jnp.sum(jnp.where(sel1, prefix, 0.0), axis=-1, keepdims=True)
    rank2 = jnp.sum(jnp.where(sel2, prefix, 0.0), axis=-1, keepdims=True)
    total = carry_sc[0:1, :] + jnp.sum(onehot, axis=0, keepdims=True)
    carry_sc[...] = jnp.broadcast_to(total, carry_sc.shape)
    cnt_ref[...] = jnp.broadcast_to(total, cnt_ref.shape).astype(I32)

    meta = jnp.where(lane == 0, i1, jnp.where(lane == 1, i2,
                     jnp.where(lane == 2, rank1, jnp.where(lane == 3, rank2, 0.0))))
    meta_ref[...] = meta.astype(I32)
    wts_ref[...] = jnp.where(lane == 0, w1, jnp.where(lane == 1, w2, 0.0))


def _router(x1, g, wr, br, tm=512):
    n = x1.shape[0]
    return pl.pallas_call(
        functools.partial(_router_kernel, tm=tm),
        out_shape=(jax.ShapeDtypeStruct((n, HALF), I32),
                   jax.ShapeDtypeStruct((n, LANES), I32),
                   jax.ShapeDtypeStruct((n, LANES), F32),
                   jax.ShapeDtypeStruct((8, LANES), I32)),
        grid=(n // tm,),
        in_specs=[pl.BlockSpec((tm, D_MODEL), lambda i: (i, 0)),
                  pl.BlockSpec((1, D_MODEL), lambda i: (0, 0)),
                  pl.BlockSpec((D_MODEL, LANES), lambda i: (0, 0)),
                  pl.BlockSpec((1, LANES), lambda i: (0, 0))],
        out_specs=(pl.BlockSpec((tm, HALF), lambda i: (i, 0)),
                   pl.BlockSpec((tm, LANES), lambda i: (i, 0)),
                   pl.BlockSpec((tm, LANES), lambda i: (i, 0)),
                   pl.BlockSpec((8, LANES), lambda i: (0, 0))),
        scratch_shapes=[pltpu.VMEM((8, LANES), F32),
                        pltpu.VMEM((D_MODEL, 2 * LANES), BF16)],
        compiler_params=_cparams(("arbitrary",)),
        name="router",
    )(x1, g, wr, br)


def _dispatch_kernel(e_ref, rk_ref, row0_ref, cnt_ref, misc_ref, xp_ref, xs_ref, zbuf, sem, zsem, *, tm):
    base = pl.program_id(0) * (tm * TOP_K)

    @pl.when(pl.program_id(0) == 0)
    def _():
        zbuf[...] = jnp.zeros_like(zbuf)

        def pad_fill(e, carry, *, wait):
            cnt = cnt_ref[e]
            start = row0_ref[e] + cnt
            head = (-cnt) & (SUBLANES - 1)
            for j in range(SUBLANES - 1):
                @pl.when(j < head)
                def _(j=j):
                    cp = pltpu.make_async_copy(zbuf.at[pl.ds(0, 1), :], xs_ref.at[pl.ds(start + j, 1), :], zsem)
                    cp.wait() if wait else cp.start()
            off = start + head
            rest = (-(cnt + head)) & (R_SUB - 1)
            bit = R_SUB // 2
            while bit >= SUBLANES:
                @pl.when((rest & bit) != 0)
                def _(off=off, bit=bit):
                    dst = xs_ref.at[pl.ds(pl.multiple_of(off, SUBLANES), bit), :]
                    cp = pltpu.make_async_copy(zbuf.at[pl.ds(0, bit), :], dst, zsem)
                    cp.wait() if wait else cp.start()
                off = off + (rest & bit)
                bit //= 2
            return carry

        def tail_fill(sb, carry, *, wait):
            r = pl.multiple_of(sb * R_SUB, R_SUB)
            cp = pltpu.make_async_copy(zbuf, xs_ref.at[pl.ds(r, R_SUB), :], zsem)
            cp.wait() if wait else cp.start()
            return carry

        for wait in (False, True):
            lax.fori_loop(0, N_EXPERTS, functools.partial(pad_fill, wait=wait), 0)
            lax.fori_loop(misc_ref[0], NSB_MAX, functools.partial(tail_fill, wait=wait), 0)

    def row_copy(r, kk):
        a = base + TOP_K * r + kk
        p = row0_ref[e_ref[a]] + rk_ref[a]
        return pltpu.make_async_copy(xp_ref.at[pl.ds(r, 1), :], xs_ref.at[pl.ds(p, 1), :], sem)

    def issue(r, carry):
        for kk in range(TOP_K):
            row_copy(r, kk).start()
        return carry

    lax.fori_loop(0, tm, issue, 0, unroll=8)
    for _ in range(TOP_K):
        pltpu.make_async_copy(xp_ref, xs_ref.at[pl.ds(0, tm), :], sem).wait()


def _dispatch(e_flat, rk_flat, row0, cnt, misc, xp, tm=512):
    n = xp.shape[0]
    return pl.pallas_call(
        functools.partial(_dispatch_kernel, tm=tm),
        out_shape=jax.ShapeDtypeStruct((NSB_MAX * R_SUB, HALF), I32),
        grid_spec=pltpu.PrefetchScalarGridSpec(
            num_scalar_prefetch=5,
            grid=(n // tm,),
            in_specs=[pl.BlockSpec((tm, HALF), lambda i, *_: (i, 0))],
            out_specs=pl.BlockSpec(memory_space=pl.ANY),
            scratch_shapes=[pltpu.VMEM((R_SUB, HALF), I32),
                            pltpu.SemaphoreType.DMA(()),
                            pltpu.SemaphoreType.DMA(())]),
        compiler_params=_cparams(("arbitrary",)),
        name="dispatch",
    )(e_flat, rk_flat, row0, cnt, misc, xp)


def _expert_kernel(ee_ref, sb0_ref, nsb_ref, misc_ref, xs_hbm, wg_hbm, wu_hbm, wd_hbm, os_hbm,
                   xbuf, hbuf, obuf, zbuf, gbuf, ubuf, dbuf, xsem, osem, zsem, gsem, dsem):
    s = pl.program_id(0)
    n_ent = pl.num_programs(0)
    nsb = nsb_ref[s]
    slot = lax.rem(s, 2)
    nxt = jnp.minimum(s + 1, n_ent - 1)
    has_next = (s + 1 < n_ent) & (nsb_ref[nxt] > 0)

    def sub_row(ent, sb):
        return pl.multiple_of((sb0_ref[ent] + sb) * R_SUB, R_SUB)

    def x_copies(ent, slot_, wait):
        for sb in range(N_SUB):
            @pl.when(sb < nsb_ref[ent])
            def _(sb=sb):
                cp = pltpu.make_async_copy(xs_hbm.at[pl.ds(sub_row(ent, sb), R_SUB), :],
                                           xbuf.at[slot_, sb], xsem.at[slot_])
                cp.wait() if wait else cp.start()

    def gu_slot(ent, c):
        return lax.rem(ent * N_A + c, GU_SLOTS)

    def gu_copies(ent, c, wait):
        e, sl = ee_ref[ent], gu_slot(ent, c)
        for w_hbm, buf in ((wg_hbm, gbuf), (wu_hbm, ubuf)):
            cp = pltpu.make_async_copy(w_hbm.at[e, :, pl.ds(c * GU_CHUNK, GU_CHUNK)], buf.at[sl], gsem.at[sl])
            cp.wait() if wait else cp.start()

    def dn_copy(ent, nb, wait):
        cp = pltpu.make_async_copy(wd_hbm.at[ee_ref[ent], :, pl.ds(nb * DN_CHUNK, DN_CHUNK)],
                                   dbuf.at[nb], dsem.at[nb])
        cp.wait() if wait else cp.start()

    def o_copy(ent, sb, nb, wait):
        cp = pltpu.make_async_copy(
            obuf.at[nb % 2, sb],
            os_hbm.at[pl.ds(sub_row(ent, sb), R_SUB), pl.ds(nb * DN_CHUNK, DN_CHUNK)], osem.at[nb % 2])
        cp.wait() if wait else cp.start()

    def o_waits(ent, nb):
        for sb in range(N_SUB):
            @pl.when(sb < nsb_ref[ent])
            def _(sb=sb):
                o_copy(ent, sb, nb, True)

    @pl.when(s == 0)
    def _():
        x_copies(0, 0, False)
        for c in range(N_A):
            gu_copies(0, c, False)
        for nb in range(N_B):
            dn_copy(0, nb, False)

    @pl.when(nsb == 0)
    def _():
        @pl.when(s >= 1)
        def _():
            for nb in range(N_B - 2, N_B):
                o_waits(s - 1, nb)

    @pl.when(nsb > 0)
    def _():
        x_copies(s, slot, True)

        @pl.when(s + 1 < n_ent)
        def _():
            x_copies(s + 1, 1 - slot, False)

        for c in range(N_A):
            gu_copies(s, c, True)

            @pl.when(has_next)
            def _(c=c):
                gu_copies(nxt, c, False)
            for sb in range(N_SUB):
                @pl.when(sb < nsb)
                def _(sb=sb, c=c):
                    w = xbuf[slot, sb]
                    xa = lax.bitcast_convert_type(w & jnp.int32(-65536), F32).astype(BF16)
                    xb = lax.bitcast_convert_type(lax.shift_left(w, 16), F32).astype(BF16)
                    wg, wu = gbuf.at[gu_slot(s, c)], ubuf.at[gu_slot(s, c)]
                    gate = (jnp.dot(xa, wg[:HALF, :].astype(BF16), preferred_element_type=F32)
                            + jnp.dot(xb, wg[HALF:, :].astype(BF16), preferred_element_type=F32))
                    up = (jnp.dot(xa, wu[:HALF, :].astype(BF16), preferred_element_type=F32)
                          + jnp.dot(xb, wu[HALF:, :].astype(BF16), preferred_element_type=F32))
                    hbuf[c, sb] = (jax.nn.silu(gate) * up).astype(BF16)

        for nb in range(N_B):
            if nb >= 2:
                o_waits(s, nb - 2)
            else:
                @pl.when(s >= 1)
                def _(nb=nb):
                    o_waits(s - 1, nb + N_B - 2)
            dn_copy(s, nb, True)
            for sb in range(N_SUB):
                @pl.when(sb < nsb)
                def _(sb=sb, nb=nb):
                    wd = dbuf.at[nb]
                    acc = jnp.dot(hbuf[0, sb], wd[0:GU_CHUNK, :].astype(BF16), preferred_element_type=F32)
                    for ca in range(1, N_A):
                        acc = acc + jnp.dot(hbuf[ca, sb],
                                            wd[ca * GU_CHUNK:(ca + 1) * GU_CHUNK, :].astype(BF16),
                                            preferred_element_type=F32)
                    obuf[nb % 2, sb] = acc
                    o_copy(s, sb, nb, False)

            @pl.when(has_next)
            def _(nb=nb):
                dn_copy(nxt, nb, False)

    @pl.when(s == n_ent - 1)
    def _():
        @pl.when(nsb > 0)
        def _():
            for nb in range(N_B - 2, N_B):
                o_waits(s, nb)
        zbuf[...] = jnp.zeros_like(zbuf)

        def tail_fill(sb, carry, *, wait):
            r = pl.multiple_of(sb * R_SUB, R_SUB)
            for nb in range(N_B):
                cp = pltpu.make_async_copy(
                    zbuf, os_hbm.at[pl.ds(r, R_SUB), pl.ds(nb * DN_CHUNK, DN_CHUNK)], zsem)
                cp.wait() if wait else cp.start()
            return carry

        for wait in (False, True):
            lax.fori_loop(misc_ref[0], NSB_MAX, functools.partial(tail_fill, wait=wait), 0)


def _experts(ent_e, ent_sb0, ent_nsb, misc, xs, w_gate, w_up, w_down):
    assert N_B % 2 == 0, "output column chunks alternate between two staging sets"
    any_spec = pl.BlockSpec(memory_space=pl.ANY)
    return pl.pallas_call(
        _expert_kernel,
        out_shape=jax.ShapeDtypeStruct((NSB_MAX * R_SUB, D_MODEL), F32),
        grid_spec=pltpu.PrefetchScalarGridSpec(
            num_scalar_prefetch=4,
            grid=(NE_MAX,),
            in_specs=[any_spec, any_spec, any_spec, any_spec],
            out_specs=any_spec,
            scratch_shapes=[pltpu.VMEM((2, N_SUB, R_SUB, HALF), I32),
                            pltpu.VMEM((N_A, N_SUB, R_SUB, GU_CHUNK), BF16),
                            pltpu.VMEM((2, N_SUB, R_SUB, DN_CHUNK), F32),
                            pltpu.VMEM((R_SUB, DN_CHUNK), F32),
                            pltpu.VMEM((GU_SLOTS, D_MODEL, GU_CHUNK), F32),
                            pltpu.VMEM((GU_SLOTS, D_MODEL, GU_CHUNK), F32),
                            pltpu.VMEM((N_B, D_EXPERT, DN_CHUNK), F32),
                            pltpu.SemaphoreType.DMA((2,)),
                            pltpu.SemaphoreType.DMA((2,)),
                            pltpu.SemaphoreType.DMA(()),
                            pltpu.SemaphoreType.DMA((GU_SLOTS,)),
                            pltpu.SemaphoreType.DMA((N_B,))]),
        compiler_params=_cparams(("arbitrary",)),
        name="experts",
    )(ent_e, ent_sb0, ent_nsb, misc, xs, w_gate, w_up, w_down)


def _combine_kernel(e_ref, rk_ref, row0_ref, x_ref, wts_ref, g_ref, os_ref, y_ref, gbuf, sem, *, tm):
    i = pl.program_id(0)
    slot = lax.rem(i, 2)

    def issue_tile(tile, slot_):
        base = tile * (tm * TOP_K)

        def issue(r, carry):
            for kk in range(TOP_K):
                a = base + TOP_K * r + kk
                p = row0_ref[e_ref[a]] + rk_ref[a]
                pltpu.make_async_copy(os_ref.at[pl.ds(p, 1), :], gbuf.at[slot_, kk, pl.ds(r, 1), :],
                                      sem.at[slot_]).start()
            return carry

        lax.fori_loop(0, tm, issue, 0, unroll=8)

    @pl.when(i == 0)
    def _():
        issue_tile(0, 0)

    @pl.when(i + 1 < pl.num_programs(0))
    def _():
        issue_tile(i + 1, 1 - slot)

    for kk in range(TOP_K):
        pltpu.make_async_copy(os_ref.at[pl.ds(0, tm), :], gbuf.at[slot, kk], sem.at[slot]).wait()
    moe = wts_ref[:, 0:1] * gbuf[slot, 0] + wts_ref[:, 1:2] * gbuf[slot, 1]
    y_ref[...] = _rms(x_ref[...] + moe, g_ref[...])


def _combine(e_flat, rk_flat, row0, x1, wts, g, out_sorted, tm=256):
    n = x1.shape[0]
    return pl.pallas_call(
        functools.partial(_combine_kernel, tm=tm),
        out_shape=jax.ShapeDtypeStruct((n, D_MODEL), F32),
        grid_spec=pltpu.PrefetchScalarGridSpec(
            num_scalar_prefetch=3,
            grid=(n // tm,),
            in_specs=[pl.BlockSpec((tm, D_MODEL), lambda i, *_: (i, 0)),
                      pl.BlockSpec((tm, LANES), lambda i, *_: (i, 0)),
                      pl.BlockSpec((1, D_MODEL), lambda i, *_: (0, 0)),
                      pl.BlockSpec(memory_space=pl.ANY)],
            out_specs=pl.BlockSpec((tm, D_MODEL), lambda i, *_: (i, 0)),
            scratch_shapes=[pltpu.VMEM((2, TOP_K, tm, D_MODEL), F32),
                            pltpu.SemaphoreType.DMA((2,))]),
        compiler_params=_cparams(("arbitrary",)),
        name="combine",
    )(e_flat, rk_flat, row0, x1, wts, g, out_sorted)


def _moe(x1, norm_g, wr_g, br_g, wr_e, br_e, w_gate, w_up, w_down, final_g):
    n = x1.shape[0]
    pad = LANES - N_EXPERTS - N_GROUPS
    wr = jnp.concatenate([wr_e, wr_g, jnp.zeros((D_MODEL, pad), F32)], axis=1)
    br = jnp.concatenate([br_e, br_g, jnp.zeros((pad,), F32)])[None]
    xp, meta, wts, counts = _router(x1, norm_g[None], wr, br)

    assert n * TOP_K == N_ASSIGN
    cnt = counts[0, :N_EXPERTS]
    nsb = (cnt + R_SUB - 1) // R_SUB
    sb_end = jnp.cumsum(nsb)
    sb_start = sb_end - nsb
    nent = (nsb + N_SUB - 1) // N_SUB
    ent_end = jnp.cumsum(nent)
    ent_start = ent_end - nent
    ids = jnp.arange(NE_MAX, dtype=I32)
    valid = ids < ent_end[-1]
    ent_e = jnp.minimum(jnp.sum(ids[:, None] >= ent_end[None, :], axis=1), N_EXPERTS - 1)
    ent_e = jnp.where(valid, ent_e, ent_e[ent_end[-1] - 1])
    within = (ids - ent_start[ent_e]) * N_SUB
    ent_sb0 = jnp.where(valid, sb_start[ent_e] + within, 0)
    ent_nsb = jnp.where(valid, jnp.clip(nsb[ent_e] - within, 0, N_SUB), 0)
    row0 = (sb_start * R_SUB).astype(I32)
    misc = sb_end[-1:].astype(I32)
    e_flat = meta[:, 0:TOP_K].reshape(-1)
    rk_flat = meta[:, TOP_K:2 * TOP_K].reshape(-1)

    xs = _dispatch(e_flat, rk_flat, row0, cnt, misc, xp)
    out_sorted = _experts(ent_e.astype(I32), ent_sb0.astype(I32), ent_nsb.astype(I32), misc, xs,
                          w_gate, w_up, w_down)
    return _combine(e_flat, rk_flat, row0, x1, wts, final_g[None], out_sorted)


def kernel(x, norm_mix_g, w_in, gmlp_v_norm_g, gmlp_spatial_w, gmlp_spatial_b, mlstm_igate_b,
           mlstm_fgate_b, mlstm_out_norm_g, w_out, norm_ffn_g, router_group_w, router_group_b,
           router_expert_w, router_expert_b, expert_w_gate, expert_w_up, expert_w_down, final_norm_g):
    batch, seq, d = x.shape
    n = batch * seq
    assert w_in.shape[0] == 1, "the final rmsnorm is fused after the single layer's MoE"
    l = 0
    x2 = x.reshape(n, d)
    w_main = w_in[l].astype(BF16)
    w_gcol = jnp.pad(w_in[l][:, D_PROJ_MAIN:], ((0, 0), (0, LANES - N_GATE_COLS))).astype(BF16)
    proj, gate_pre = _in_proj(x2, norm_mix_g[l][None], w_main, w_gcol)
    y_a = _gmlp(proj, gmlp_v_norm_g[l][None], gmlp_spatial_w[l], gmlp_spatial_b[l].T)
    gate_b = jnp.concatenate([mlstm_igate_b[l], mlstm_fgate_b[l],
                              jnp.zeros((LANES - N_GATE_COLS,), F32)])[None]
    y_b = _mlstm(proj, gate_pre, gate_b, mlstm_out_norm_g[l][None], batch, seq)
    x1 = _out_proj(y_a, y_b, w_out[l].astype(BF16), x2)
    out = _moe(x1, norm_ffn_g[l], router_group_w[l], router_group_b[l], router_expert_w[l],
               router_expert_b[l], expert_w_gate[l], expert_w_up[l], expert_w_down[l], final_norm_g)
    return out.reshape(batch, seq, d)
```

```python
import functools

import jax
import jax.numpy as jnp
from jax import lax
from jax.experimental import pallas as pl
from jax.experimental.pallas import tpu as pltpu

F32 = jnp.float32
BF16 = jnp.bfloat16
I32 = jnp.int32

D_MODEL = 4096
D_GMLP = 2048
GMLP_CHUNK = 128
GMLP_GROUPS = 16
D_MLSTM = 2048
MLSTM_HEADS = 4
MLSTM_DV = 512
MLSTM_DQK = 256
GATE_SOFTCAP = 15.0
N_GROUPS = 8
EXPERTS_PER_GROUP = 8
N_EXPERTS = 64
TOP_K = 2
D_EXPERT = 512
EPS = 1e-6
D_PROJ_MAIN = 10240
N_GATE_COLS = 8

LANES = 128
SUBLANES = 8
VMEM_LIMIT = 56 * 1024 * 1024
MLSTM_L = 256
HALF = D_MODEL // 2
N_ASSIGN = 4 * 4096 * TOP_K
R_SUB = 256
N_SUB = 3
NSB_MAX = -(-(N_ASSIGN + N_EXPERTS * (R_SUB - 1)) // R_SUB)
NE_MAX = N_EXPERTS + N_ASSIGN // (R_SUB * N_SUB)
GU_CHUNK = 256
N_A = D_EXPERT // GU_CHUNK
DN_CHUNK = 1024
N_B = D_MODEL // DN_CHUNK
GU_SLOTS = N_A + 1


def _cparams(sem):
    return pltpu.CompilerParams(dimension_semantics=sem, vmem_limit_bytes=VMEM_LIMIT)


def _rms(x, g):
    return x * lax.rsqrt(jnp.mean(x * x, axis=-1, keepdims=True) + EPS) * g


def _in_proj_kernel(x_hbm, g_ref, w_ref, wg_ref, proj_ref, gate_ref, x_buf, xn_sc, sem, *, tm, strip):
    i = pl.program_id(0)

    def x_copy(tile):
        return pltpu.make_async_copy(x_hbm.at[pl.ds(pl.multiple_of(tile * tm, tm), tm), :], x_buf, sem)

    @pl.when(pl.program_id(1) == 0)
    def _():
        @pl.when(i == 0)
        def _():
            x_copy(0).start()

        x_copy(i).wait()

        def norm(r, carry):
            rows = pl.ds(pl.multiple_of(r * strip, strip), strip)
            xn_sc[rows, :] = _rms(x_buf[rows, :], g_ref[...]).astype(BF16)
            return carry

        lax.fori_loop(0, tm // strip, norm, 0)

        @pl.when(i + 1 < pl.num_programs(0))
        def _():
            x_copy(i + 1).start()

        gate_ref[...] = jnp.dot(xn_sc[...], wg_ref[...], preferred_element_type=F32)

    proj_ref[...] = jnp.dot(xn_sc[...], w_ref[...].astype(BF16), preferred_element_type=F32).astype(BF16)


def _in_proj(x2, g, w_main, w_gate, tm=1024, tn=512, strip=32):
    n = x2.shape[0]
    return pl.pallas_call(
        functools.partial(_in_proj_kernel, tm=tm, strip=strip),
        out_shape=(jax.ShapeDtypeStruct((n, D_PROJ_MAIN), BF16),
                   jax.ShapeDtypeStruct((n, LANES), F32)),
        grid=(n // tm, D_PROJ_MAIN // tn),
        in_specs=[pl.BlockSpec(memory_space=pl.ANY),
                  pl.BlockSpec((1, D_MODEL), lambda i, j: (0, 0)),
                  pl.BlockSpec((D_MODEL, tn), lambda i, j: (0, j)),
                  pl.BlockSpec((D_MODEL, LANES), lambda i, j: (0, 0))],
        out_specs=(pl.BlockSpec((tm, tn), lambda i, j: (i, j)),
                   pl.BlockSpec((tm, LANES), lambda i, j: (i, 0))),
        scratch_shapes=[pltpu.VMEM((tm, D_MODEL), F32),
                        pltpu.VMEM((tm, D_MODEL), BF16),
                        pltpu.SemaphoreType.DMA(())],
        compiler_params=_cparams(("arbitrary", "arbitrary")),
        name="in_proj",
    )(x2, g, w_main, w_gate)


def _gmlp_kernel(u_ref, v_ref, gv_ref, w_ref, bt_ref, y_ref, wm_sc, *, rows):
    c = GMLP_CHUNK

    @pl.when(pl.program_id(0) == 0)
    def _():
        r = lax.broadcasted_iota(I32, (c, c), 0)
        s = lax.broadcasted_iota(I32, (c, c), 1)
        for g in range(GMLP_GROUPS):
            wm_sc[g] = jnp.where(s <= r, w_ref[g], 0.0).astype(BF16)

    for ci in range(rows // c):
        rs = slice(ci * c, (ci + 1) * c)
        gu = jax.nn.gelu(u_ref[rs, :].astype(F32))
        gv = jax.nn.gelu(v_ref[rs, :].astype(F32))
        vn = _rms(gv, gv_ref[...]).astype(BF16)
        for g in range(GMLP_GROUPS):
            cs = slice(g * c, (g + 1) * c)
            z = jnp.dot(wm_sc[g], vn[:, cs], preferred_element_type=F32) + bt_ref[:, g:g + 1]
            y_ref[rs, cs] = (gu[:, cs] * z).astype(BF16)


def _gmlp(proj, gv, w_s, b_t, rows=512):
    n = proj.shape[0]
    return pl.pallas_call(
        functools.partial(_gmlp_kernel, rows=rows),
        out_shape=jax.ShapeDtypeStruct((n, D_GMLP), BF16),
        grid=(n // rows,),
        in_specs=[pl.BlockSpec((rows, D_GMLP), lambda i: (i, 0)),
                  pl.BlockSpec((rows, D_GMLP), lambda i: (i, 1)),
                  pl.BlockSpec((1, D_GMLP), lambda i: (0, 0)),
                  pl.BlockSpec((GMLP_GROUPS, GMLP_CHUNK, GMLP_CHUNK), lambda i: (0, 0, 0)),
                  pl.BlockSpec((GMLP_CHUNK, GMLP_GROUPS), lambda i: (0, 0))],
        out_specs=pl.BlockSpec((rows, D_GMLP), lambda i: (i, 0)),
        scratch_shapes=[pltpu.VMEM((GMLP_GROUPS, GMLP_CHUNK, GMLP_CHUNK), BF16)],
        compiler_params=_cparams(("arbitrary",)),
        name="gmlp",
    )(proj, proj, gv, w_s, b_t)


def _mlstm_kernel(q_ref, k_ref, v_ref, o_ref, gp_ref, gb_ref, og_ref, y_ref, c_sc, n_sc, m_sc):
    L, H, DQK, DV = MLSTM_L, MLSTM_HEADS, MLSTM_DQK, MLSTM_DV
    scale = DQK ** -0.5

    @pl.when(pl.program_id(1) == 0)
    def _():
        c_sc[...] = jnp.zeros_like(c_sc)
        n_sc[...] = jnp.zeros_like(n_sc)
        m_sc[...] = jnp.zeros_like(m_sc)

    lane = lax.broadcasted_iota(I32, (L, LANES), 1)
    z = gp_ref[...] + gb_ref[...]
    sc = GATE_SOFTCAP * jnp.tanh(z / GATE_SOFTCAP)
    logf = -(jnp.maximum(-sc, 0.0) + jnp.log1p(jnp.exp(-jnp.abs(sc))))
    row = lax.broadcasted_iota(I32, (L, L), 0)
    col = lax.broadcasted_iota(I32, (L, L), 1)
    causal = col <= row
    tri = jnp.where(causal, 1.0, 0.0).astype(BF16)
    bcum, rest = None, logf
    for _ in range(3):
        part = rest.astype(BF16)
        rest = rest - part.astype(F32)
        term = jnp.dot(tri, part, preferred_element_type=F32)
        bcum = term if bcum is None else bcum + term
    gcol = jnp.where(lane < H, sc, bcum)
    grow = gcol.T

    for h in range(H):
        ig_col, b_col = gcol[:, h:h + 1], gcol[:, H + h:H + h + 1]
        ig_row, b_row = grow[h:h + 1, :], grow[H + h:H + h + 1, :]
        m = m_sc[h, 0:1, 0:1]
        q = q_ref[:, h * DQK:(h + 1) * DQK]
        k = k_ref[:, h * DQK:(h + 1) * DQK]
        v = v_ref[:, h * DV:(h + 1) * DV]

        d_log = jnp.where(causal, b_col - b_row + ig_row, -jnp.inf)
        inter = b_col + m
        m_row = jnp.maximum(inter, jnp.max(d_log, axis=-1, keepdims=True))
        w_intra = jnp.exp(d_log - m_row)
        w_inter = jnp.exp(inter - m_row)
        s = lax.dot_general(q, k, (((1,), (1,)), ((), ())), preferred_element_type=F32)
        s = s * scale * w_intra
        c_old = c_sc[h]
        n_old = n_sc[h]
        num = (w_inter * jnp.dot(q, c_old.astype(BF16), preferred_element_type=F32)
               + jnp.dot(s.astype(BF16), v, preferred_element_type=F32))
        qn = jnp.sum(q.astype(F32) * n_old, axis=-1, keepdims=True)
        den = w_inter * qn + jnp.sum(s, axis=-1, keepdims=True)
        hh = num / jnp.maximum(jnp.abs(den), jnp.exp(-m_row))

        b_last = b_row[:, L - 1:L]
        tail = b_last - b_col + ig_col
        m_new = jnp.maximum(b_last + m, jnp.max(tail, axis=0, keepdims=True))
        w_tail = jnp.exp(tail - m_new)
        decay = jnp.exp(b_last + m - m_new)
        kw = k.astype(F32) * scale * w_tail
        c_sc[h] = decay * c_old + lax.dot_general(
            kw.astype(BF16), v, (((0,), (0,)), ((), ())), preferred_element_type=F32)
        n_sc[h] = decay * n_old + jnp.sum(kw, axis=0, keepdims=True)
        m_sc[h] = jnp.broadcast_to(m_new, m_sc.shape[1:])

        hn = hh * lax.rsqrt(jnp.mean(hh * hh, axis=-1, keepdims=True) + EPS)
        hn = hn * og_ref[:, h * DV:(h + 1) * DV]
        hn = hn * jax.nn.sigmoid(o_ref[:, h * DV:(h + 1) * DV].astype(F32))
        y_ref[:, h * DV:(h + 1) * DV] = hn.astype(BF16)


def _mlstm(proj, gate_pre, gate_b, out_g, batch, seq):
    n = proj.shape[0]
    L = MLSTM_L
    nc = seq // L
    qk_w = MLSTM_HEADS * MLSTM_DQK
    q_blk, k_blk = 2 * D_GMLP // qk_w, 2 * D_GMLP // qk_w + 1
    v_blk, o_blk = (2 * D_GMLP + 2 * qk_w) // D_MLSTM, (2 * D_GMLP + 2 * qk_w) // D_MLSTM + 1
    rmap = lambda b, c: b * nc + c
    return pl.pallas_call(
        _mlstm_kernel,
        out_shape=jax.ShapeDtypeStruct((n, D_MLSTM), BF16),
        grid=(batch, nc),
        in_specs=[pl.BlockSpec((L, qk_w), lambda b, c: (rmap(b, c), q_blk)),
                  pl.BlockSpec((L, qk_w), lambda b, c: (rmap(b, c), k_blk)),
                  pl.BlockSpec((L, D_MLSTM), lambda b, c: (rmap(b, c), v_blk)),
                  pl.BlockSpec((L, D_MLSTM), lambda b, c: (rmap(b, c), o_blk)),
                  pl.BlockSpec((L, LANES), lambda b, c: (rmap(b, c), 0)),
                  pl.BlockSpec((1, LANES), lambda b, c: (0, 0)),
                  pl.BlockSpec((1, D_MLSTM), lambda b, c: (0, 0))],
        out_specs=pl.BlockSpec((L, D_MLSTM), lambda b, c: (rmap(b, c), 0)),
        scratch_shapes=[pltpu.VMEM((MLSTM_HEADS, MLSTM_DQK, MLSTM_DV), F32),
                        pltpu.VMEM((MLSTM_HEADS, 1, MLSTM_DQK), F32),
                        pltpu.VMEM((MLSTM_HEADS, 8, LANES), F32)],
        compiler_params=_cparams(("parallel", "arbitrary")),
        name="mlstm",
    )(proj, proj, proj, proj, gate_pre, gate_b, out_g)


def _out_proj_kernel(ya_ref, yb_ref, wa_ref, wb_ref, x_ref, o_ref):
    acc = jnp.dot(ya_ref[...], wa_ref[...], preferred_element_type=F32)
    acc = acc + jnp.dot(yb_ref[...], wb_ref[...], preferred_element_type=F32)
    o_ref[...] = x_ref[...] + acc


def _out_proj(y_a, y_b, w_out, x2, tm=1024, tn=1024):
    n = x2.shape[0]
    return pl.pallas_call(
        _out_proj_kernel,
        out_shape=jax.ShapeDtypeStruct((n, D_MODEL), F32),
        grid=(n // tm, D_MODEL // tn),
        in_specs=[pl.BlockSpec((tm, D_GMLP), lambda i, j: (i, 0)),
                  pl.BlockSpec((tm, D_MLSTM), lambda i, j: (i, 0)),
                  pl.BlockSpec((D_GMLP, tn), lambda i, j: (0, j)),
                  pl.BlockSpec((D_MLSTM, tn), lambda i, j: (1, j)),
                  pl.BlockSpec((tm, tn), lambda i, j: (i, j))],
        out_specs=pl.BlockSpec((tm, tn), lambda i, j: (i, j)),
        compiler_params=_cparams(("parallel", "arbitrary")),
        name="out_proj",
    )(y_a, y_b, w_out, w_out, x2)


def _router_kernel(x_ref, g_ref, wr_ref, br_ref, xp_ref, meta_ref, wts_ref, cnt_ref, carry_sc, w2_sc,
                   *, tm):
    i = pl.program_id(0)

    @pl.when(i == 0)
    def _():
        carry_sc[...] = jnp.zeros_like(carry_sc)
        w_hi = wr_ref[...].astype(BF16)
        w2_sc[:, :LANES] = w_hi
        w2_sc[:, LANES:] = (wr_ref[...] - w_hi.astype(F32)).astype(BF16)

    xn = _rms(x_ref[...], g_ref[...])
    bits = lax.bitcast_convert_type(xn.astype(BF16).astype(F32), I32)
    xp_ref[...] = bits[:, :HALF] | lax.shift_right_logical(bits[:, HALF:], 16)

    x_hi = xn.astype(BF16)
    x_lo = (xn - x_hi.astype(F32)).astype(BF16)
    both = jnp.dot(x_hi, w2_sc[...], preferred_element_type=F32)
    logits = (both[:, :LANES] + both[:, LANES:]
              + jnp.dot(x_lo, w2_sc[:, :LANES], preferred_element_type=F32)) + br_ref[...]
    lane = lax.broadcasted_iota(I32, (tm, LANES), 1)
    lane_f = lane.astype(F32)
    big = float(LANES)
    is_g = (lane >= N_EXPERTS) & (lane < N_EXPERTS + N_GROUPS)
    gl = jnp.where(is_g, logits, -jnp.inf)
    gmax = jnp.max(gl, axis=-1, keepdims=True)
    g_lane = jnp.min(jnp.where(gl == gmax, lane_f, big), axis=-1, keepdims=True)
    g_sum = jnp.sum(jnp.where(is_g, jnp.exp(gl - gmax), 0.0), axis=-1, keepdims=True)
    g_prob = 1.0 / g_sum
    g_idx = g_lane.astype(I32) - N_EXPERTS
    in_grp = (lane < N_EXPERTS) & (lax.shift_right_logical(lane, 3) == g_idx)
    el = jnp.where(in_grp, logits, -jnp.inf)
    t1 = jnp.max(el, axis=-1, keepdims=True)
    i1 = jnp.min(jnp.where(el == t1, lane_f, big), axis=-1, keepdims=True)
    el2 = jnp.where(lane_f == i1, -jnp.inf, el)
    t2 = jnp.max(el2, axis=-1, keepdims=True)
    i2 = jnp.min(jnp.where(el2 == t2, lane_f, big), axis=-1, keepdims=True)
    e2 = jnp.exp(t2 - t1)
    w1 = g_prob * (1.0 / (1.0 + e2))
    w2 = g_prob * (e2 / (1.0 + e2))

    sel1 = lane_f == i1
    sel2 = lane_f == i2
    onehot = jnp.where(sel1 | sel2, 1.0, 0.0)
    r = lax.broadcasted_iota(I32, (tm, tm), 0)
    c = lax.broadcasted_iota(I32, (tm, tm), 1)
    strict = jnp.where(c < r, 1.0, 0.0).astype(BF16)
    prefix = jnp.dot(strict, onehot.astype(BF16), preferred_element_type=F32) + carry_sc[0:1, :]
    rank1 = jnp.sum(jnp.where(sel1, prefix, 0.0), axis=-1, keepdims=True)
    rank2 = jnp.sum(jnp.where(sel2, prefix, 0.0), axis=-1, keepdims=True)
    total = carry_sc[0:1, :] + jnp.sum(onehot, axis=0, keepdims=True)
    carry_sc[...] = jnp.broadcast_to(total, carry_sc.shape)
    cnt_ref[...] = jnp.broadcast_to(total, cnt_ref.shape).astype(I32)

    meta = jnp.where(lane == 0, i1, jnp.where(lane == 1, i2,
                     jnp.where(lane == 2, rank1, jnp.where(lane == 3, rank2, 0.0))))
    meta_ref[...] = meta.astype(I32)
    wts_ref[...] = jnp.where(lane == 0, w1, jnp.where(lane == 1, w2, 0.0))


def _router(x1, g, wr, br, tm=512):
    n = x1.shape[0]
    return pl.pallas_call(
        functools.partial(_router_kernel, tm=tm),
        out_shape=(jax.ShapeDtypeStruct((n, HALF), I32),
                   jax.ShapeDtypeStruct((n, LANES), I32),
                   jax.ShapeDtypeStruct((n, LANES), F32),
                   jax.ShapeDtypeStruct((8, LANES), I32)),
        grid=(n // tm,),
        in_specs=[pl.BlockSpec((tm, D_MODEL), lambda i: (i, 0)),
                  pl.BlockSpec((1, D_MODEL), lambda i: (0, 0)),
                  pl.BlockSpec((D_MODEL, LANES), lambda i: (0, 0)),
                  pl.BlockSpec((1, LANES), lambda i: (0, 0))],
        out_specs=(pl.BlockSpec((tm, HALF), lambda i: (i, 0)),
                   pl.BlockSpec((tm, LANES), lambda i: (i, 0)),
                   pl.BlockSpec((tm, LANES), lambda i: (i, 0)),
                   pl.BlockSpec((8, LANES), lambda i: (0, 0))),
        scratch_shapes=[pltpu.VMEM((8, LANES), F32),
                        pltpu.VMEM((D_MODEL, 2 * LANES), BF16)],
        compiler_params=_cparams(("arbitrary",)),
        name="router",
    )(x1, g, wr, br)


def _dispatch_kernel(e_ref, rk_ref, row0_ref, cnt_ref, misc_ref, xp_ref, xs_ref, zbuf, sem, zsem, *, tm):
    base = pl.program_id(0) * (tm * TOP_K)

    @pl.when(pl.program_id(0) == 0)
    def _():
        zbuf[...] = jnp.zeros_like(zbuf)

        def pad_fill(e, carry, *, wait):
            cnt = cnt_ref[e]
            start = row0_ref[e] + cnt
            head = (-cnt) & (SUBLANES - 1)
            for j in range(SUBLANES - 1):
                @pl.when(j < head)
                def _(j=j):
                    cp = pltpu.make_async_copy(zbuf.at[pl.ds(0, 1), :], xs_ref.at[pl.ds(start + j, 1), :], zsem)
                    cp.wait() if wait else cp.start()
            off = start + head
            rest = (-(cnt + head)) & (R_SUB - 1)
            bit = R_SUB // 2
            while bit >= SUBLANES:
                @pl.when((rest & bit) != 0)
                def _(off=off, bit=bit):
                    dst = xs_ref.at[pl.ds(pl.multiple_of(off, SUBLANES), bit), :]
                    cp = pltpu.make_async_copy(zbuf.at[pl.ds(0, bit), :], dst, zsem)
                    cp.wait() if wait else cp.start()
                off = off + (rest & bit)
                bit //= 2
            return carry

        def tail_fill(sb, carry, *, wait):
            r = pl.multiple_of(sb * R_SUB, R_SUB)
            cp = pltpu.make_async_copy(zbuf, xs_ref.at[pl.ds(r, R_SUB), :], zsem)
            cp.wait() if wait else cp.start()
            return carry

        for wait in (False, True):
            lax.fori_loop(0, N_EXPERTS, functools.partial(pad_fill, wait=wait), 0)
            lax.fori_loop(misc_ref[0], NSB_MAX, functools.partial(tail_fill, wait=wait), 0)

    def row_copy(r, kk):
        a = base + TOP_K * r + kk
        p = row0_ref[e_ref[a]] + rk_ref[a]
        return pltpu.make_async_copy(xp_ref.at[pl.ds(r, 1), :], xs_ref.at[pl.ds(p, 1), :], sem)

    def issue(r, carry):
        for kk in range(TOP_K):
            row_copy(r, kk).start()
        return carry

    lax.fori_loop(0, tm, issue, 0, unroll=8)
    for _ in range(TOP_K):
        pltpu.make_async_copy(xp_ref, xs_ref.at[pl.ds(0, tm), :], sem).wait()


def _dispatch(e_flat, rk_flat, row0, cnt, misc, xp, tm=512):
    n = xp.shape[0]
    return pl.pallas_call(
        functools.partial(_dispatch_kernel, tm=tm),
        out_shape=jax.ShapeDtypeStruct((NSB_MAX * R_SUB, HALF), I32),
        grid_spec=pltpu.PrefetchScalarGridSpec(
            num_scalar_prefetch=5,
            grid=(n // tm,),
            in_specs=[pl.BlockSpec((tm, HALF), lambda i, *_: (i, 0))],
            out_specs=pl.BlockSpec(memory_space=pl.ANY),
            scratch_shapes=[pltpu.VMEM((R_SUB, HALF), I32),
                            pltpu.SemaphoreType.DMA(()),
                            pltpu.SemaphoreType.DMA(())]),
        compiler_params=_cparams(("arbitrary",)),
        name="dispatch",
    )(e_flat, rk_flat, row0, cnt, misc, xp)


def _expert_kernel(ee_ref, sb0_ref, nsb_ref, misc_ref, xs_hbm, wg_hbm, wu_hbm, wd_hbm, os_hbm,
                   xbuf, hbuf, obuf, zbuf, gbuf, ubuf, dbuf, xsem, osem, zsem, gsem, dsem):
    s = pl.program_id(0)
    n_ent = pl.num_programs(0)
    nsb = nsb_ref[s]
    slot = lax.rem(s, 2)
    nxt = jnp.minimum(s + 1, n_ent - 1)
    has_next = (s + 1 < n_ent) & (nsb_ref[nxt] > 0)

    def sub_row(ent, sb):
        return pl.multiple_of((sb0_ref[ent] + sb) * R_SUB, R_SUB)

    def x_copies(ent, slot_, wait):
        for sb in range(N_SUB):
            @pl.when(sb < nsb_ref[ent])
            def _(sb=sb):
                cp = pltpu.make_async_copy(xs_hbm.at[pl.ds(sub_row(ent, sb), R_SUB), :],
                                           xbuf.at[slot_, sb], xsem.at[slot_])
                cp.wait() if wait else cp.start()

    def gu_slot(ent, c):
        return lax.rem(ent * N_A + c, GU_SLOTS)

    def gu_copies(ent, c, wait):
        e, sl = ee_ref[ent], gu_slot(ent, c)
        for w_hbm, buf in ((wg_hbm, gbuf), (wu_hbm, ubuf)):
            cp = pltpu.make_async_copy(w_hbm.at[e, :, pl.ds(c * GU_CHUNK, GU_CHUNK)], buf.at[sl], gsem.at[sl])
            cp.wait() if wait else cp.start()

    def dn_copy(ent, nb, wait):
        cp = pltpu.make_async_copy(wd_hbm.at[ee_ref[ent], :, pl.ds(nb * DN_CHUNK, DN_CHUNK)],
                                   dbuf.at[nb], dsem.at[nb])
        cp.wait() if wait else cp.start()

    def o_copy(ent, sb, nb, wait):
        cp = pltpu.make_async_copy(
            obuf.at[nb % 2, sb],
            os_hbm.at[pl.ds(sub_row(ent, sb), R_SUB), pl.ds(nb * DN_CHUNK, DN_CHUNK)], osem.at[nb % 2])
        cp.wait() if wait else cp.start()

    def o_waits(ent, nb):
        for sb in range(N_SUB):
            @pl.when(sb < nsb_ref[ent])
            def _(sb=sb):
                o_copy(ent, sb, nb, True)

    @pl.when(s == 0)
    def _():
        x_copies(0, 0, False)
        for c in range(N_A):
            gu_copies(0, c, False)
        for nb in range(N_B):
            dn_copy(0, nb, False)

    @pl.when(nsb == 0)
    def _():
        @pl.when(s >= 1)
        def _():
            for nb in range(N_B - 2, N_B):
                o_waits(s - 1, nb)

    @pl.when(nsb > 0)
    def _():
        x_copies(s, slot, True)

        @pl.when(s + 1 < n_ent)
        def _():
            x_copies(s + 1, 1 - slot, False)

        for c in range(N_A):
            gu_copies(s, c, True)

            @pl.when(has_next)
            def _(c=c):
                gu_copies(nxt, c, False)
            for sb in range(N_SUB):
                @pl.when(sb < nsb)
                def _(sb=sb, c=c):
                    w = xbuf[slot, sb]
                    xa = lax.bitcast_convert_type(w & jnp.int32(-65536), F32).astype(BF16)
                    xb = lax.bitcast_convert_type(lax.shift_left(w, 16), F32).astype(BF16)
                    wg, wu = gbuf.at[gu_slot(s, c)], ubuf.at[gu_slot(s, c)]
                    gate = (jnp.dot(xa, wg[:HALF, :].astype(BF16), preferred_element_type=F32)
                            + jnp.dot(xb, wg[HALF:, :].astype(BF16), preferred_element_type=F32))
                    up = (jnp.dot(xa, wu[:HALF, :].astype(BF16), preferred_element_type=F32)
                          + jnp.dot(xb, wu[HALF:, :].astype(BF16), preferred_element_type=F32))
                    hbuf[c, sb] = (jax.nn.silu(gate) * up).astype(BF16)

        for nb in range(N_B):
            if nb >= 2:
                o_waits(s, nb - 2)
            else:
                @pl.when(s >= 1)
                def _(nb=nb):
                    o_waits(s - 1, nb + N_B - 2)
            dn_copy(s, nb, True)
            for sb in range(N_SUB):
                @pl.when(sb < nsb)
                def _(sb=sb, nb=nb):
                    wd = dbuf.at[nb]
                    acc = jnp.dot(hbuf[0, sb], wd[0:GU_CHUNK, :].astype(BF16), preferred_element_type=F32)
                    for ca in range(1, N_A):
                        acc = acc + jnp.dot(hbuf[ca, sb],
                                            wd[ca * GU_CHUNK:(ca + 1) * GU_CHUNK, :].astype(BF16),
                                            preferred_element_type=F32)
                    obuf[nb % 2, sb] = acc
                    o_copy(s, sb, nb, False)

            @pl.when(has_next)
            def _(nb=nb):
                dn_copy(nxt, nb, False)

    @pl.when(s == n_ent - 1)
    def _():
        @pl.when(nsb > 0)
        def _():
            for nb in range(N_B - 2, N_B):
                o_waits(s, nb)
        zbuf[...] = jnp.zeros_like(zbuf)

        def tail_fill(sb, carry, *, wait):
            r = pl.multiple_of(sb * R_SUB, R_SUB)
            for nb in range(N_B):
                cp = pltpu.make_async_copy(
                    zbuf, os_hbm.at[pl.ds(r, R_SUB), pl.ds(nb * DN_CHUNK, DN_CHUNK)], zsem)
                cp.wait() if wait else cp.start()
            return carry

        for wait in (False, True):
            lax.fori_loop(misc_ref[0], NSB_MAX, functools.partial(tail_fill, wait=wait), 0)


def _experts(ent_e, ent_sb0, ent_nsb, misc, xs, w_gate, w_up, w_down):
    assert N_B % 2 == 0, "output column chunks alternate between two staging sets"
    any_spec = pl.BlockSpec(memory_space=pl.ANY)
    return pl.pallas_call(
        _expert_kernel,
        out_shape=jax.ShapeDtypeStruct((NSB_MAX * R_SUB, D_MODEL), F32),
        grid_spec=pltpu.PrefetchScalarGridSpec(
            num_scalar_prefetch=4,
            grid=(NE_MAX,),
            in_specs=[any_spec, any_spec, any_spec, any_spec],
            out_specs=any_spec,
            scratch_shapes=[pltpu.VMEM((2, N_SUB, R_SUB, HALF), I32),
                            pltpu.VMEM((N_A, N_SUB, R_SUB, GU_CHUNK), BF16),
                            pltpu.VMEM((2, N_SUB, R_SUB, DN_CHUNK), F32),
                            pltpu.VMEM((R_SUB, DN_CHUNK), F32),
                            pltpu.VMEM((GU_SLOTS, D_MODEL, GU_CHUNK), F32),
                            pltpu.VMEM((GU_SLOTS, D_MODEL, GU_CHUNK), F32),
                            pltpu.VMEM((N_B, D_EXPERT, DN_CHUNK), F32),
                            pltpu.SemaphoreType.DMA((2,)),
                            pltpu.SemaphoreType.DMA((2,)),
                            pltpu.SemaphoreType.DMA(()),
                            pltpu.SemaphoreType.DMA((GU_SLOTS,)),
                            pltpu.SemaphoreType.DMA((N_B,))]),
        compiler_params=_cparams(("arbitrary",)),
        name="experts",
    )(ent_e, ent_sb0, ent_nsb, misc, xs, w_gate, w_up, w_down)


def _combine_kernel(e_ref, rk_ref, row0_ref, x_ref, wts_ref, g_ref, os_ref, y_ref, gbuf, sem, *, tm):
    i = pl.program_id(0)
    slot = lax.rem(i, 2)

    def issue_tile(tile, slot_):
        base = tile * (tm * TOP_K)

        def issue(r, carry):
            for kk in range(TOP_K):
                a = base + TOP_K * r + kk
                p = row0_ref[e_ref[a]] + rk_ref[a]
                pltpu.make_async_copy(os_ref.at[pl.ds(p, 1), :], gbuf.at[slot_, kk, pl.ds(r, 1), :],
                                      sem.at[slot_]).start()
            return carry

        lax.fori_loop(0, tm, issue, 0, unroll=8)

    @pl.when(i == 0)
    def _():
        issue_tile(0, 0)

    @pl.when(i + 1 < pl.num_programs(0))
    def _():
        issue_tile(i + 1, 1 - slot)

    for kk in range(TOP_K):
        pltpu.make_async_copy(os_ref.at[pl.ds(0, tm), :], gbuf.at[slot, kk], sem.at[slot]).wait()
    moe = wts_ref[:, 0:1] * gbuf[slot, 0] + wts_ref[:, 1:2] * gbuf[slot, 1]
    y_ref[...] = _rms(x_ref[...] + moe, g_ref[...])


def _combine(e_flat, rk_flat, row0, x1, wts, g, out_sorted, tm=256):
    n = x1.shape[0]
    return pl.pallas_call(
        functools.partial(_combine_kernel, tm=tm),
        out_shape=jax.ShapeDtypeStruct((n, D_MODEL), F32),
        grid_spec=pltpu.PrefetchScalarGridSpec(
            num_scalar_prefetch=3,
            grid=(n // tm,),
            in_specs=[pl.BlockSpec((tm, D_MODEL), lambda i, *_: (i, 0)),
                      pl.BlockSpec((tm, LANES), lambda i, *_: (i, 0)),
                      pl.BlockSpec((1, D_MODEL), lambda i, *_: (0, 0)),
                      pl.BlockSpec(memory_space=pl.ANY)],
            out_specs=pl.BlockSpec((tm, D_MODEL), lambda i, *_: (i, 0)),
            scratch_shapes=[pltpu.VMEM((2, TOP_K, tm, D_MODEL), F32),
                            pltpu.SemaphoreType.DMA((2,))]),
        compiler_params=_cparams(("arbitrary",)),
        name="combine",
    )(e_flat, rk_flat, row0, x1, wts, g, out_sorted)


def _moe(x1, norm_g, wr_g, br_g, wr_e, br_e, w_gate, w_up, w_down, final_g):
    n = x1.shape[0]
    pad = LANES - N_EXPERTS - N_GROUPS
    wr = jnp.concatenate([wr_e, wr_g, jnp.zeros((D_MODEL, pad), F32)], axis=1)
    br = jnp.concatenate([br_e, br_g, jnp.zeros((pad,), F32)])[None]
    xp, meta, wts, counts = _router(x1, norm_g[None], wr, br)

    assert n * TOP_K == N_ASSIGN
    cnt = counts[0, :N_EXPERTS]
    nsb = (cnt + R_SUB - 1) // R_SUB
    sb_end = jnp.cumsum(nsb)
    sb_start = sb_end - nsb
    nent = (nsb + N_SUB - 1) // N_SUB
    ent_end = jnp.cumsum(nent)
    ent_start = ent_end - nent
    ids = jnp.arange(NE_MAX, dtype=I32)
    valid = ids < ent_end[-1]
    ent_e = jnp.minimum(jnp.sum(ids[:, None] >= ent_end[None, :], axis=1), N_EXPERTS - 1)
    ent_e = jnp.where(valid, ent_e, ent_e[ent_end[-1] - 1])
    within = (ids - ent_start[ent_e]) * N_SUB
    ent_sb0 = jnp.where(valid, sb_start[ent_e] + within, 0)
    ent_nsb = jnp.where(valid, jnp.clip(nsb[ent_e] - within, 0, N_SUB), 0)
    row0 = (sb_start * R_SUB).astype(I32)
    misc = sb_end[-1:].astype(I32)
    e_flat = meta[:, 0:TOP_K].reshape(-1)
    rk_flat = meta[:, TOP_K:2 * TOP_K].reshape(-1)

    xs = _dispatch(e_flat, rk_flat, row0, cnt, misc, xp)
    out_sorted = _experts(ent_e.astype(I32), ent_sb0.astype(I32), ent_nsb.astype(I32), misc, xs,
                          w_gate, w_up, w_down)
    return _combine(e_flat, rk_flat, row0, x1, wts, final_g[None], out_sorted)


def kernel(x, norm_mix_g, w_in, gmlp_v_norm_g, gmlp_spatial_w, gmlp_spatial_b, mlstm_igate_b,
           mlstm_fgate_b, mlstm_out_norm_g, w_out, norm_ffn_g, router_group_w, router_group_b,
           router_expert_w, router_expert_b, expert_w_gate, expert_w_up, expert_w_down, final_norm_g):
    batch, seq, d = x.shape
    n = batch * seq
    assert w_in.shape[0] == 1, "the final rmsnorm is fused after the single layer's MoE"
    l = 0
    x2 = x.reshape(n, d)
    w_main = w_in[l]
    w_gcol = jnp.pad(w_in[l][:, D_PROJ_MAIN:], ((0, 0), (0, LANES - N_GATE_COLS))).astype(BF16)
    proj, gate_pre = _in_proj(x2, norm_mix_g[l][None], w_main, w_gcol)
    y_a = _gmlp(proj, gmlp_v_norm_g[l][None], gmlp_spatial_w[l], gmlp_spatial_b[l].T)
    gate_b = jnp.concatenate([mlstm_igate_b[l], mlstm_fgate_b[l],
                              jnp.zeros((LANES - N_GATE_COLS,), F32)])[None]
    y_b = _mlstm(proj, gate_pre, gate_b, mlstm_out_norm_g[l][None], batch, seq)
    x1 = _out_proj(y_a, y_b, w_out[l].astype(BF16), x2)
    out = _moe(x1, norm_ffn_g[l], router_group_w[l], router_group_b[l], router_expert_w[l],
               router_expert_b[l], expert_w_gate[l], expert_w_up[l], expert_w_down[l], final_norm_g)
    return out.reshape(batch, seq, d)
```

```python
import functools

import jax
import jax.numpy as jnp
from jax import lax
from jax.experimental import pallas as pl
from jax.experimental.pallas import tpu as pltpu

F32 = jnp.float32
BF16 = jnp.bfloat16
I32 = jnp.int32

D_MODEL = 4096
D_GMLP = 2048
GMLP_CHUNK = 128
GMLP_GROUPS = 16
D_MLSTM = 2048
MLSTM_HEADS = 4
MLSTM_DV = 512
MLSTM_DQK = 256
GATE_SOFTCAP = 15.0
N_GROUPS = 8
EXPERTS_PER_GROUP = 8
N_EXPERTS = 64
TOP_K = 2
D_EXPERT = 512
EPS = 1e-6
D_PROJ_MAIN = 10240
N_GATE_COLS = 8

LANES = 128
SUBLANES = 8
VMEM_LIMIT = 56 * 1024 * 1024
MLSTM_L = 256
HALF = D_MODEL // 2
N_ASSIGN = 4 * 4096 * TOP_K
R_SUB = 256
N_SUB = 3
NSB_MAX = -(-(N_ASSIGN + N_EXPERTS * (R_SUB - 1)) // R_SUB)
NE_MAX = N_EXPERTS + N_ASSIGN // (R_SUB * N_SUB)
GU_CHUNK = 256
N_A = D_EXPERT // GU_CHUNK
DN_CHUNK = 1024
N_B = D_MODEL // DN_CHUNK
GU_SLOTS = N_A + 1


def _cparams(sem):
    return pltpu.CompilerParams(dimension_semantics=sem, vmem_limit_bytes=VMEM_LIMIT)


def _rms(x, g):
    return x * lax.rsqrt(jnp.mean(x * x, axis=-1, keepdims=True) + EPS) * g


def _in_proj_kernel(x_hbm, g_ref, w_ref, wg_ref, proj_ref, gate_ref, x_buf, xn_sc, sem, *, tm, strip):
    i = pl.program_id(0)

    def x_copy(tile):
        return pltpu.make_async_copy(x_hbm.at[pl.ds(pl.multiple_of(tile * tm, tm), tm), :], x_buf, sem)

    @pl.when(pl.program_id(1) == 0)
    def _():
        @pl.when(i == 0)
        def _():
            x_copy(0).start()

        x_copy(i).wait()

        def norm(r, carry):
            rows = pl.ds(pl.multiple_of(r * strip, strip), strip)
            xn_sc[rows, :] = _rms(x_buf[rows, :], g_ref[...]).astype(BF16)
            return carry

        lax.fori_loop(0, tm // strip, norm, 0)

        @pl.when(i + 1 < pl.num_programs(0))
        def _():
            x_copy(i + 1).start()

        wg = jnp.concatenate([wg_ref[...].astype(BF16),
                              jnp.zeros((LANES - N_GATE_COLS, D_MODEL), BF16)], axis=0)
        gate_ref[...] = lax.dot_general(xn_sc[...], wg, (((1,), (1,)), ((), ())),
                                        preferred_element_type=F32)

    proj_ref[...] = lax.dot_general(xn_sc[...], w_ref[...].astype(BF16), (((1,), (1,)), ((), ())),
                                    preferred_element_type=F32).astype(BF16)


def _in_proj(x2, g, w_t, tm=1024, tn=512, strip=32):
    n = x2.shape[0]
    return pl.pallas_call(
        functools.partial(_in_proj_kernel, tm=tm, strip=strip),
        out_shape=(jax.ShapeDtypeStruct((n, D_PROJ_MAIN), BF16),
                   jax.ShapeDtypeStruct((n, LANES), F32)),
        grid=(n // tm, D_PROJ_MAIN // tn),
        in_specs=[pl.BlockSpec(memory_space=pl.ANY),
                  pl.BlockSpec((1, D_MODEL), lambda i, j: (0, 0)),
                  pl.BlockSpec((tn, D_MODEL), lambda i, j: (j, 0)),
                  pl.BlockSpec((N_GATE_COLS, D_MODEL), lambda i, j: (D_PROJ_MAIN // N_GATE_COLS, 0))],
        out_specs=(pl.BlockSpec((tm, tn), lambda i, j: (i, j)),
                   pl.BlockSpec((tm, LANES), lambda i, j: (i, 0))),
        scratch_shapes=[pltpu.VMEM((tm, D_MODEL), F32),
                        pltpu.VMEM((tm, D_MODEL), BF16),
                        pltpu.SemaphoreType.DMA(())],
        compiler_params=_cparams(("arbitrary", "arbitrary")),
        name="in_proj",
    )(x2, g, w_t, w_t)


def _gmlp_kernel(u_ref, v_ref, gv_ref, w_ref, bt_ref, y_ref, wm_sc, *, rows):
    c = GMLP_CHUNK

    @pl.when(pl.program_id(0) == 0)
    def _():
        r = lax.broadcasted_iota(I32, (c, c), 0)
        s = lax.broadcasted_iota(I32, (c, c), 1)
        for g in range(GMLP_GROUPS):
            wm_sc[g] = jnp.where(s <= r, w_ref[g], 0.0).astype(BF16)

    for ci in range(rows // c):
        rs = slice(ci * c, (ci + 1) * c)
        gu = jax.nn.gelu(u_ref[rs, :].astype(F32))
        gv = jax.nn.gelu(v_ref[rs, :].astype(F32))
        vn = _rms(gv, gv_ref[...]).astype(BF16)
        for g in range(GMLP_GROUPS):
            cs = slice(g * c, (g + 1) * c)
            z = jnp.dot(wm_sc[g], vn[:, cs], preferred_element_type=F32) + bt_ref[:, g:g + 1]
            y_ref[rs, cs] = (gu[:, cs] * z).astype(BF16)


def _gmlp(proj, gv, w_s, b_t, rows=512):
    n = proj.shape[0]
    return pl.pallas_call(
        functools.partial(_gmlp_kernel, rows=rows),
        out_shape=jax.ShapeDtypeStruct((n, D_GMLP), BF16),
        grid=(n // rows,),
        in_specs=[pl.BlockSpec((rows, D_GMLP), lambda i: (i, 0)),
                  pl.BlockSpec((rows, D_GMLP), lambda i: (i, 1)),
                  pl.BlockSpec((1, D_GMLP), lambda i: (0, 0)),
                  pl.BlockSpec((GMLP_GROUPS, GMLP_CHUNK, GMLP_CHUNK), lambda i: (0, 0, 0)),
                  pl.BlockSpec((GMLP_CHUNK, GMLP_GROUPS), lambda i: (0, 0))],
        out_specs=pl.BlockSpec((rows, D_GMLP), lambda i: (i, 0)),
        scratch_shapes=[pltpu.VMEM((GMLP_GROUPS, GMLP_CHUNK, GMLP_CHUNK), BF16)],
        compiler_params=_cparams(("arbitrary",)),
        name="gmlp",
    )(proj, proj, gv, w_s, b_t)


def _mlstm_kernel(q_ref, k_ref, v_ref, o_ref, gp_ref, gb_ref, og_ref, y_ref, c_sc, n_sc, m_sc):
    L, H, DQK, DV = MLSTM_L, MLSTM_HEADS, MLSTM_DQK, MLSTM_DV
    scale = DQK ** -0.5

    @pl.when(pl.program_id(1) == 0)
    def _():
        c_sc[...] = jnp.zeros_like(c_sc)
        n_sc[...] = jnp.zeros_like(n_sc)
        m_sc[...] = jnp.zeros_like(m_sc)

    lane = lax.broadcasted_iota(I32, (L, LANES), 1)
    z = gp_ref[...] + gb_ref[...]
    sc = GATE_SOFTCAP * jnp.tanh(z / GATE_SOFTCAP)
    logf = -(jnp.maximum(-sc, 0.0) + jnp.log1p(jnp.exp(-jnp.abs(sc))))
    row = lax.broadcasted_iota(I32, (L, L), 0)
    col = lax.broadcasted_iota(I32, (L, L), 1)
    causal = col <= row
    tri = jnp.where(causal, 1.0, 0.0).astype(BF16)
    bcum, rest = None, logf
    for _ in range(3):
        part = rest.astype(BF16)
        rest = rest - part.astype(F32)
        term = jnp.dot(tri, part, preferred_element_type=F32)
        bcum = term if bcum is None else bcum + term
    gcol = jnp.where(lane < H, sc, bcum)
    grow = gcol.T

    for h in range(H):
        ig_col, b_col = gcol[:, h:h + 1], gcol[:, H + h:H + h + 1]
        ig_row, b_row = grow[h:h + 1, :], grow[H + h:H + h + 1, :]
        m = m_sc[h, 0:1, 0:1]
        q = q_ref[:, h * DQK:(h + 1) * DQK]
        k = k_ref[:, h * DQK:(h + 1) * DQK]
        v = v_ref[:, h * DV:(h + 1) * DV]

        d_log = jnp.where(causal, b_col - b_row + ig_row, -jnp.inf)
        inter = b_col + m
        m_row = jnp.maximum(inter, jnp.max(d_log, axis=-1, keepdims=True))
        w_intra = jnp.exp(d_log - m_row)
        w_inter = jnp.exp(inter - m_row)
        s = lax.dot_general(q, k, (((1,), (1,)), ((), ())), preferred_element_type=F32)
        s = s * scale * w_intra
        c_old = c_sc[h]
        n_old = n_sc[h]
        num = (w_inter * jnp.dot(q, c_old.astype(BF16), preferred_element_type=F32)
               + jnp.dot(s.astype(BF16), v, preferred_element_type=F32))
        qn = jnp.sum(q.astype(F32) * n_old, axis=-1, keepdims=True)
        den = w_inter * qn + jnp.sum(s, axis=-1, keepdims=True)
        hh = num / jnp.maximum(jnp.abs(den), jnp.exp(-m_row))

        b_last = b_row[:, L - 1:L]
        tail = b_last - b_col + ig_col
        m_new = jnp.maximum(b_last + m, jnp.max(tail, axis=0, keepdims=True))
        w_tail = jnp.exp(tail - m_new)
        decay = jnp.exp(b_last + m - m_new)
        kw = k.astype(F32) * scale * w_tail
        c_sc[h] = decay * c_old + lax.dot_general(
            kw.astype(BF16), v, (((0,), (0,)), ((), ())), preferred_element_type=F32)
        n_sc[h] = decay * n_old + jnp.sum(kw, axis=0, keepdims=True)
        m_sc[h] = jnp.broadcast_to(m_new, m_sc.shape[1:])

        hn = hh * lax.rsqrt(jnp.mean(hh * hh, axis=-1, keepdims=True) + EPS)
        hn = hn * og_ref[:, h * DV:(h + 1) * DV]
        hn = hn * jax.nn.sigmoid(o_ref[:, h * DV:(h + 1) * DV].astype(F32))
        y_ref[:, h * DV:(h + 1) * DV] = hn.astype(BF16)


def _mlstm(proj, gate_pre, gate_b, out_g, batch, seq):
    n = proj.shape[0]
    L = MLSTM_L
    nc = seq // L
    qk_w = MLSTM_HEADS * MLSTM_DQK
    q_blk, k_blk = 2 * D_GMLP // qk_w, 2 * D_GMLP // qk_w + 1
    v_blk, o_blk = (2 * D_GMLP + 2 * qk_w) // D_MLSTM, (2 * D_GMLP + 2 * qk_w) // D_MLSTM + 1
    rmap = lambda b, c: b * nc + c
    return pl.pallas_call(
        _mlstm_kernel,
        out_shape=jax.ShapeDtypeStruct((n, D_MLSTM), BF16),
        grid=(batch, nc),
        in_specs=[pl.BlockSpec((L, qk_w), lambda b, c: (rmap(b, c), q_blk)),
                  pl.BlockSpec((L, qk_w), lambda b, c: (rmap(b, c), k_blk)),
                  pl.BlockSpec((L, D_MLSTM), lambda b, c: (rmap(b, c), v_blk)),
                  pl.BlockSpec((L, D_MLSTM), lambda b, c: (rmap(b, c), o_blk)),
                  pl.BlockSpec((L, LANES), lambda b, c: (rmap(b, c), 0)),
                  pl.BlockSpec((1, LANES), lambda b, c: (0, 0)),
                  pl.BlockSpec((1, D_MLSTM), lambda b, c: (0, 0))],
        out_specs=pl.BlockSpec((L, D_MLSTM), lambda b, c: (rmap(b, c), 0)),
        scratch_shapes=[pltpu.VMEM((MLSTM_HEADS, MLSTM_DQK, MLSTM_DV), F32),
                        pltpu.VMEM((MLSTM_HEADS, 1, MLSTM_DQK), F32),
                        pltpu.VMEM((MLSTM_HEADS, 8, LANES), F32)],
        compiler_params=_cparams(("parallel", "arbitrary")),
        name="mlstm",
    )(proj, proj, proj, proj, gate_pre, gate_b, out_g)


def _out_proj_kernel(ya_ref, yb_ref, wa_ref, wb_ref, x_ref, o_ref):
    acc = jnp.dot(ya_ref[...], wa_ref[...], preferred_element_type=F32)
    acc = acc + jnp.dot(yb_ref[...], wb_ref[...], preferred_element_type=F32)
    o_ref[...] = x_ref[...] + acc


def _out_proj(y_a, y_b, w_out, x2, tm=1024, tn=1024):
    n = x2.shape[0]
    return pl.pallas_call(
        _out_proj_kernel,
        out_shape=jax.ShapeDtypeStruct((n, D_MODEL), F32),
        grid=(n // tm, D_MODEL // tn),
        in_specs=[pl.BlockSpec((tm, D_GMLP), lambda i, j: (i, 0)),
                  pl.BlockSpec((tm, D_MLSTM), lambda i, j: (i, 0)),
                  pl.BlockSpec((D_GMLP, tn), lambda i, j: (0, j)),
                  pl.BlockSpec((D_MLSTM, tn), lambda i, j: (1, j)),
                  pl.BlockSpec((tm, tn), lambda i, j: (i, j))],
        out_specs=pl.BlockSpec((tm, tn), lambda i, j: (i, j)),
        compiler_params=_cparams(("parallel", "arbitrary")),
        name="out_proj",
    )(y_a, y_b, w_out, w_out, x2)


def _router_kernel(x_ref, g_ref, wr_ref, br_ref, xp_ref, meta_ref, wts_ref, cnt_ref, carry_sc, w2_sc,
                   *, tm):
    i = pl.program_id(0)

    @pl.when(i == 0)
    def _():
        carry_sc[...] = jnp.zeros_like(carry_sc)
        w_hi = wr_ref[...].astype(BF16)
        w2_sc[:, :LANES] = w_hi
        w2_sc[:, LANES:] = (wr_ref[...] - w_hi.astype(F32)).astype(BF16)

    xn = _rms(x_ref[...], g_ref[...])
    bits = lax.bitcast_convert_type(xn.astype(BF16).astype(F32), I32)
    xp_ref[...] = bits[:, :HALF] | lax.shift_right_logical(bits[:, HALF:], 16)

    x_hi = xn.astype(BF16)
    x_lo = (xn - x_hi.astype(F32)).astype(BF16)
    both = jnp.dot(x_hi, w2_sc[...], preferred_element_type=F32)
    logits = (both[:, :LANES] + both[:, LANES:]
              + jnp.dot(x_lo, w2_sc[:, :LANES], preferred_element_type=F32)) + br_ref[...]
    lane = lax.broadcasted_iota(I32, (tm, LANES), 1)
    lane_f = lane.astype(F32)
    big = float(LANES)
    is_g = (lane >= N_EXPERTS) & (lane < N_EXPERTS + N_GROUPS)
    gl = jnp.where(is_g, logits, -jnp.inf)
    gmax = jnp.max(gl, axis=-1, keepdims=True)
    g_lane = jnp.min(jnp.where(gl == gmax, lane_f, big), axis=-1, keepdims=True)
    g_sum = jnp.sum(jnp.where(is_g, jnp.exp(gl - gmax), 0.0), axis=-1, keepdims=True)
    g_prob = 1.0 / g_sum
    g_idx = g_lane.astype(I32) - N_EXPERTS
    in_grp = (lane < N_EXPERTS) & (lax.shift_right_logical(lane, 3) == g_idx)
    el = jnp.where(in_grp, logits, -jnp.inf)
    t1 = jnp.max(el, axis=-1, keepdims=True)
    i1 = jnp.min(jnp.where(el == t1, lane_f, big), axis=-1, keepdims=True)
    el2 = jnp.where(lane_f == i1, -jnp.inf, el)
    t2 = jnp.max(el2, axis=-1, keepdims=True)
    i2 = jnp.min(jnp.where(el2 == t2, lane_f, big), axis=-1, keepdims=True)
    e2 = jnp.exp(t2 - t1)
    w1 = g_prob * (1.0 / (1.0 + e2))
    w2 = g_prob * (e2 / (1.0 + e2))

    sel1 = lane_f == i1
    sel2 = lane_f == i2
    onehot = jnp.where(sel1 | sel2, 1.0, 0.0)
    r = lax.broadcasted_iota(I32, (tm, tm), 0)
    c = lax.broadcasted_iota(I32, (tm, tm), 1)
    strict = jnp.where(c < r, 1.0, 0.0).astype(BF16)
    prefix = jnp.dot(strict, onehot.astype(BF16), preferred_element_type=F32) + carry_sc[0:1, :]
    rank1 = jnp.sum(jnp.where(sel1, prefix, 0.0), axis=-1, keepdims=True)
    rank2 = jnp.sum(jnp.where(sel2, prefix, 0.0), axis=-1, keepdims=True)
    total = carry_sc[0:1, :] + jnp.sum(onehot, axis=0, keepdims=True)
    carry_sc[...] = jnp.broadcast_to(total, carry_sc.shape)
    cnt_ref[...] = jnp.broadcast_to(total, cnt_ref.shape).astype(I32)

    meta = jnp.where(lane == 0, i1, jnp.where(lane == 1, i2,
                     jnp.where(lane == 2, rank1, jnp.where(lane == 3, rank2, 0.0))))
    meta_ref[...] = meta.astype(I32)
    wts_ref[...] = jnp.where(lane == 0, w1, jnp.where(lane == 1, w2, 0.0))


def _router(x1, g, wr, br, tm=512):
    n = x1.shape[0]
    return pl.pallas_call(
        functools.partial(_router_kernel, tm=tm),
        out_shape=(jax.ShapeDtypeStruct((n, HALF), I32),
                   jax.ShapeDtypeStruct((n, LANES), I32),
                   jax.ShapeDtypeStruct((n, LANES), F32),
                   jax.ShapeDtypeStruct((8, LANES), I32)),
        grid=(n // tm,),
        in_specs=[pl.BlockSpec((tm, D_MODEL), lambda i: (i, 0)),
                  pl.BlockSpec((1, D_MODEL), lambda i: (0, 0)),
                  pl.BlockSpec((D_MODEL, LANES), lambda i: (0, 0)),
                  pl.BlockSpec((1, LANES), lambda i: (0, 0))],
        out_specs=(pl.BlockSpec((tm, HALF), lambda i: (i, 0)),
                   pl.BlockSpec((tm, LANES), lambda i: (i, 0)),
                   pl.BlockSpec((tm, LANES), lambda i: (i, 0)),
                   pl.BlockSpec((8, LANES), lambda i: (0, 0))),
        scratch_shapes=[pltpu.VMEM((8, LANES), F32),
                        pltpu.VMEM((D_MODEL, 2 * LANES), BF16)],
        compiler_params=_cparams(("arbitrary",)),
        name="router",
    )(x1, g, wr, br)


def _dispatch_kernel(e_ref, rk_ref, row0_ref, cnt_ref, misc_ref, xp_ref, xs_ref, zbuf, sem, zsem, *, tm):
    base = pl.program_id(0) * (tm * TOP_K)

    @pl.when(pl.program_id(0) == 0)
    def _():
        zbuf[...] = jnp.zeros_like(zbuf)

        def pad_fill(e, carry, *, wait):
            cnt = cnt_ref[e]
            start = row0_ref[e] + cnt
            head = (-cnt) & (SUBLANES - 1)
            for j in range(SUBLANES - 1):
                @pl.when(j < head)
                def _(j=j):
                    cp = pltpu.make_async_copy(zbuf.at[pl.ds(0, 1), :], xs_ref.at[pl.ds(start + j, 1), :], zsem)
                    cp.wait() if wait else cp.start()
            off = start + head
            rest = (-(cnt + head)) & (R_SUB - 1)
            bit = R_SUB // 2
            while bit >= SUBLANES:
                @pl.when((rest & bit) != 0)
                def _(off=off, bit=bit):
                    dst = xs_ref.at[pl.ds(pl.multiple_of(off, SUBLANES), bit), :]
                    cp = pltpu.make_async_copy(zbuf.at[pl.ds(0, bit), :], dst, zsem)
                    cp.wait() if wait else cp.start()
                off = off + (rest & bit)
                bit //= 2
            return carry

        def tail_fill(sb, carry, *, wait):
            r = pl.multiple_of(sb * R_SUB, R_SUB)
            cp = pltpu.make_async_copy(zbuf, xs_ref.at[pl.ds(r, R_SUB), :], zsem)
            cp.wait() if wait else cp.start()
            return carry

        for wait in (False, True):
            lax.fori_loop(0, N_EXPERTS, functools.partial(pad_fill, wait=wait), 0)
            lax.fori_loop(misc_ref[0], NSB_MAX, functools.partial(tail_fill, wait=wait), 0)

    def row_copy(r, kk):
        a = base + TOP_K * r + kk
        p = row0_ref[e_ref[a]] + rk_ref[a]
        return pltpu.make_async_copy(xp_ref.at[pl.ds(r, 1), :], xs_ref.at[pl.ds(p, 1), :], sem)

    def issue(r, carry):
        for kk in range(TOP_K):
            row_copy(r, kk).start()
        return carry

    lax.fori_loop(0, tm, issue, 0, unroll=8)
    for _ in range(TOP_K):
        pltpu.make_async_copy(xp_ref, xs_ref.at[pl.ds(0, tm), :], sem).wait()


def _dispatch(e_flat, rk_flat, row0, cnt, misc, xp, tm=512):
    n = xp.shape[0]
    return pl.pallas_call(
        functools.partial(_dispatch_kernel, tm=tm),
        out_shape=jax.ShapeDtypeStruct((NSB_MAX * R_SUB, HALF), I32),
        grid_spec=pltpu.PrefetchScalarGridSpec(
            num_scalar_prefetch=5,
            grid=(n // tm,),
            in_specs=[pl.BlockSpec((tm, HALF), lambda i, *_: (i, 0))],
            out_specs=pl.BlockSpec(memory_space=pl.ANY),
            scratch_shapes=[pltpu.VMEM((R_SUB, HALF), I32),
                            pltpu.SemaphoreType.DMA(()),
                            pltpu.SemaphoreType.DMA(())]),
        compiler_params=_cparams(("arbitrary",)),
        name="dispatch",
    )(e_flat, rk_flat, row0, cnt, misc, xp)


def _expert_kernel(ee_ref, sb0_ref, nsb_ref, misc_ref, xs_hbm, wg_hbm, wu_hbm, wd_hbm, os_hbm,
                   xbuf, hbuf, obuf, zbuf, gbuf, ubuf, dbuf, xsem, osem, zsem, gsem, dsem):
    s = pl.program_id(0)
    n_ent = pl.num_programs(0)
    nsb = nsb_ref[s]
    slot = lax.rem(s, 2)
    nxt = jnp.minimum(s + 1, n_ent - 1)
    has_next = (s + 1 < n_ent) & (nsb_ref[nxt] > 0)

    def sub_row(ent, sb):
        return pl.multiple_of((sb0_ref[ent] + sb) * R_SUB, R_SUB)

    def x_copies(ent, slot_, wait):
        for sb in range(N_SUB):
            @pl.when(sb < nsb_ref[ent])
            def _(sb=sb):
                cp = pltpu.make_async_copy(xs_hbm.at[pl.ds(sub_row(ent, sb), R_SUB), :],
                                           xbuf.at[slot_, sb], xsem.at[slot_])
                cp.wait() if wait else cp.start()

    def gu_slot(ent, c):
        return lax.rem(ent * N_A + c, GU_SLOTS)

    def gu_copies(ent, c, wait):
        e, sl = ee_ref[ent], gu_slot(ent, c)
        for w_hbm, buf in ((wg_hbm, gbuf), (wu_hbm, ubuf)):
            cp = pltpu.make_async_copy(w_hbm.at[e, :, pl.ds(c * GU_CHUNK, GU_CHUNK)], buf.at[sl], gsem.at[sl])
            cp.wait() if wait else cp.start()

    def dn_copy(ent, nb, wait):
        cp = pltpu.make_async_copy(wd_hbm.at[ee_ref[ent], :, pl.ds(nb * DN_CHUNK, DN_CHUNK)],
                                   dbuf.at[nb], dsem.at[nb])
        cp.wait() if wait else cp.start()

    def o_copy(ent, sb, nb, wait):
        cp = pltpu.make_async_copy(
            obuf.at[nb % 2, sb],
            os_hbm.at[pl.ds(sub_row(ent, sb), R_SUB), pl.ds(nb * DN_CHUNK, DN_CHUNK)], osem.at[nb % 2])
        cp.wait() if wait else cp.start()

    def o_waits(ent, nb):
        for sb in range(N_SUB):
            @pl.when(sb < nsb_ref[ent])
            def _(sb=sb):
                o_copy(ent, sb, nb, True)

    @pl.when(s == 0)
    def _():
        x_copies(0, 0, False)
        for c in range(N_A):
            gu_copies(0, c, False)
        for nb in range(N_B):
            dn_copy(0, nb, False)

    @pl.when(nsb == 0)
    def _():
        @pl.when(s >= 1)
        def _():
            for nb in range(N_B - 2, N_B):
                o_waits(s - 1, nb)

    @pl.when(nsb > 0)
    def _():
        x_copies(s, slot, True)

        @pl.when(s + 1 < n_ent)
        def _():
            x_copies(s + 1, 1 - slot, False)

        for c in range(N_A):
            gu_copies(s, c, True)

            @pl.when(has_next)
            def _(c=c):
                gu_copies(nxt, c, False)
            for sb in range(N_SUB):
                @pl.when(sb < nsb)
                def _(sb=sb, c=c):
                    w = xbuf[slot, sb]
                    xa = lax.bitcast_convert_type(w & jnp.int32(-65536), F32).astype(BF16)
                    xb = lax.bitcast_convert_type(lax.shift_left(w, 16), F32).astype(BF16)
                    wg, wu = gbuf.at[gu_slot(s, c)], ubuf.at[gu_slot(s, c)]
                    gate = (jnp.dot(xa, wg[:HALF, :].astype(BF16), preferred_element_type=F32)
                            + jnp.dot(xb, wg[HALF:, :].astype(BF16), preferred_element_type=F32))
                    up = (jnp.dot(xa, wu[:HALF, :].astype(BF16), preferred_element_type=F32)
                          + jnp.dot(xb, wu[HALF:, :].astype(BF16), preferred_element_type=F32))
                    hbuf[c, sb] = (jax.nn.silu(gate) * up).astype(BF16)

        for nb in range(N_B):
            if nb >= 2:
                o_waits(s, nb - 2)
            else:
                @pl.when(s >= 1)
                def _(nb=nb):
                    o_waits(s - 1, nb + N_B - 2)
            dn_copy(s, nb, True)
            for sb in range(N_SUB):
                @pl.when(sb < nsb)
                def _(sb=sb, nb=nb):
                    wd = dbuf.at[nb]
                    acc = jnp.dot(hbuf[0, sb], wd[0:GU_CHUNK, :].astype(BF16), preferred_element_type=F32)
                    for ca in range(1, N_A):
                        acc = acc + jnp.dot(hbuf[ca, sb],
                                            wd[ca * GU_CHUNK:(ca + 1) * GU_CHUNK, :].astype(BF16),
                                            preferred_element_type=F32)
                    obuf[nb % 2, sb] = acc
                    o_copy(s, sb, nb, False)

            @pl.when(has_next)
            def _(nb=nb):
                dn_copy(nxt, nb, False)

    @pl.when(s == n_ent - 1)
    def _():
        @pl.when(nsb > 0)
        def _():
            for nb in range(N_B - 2, N_B):
                o_waits(s, nb)
        zbuf[...] = jnp.zeros_like(zbuf)

        def tail_fill(sb, carry, *, wait):
            r = pl.multiple_of(sb * R_SUB, R_SUB)
            for nb in range(N_B):
                cp = pltpu.make_async_copy(
                    zbuf, os_hbm.at[pl.ds(r, R_SUB), pl.ds(nb * DN_CHUNK, DN_CHUNK)], zsem)
                cp.wait() if wait else cp.start()
            return carry

        for wait in (False, True):
            lax.fori_loop(misc_ref[0], NSB_MAX, functools.partial(tail_fill, wait=wait), 0)


def _experts(ent_e, ent_sb0, ent_nsb, misc, xs, w_gate, w_up, w_down):
    assert N_B % 2 == 0, "output column chunks alternate between two staging sets"
    any_spec = pl.BlockSpec(memory_space=pl.ANY)
    return pl.pallas_call(
        _expert_kernel,
        out_shape=jax.ShapeDtypeStruct((NSB_MAX * R_SUB, D_MODEL), F32),
        grid_spec=pltpu.PrefetchScalarGridSpec(
            num_scalar_prefetch=4,
            grid=(NE_MAX,),
            in_specs=[any_spec, any_spec, any_spec, any_spec],
            out_specs=any_spec,
            scratch_shapes=[pltpu.VMEM((2, N_SUB, R_SUB, HALF), I32),
                            pltpu.VMEM((N_A, N_SUB, R_SUB, GU_CHUNK), BF16),
                            pltpu.VMEM((2, N_SUB, R_SUB, DN_CHUNK), F32),
                            pltpu.VMEM((R_SUB, DN_CHUNK), F32),
                            pltpu.VMEM((GU_SLOTS, D_MODEL, GU_CHUNK), F32),
                            pltpu.VMEM((GU_SLOTS, D_MODEL, GU_CHUNK), F32),
                            pltpu.VMEM((N_B, D_EXPERT, DN_CHUNK), F32),
                            pltpu.SemaphoreType.DMA((2,)),
                            pltpu.SemaphoreType.DMA((2,)),
                            pltpu.SemaphoreType.DMA(()),
                            pltpu.SemaphoreType.DMA((GU_SLOTS,)),
                            pltpu.SemaphoreType.DMA((N_B,))]),
        compiler_params=_cparams(("arbitrary",)),
        name="experts",
    )(ent_e, ent_sb0, ent_nsb, misc, xs, w_gate, w_up, w_down)


def _combine_kernel(e_ref, rk_ref, row0_ref, x_ref, wts_ref, g_ref, os_ref, y_ref, gbuf, sem, *, tm):
    i = pl.program_id(0)
    slot = lax.rem(i, 2)

    def issue_tile(tile, slot_):
        base = tile * (tm * TOP_K)

        def issue(r, carry):
            for kk in range(TOP_K):
                a = base + TOP_K * r + kk
                p = row0_ref[e_ref[a]] + rk_ref[a]
                pltpu.make_async_copy(os_ref.at[pl.ds(p, 1), :], gbuf.at[slot_, kk, pl.ds(r, 1), :],
                                      sem.at[slot_]).start()
            return carry

        lax.fori_loop(0, tm, issue, 0, unroll=8)

    @pl.when(i == 0)
    def _():
        issue_tile(0, 0)

    @pl.when(i + 1 < pl.num_programs(0))
    def _():
        issue_tile(i + 1, 1 - slot)

    for kk in range(TOP_K):
        pltpu.make_async_copy(os_ref.at[pl.ds(0, tm), :], gbuf.at[slot, kk], sem.at[slot]).wait()
    moe = wts_ref[:, 0:1] * gbuf[slot, 0] + wts_ref[:, 1:2] * gbuf[slot, 1]
    y_ref[...] = _rms(x_ref[...] + moe, g_ref[...])


def _combine(e_flat, rk_flat, row0, x1, wts, g, out_sorted, tm=256):
    n = x1.shape[0]
    return pl.pallas_call(
        functools.partial(_combine_kernel, tm=tm),
        out_shape=jax.ShapeDtypeStruct((n, D_MODEL), F32),
        grid_spec=pltpu.PrefetchScalarGridSpec(
            num_scalar_prefetch=3,
            grid=(n // tm,),
            in_specs=[pl.BlockSpec((tm, D_MODEL), lambda i, *_: (i, 0)),
                      pl.BlockSpec((tm, LANES), lambda i, *_: (i, 0)),
                      pl.BlockSpec((1, D_MODEL), lambda i, *_: (0, 0)),
                      pl.BlockSpec(memory_space=pl.ANY)],
            out_specs=pl.BlockSpec((tm, D_MODEL), lambda i, *_: (i, 0)),
            scratch_shapes=[pltpu.VMEM((2, TOP_K, tm, D_MODEL), F32),
                            pltpu.SemaphoreType.DMA((2,))]),
        compiler_params=_cparams(("arbitrary",)),
        name="combine",
    )(e_flat, rk_flat, row0, x1, wts, g, out_sorted)


def _moe(x1, norm_g, wr_g, br_g, wr_e, br_e, w_gate, w_up, w_down, final_g):
    n = x1.shape[0]
    pad = LANES - N_EXPERTS - N_GROUPS
    wr = jnp.concatenate([wr_e, wr_g, jnp.zeros((D_MODEL, pad), F32)], axis=1)
    br = jnp.concatenate([br_e, br_g, jnp.zeros((pad,), F32)])[None]
    xp, meta, wts, counts = _router(x1, norm_g[None], wr, br)

    assert n * TOP_K == N_ASSIGN
    cnt = counts[0, :N_EXPERTS]
    nsb = (cnt + R_SUB - 1) // R_SUB
    sb_end = jnp.cumsum(nsb)
    sb_start = sb_end - nsb
    nent = (nsb + N_SUB - 1) // N_SUB
    ent_end = jnp.cumsum(nent)
    ent_start = ent_end - nent
    ids = jnp.arange(NE_MAX, dtype=I32)
    valid = ids < ent_end[-1]
    ent_e = jnp.minimum(jnp.sum(ids[:, None] >= ent_end[None, :], axis=1), N_EXPERTS - 1)
    ent_e = jnp.where(valid, ent_e, ent_e[ent_end[-1] - 1])
    within = (ids - ent_start[ent_e]) * N_SUB
    ent_sb0 = jnp.where(valid, sb_start[ent_e] + within, 0)
    ent_nsb = jnp.where(valid, jnp.clip(nsb[ent_e] - within, 0, N_SUB), 0)
    row0 = (sb_start * R_SUB).astype(I32)
    misc = sb_end[-1:].astype(I32)
    e_flat = meta[:, 0:TOP_K].reshape(-1)
    rk_flat = meta[:, TOP_K:2 * TOP_K].reshape(-1)

    xs = _dispatch(e_flat, rk_flat, row0, cnt, misc, xp)
    out_sorted = _experts(ent_e.astype(I32), ent_sb0.astype(I32), ent_nsb.astype(I32), misc, xs,
                          w_gate, w_up, w_down)
    return _combine(e_flat, rk_flat, row0, x1, wts, final_g[None], out_sorted)


def kernel(x, norm_mix_g, w_in, gmlp_v_norm_g, gmlp_spatial_w, gmlp_spatial_b, mlstm_igate_b,
           mlstm_fgate_b, mlstm_out_norm_g, w_out, norm_ffn_g, router_group_w, router_group_b,
           router_expert_w, router_expert_b, expert_w_gate, expert_w_up, expert_w_down, final_norm_g):
    batch, seq, d = x.shape
    n = batch * seq
    assert w_in.shape[0] == 1, "the final rmsnorm is fused after the single layer's MoE"
    l = 0
    x2 = x.reshape(n, d)
    proj, gate_pre = _in_proj(x2, norm_mix_g[l][None], w_in[l].T)
    y_a = _gmlp(proj, gmlp_v_norm_g[l][None], gmlp_spatial_w[l], gmlp_spatial_b[l].T)
    gate_b = jnp.concatenate([mlstm_igate_b[l], mlstm_fgate_b[l],
                              jnp.zeros((LANES - N_GATE_COLS,), F32)])[None]
    y_b = _mlstm(proj, gate_pre, gate_b, mlstm_out_norm_g[l][None], batch, seq)
    x1 = _out_proj(y_a, y_b, w_out[l].astype(BF16), x2)
    out = _moe(x1, norm_ffn_g[l], router_group_w[l], router_group_b[l], router_expert_w[l],
               router_expert_b[l], expert_w_gate[l], expert_w_up[l], expert_w_down[l], final_norm_g)
    return out.reshape(batch, seq, d)
```

```python
import functools

import jax
import jax.numpy as jnp
from jax import lax
from jax.experimental import pallas as pl
from jax.experimental.pallas import tpu as pltpu

F32 = jnp.float32
BF16 = jnp.bfloat16
I32 = jnp.int32

D_MODEL = 4096
D_GMLP = 2048
GMLP_CHUNK = 128
GMLP_GROUPS = 16
D_MLSTM = 2048
MLSTM_HEADS = 4
MLSTM_DV = 512
MLSTM_DQK = 256
GATE_SOFTCAP = 15.0
N_GROUPS = 8
EXPERTS_PER_GROUP = 8
N_EXPERTS = 64
TOP_K = 2
D_EXPERT = 512
EPS = 1e-6
D_PROJ_MAIN = 10240
N_GATE_COLS = 8

LANES = 128
SUBLANES = 8
VMEM_LIMIT = 56 * 1024 * 1024
MLSTM_L = 256
N_ASSIGN = 4 * 4096 * TOP_K
R_SUB = 256
N_SUB = 3
NSB_MAX = -(-(N_ASSIGN + N_EXPERTS * (R_SUB - 1)) // R_SUB)
NE_MAX = N_EXPERTS + N_ASSIGN // (R_SUB * N_SUB)
GU_CHUNK = 256
N_A = D_EXPERT // GU_CHUNK
DN_CHUNK = 1024
N_B = D_MODEL // DN_CHUNK
GU_SLOTS = N_A + 1


def _cparams(sem):
    return pltpu.CompilerParams(dimension_semantics=sem, vmem_limit_bytes=VMEM_LIMIT)


def _rms(x, g):
    return x * lax.rsqrt(jnp.mean(x * x, axis=-1, keepdims=True) + EPS) * g


def _in_proj_kernel(x_hbm, g_ref, w_ref, wg_ref, proj_ref, gate_ref, x_buf, xn_sc, sem, *, tm, strip):
    i = pl.program_id(0)
    nt = (((1,), (1,)), ((), ()))

    def x_copy(tile):
        return pltpu.make_async_copy(x_hbm.at[pl.ds(pl.multiple_of(tile * tm, tm), tm), :], x_buf, sem)

    @pl.when(pl.program_id(1) == 0)
    def _():
        @pl.when(i == 0)
        def _():
            x_copy(0).start()

        x_copy(i).wait()
        for r in range(0, tm, strip):
            rows = slice(r, r + strip)
            xn_sc[rows, :] = _rms(x_buf[rows, :], g_ref[...]).astype(BF16)

        @pl.when(i + 1 < pl.num_programs(0))
        def _():
            x_copy(i + 1).start()

        wg = jnp.concatenate([wg_ref[...].astype(BF16),
                              jnp.zeros((LANES - N_GATE_COLS, D_MODEL), BF16)], axis=0)
        gate_ref[...] = lax.dot_general(xn_sc[...], wg, nt, preferred_element_type=F32)

    proj_ref[...] = lax.dot_general(xn_sc[...], w_ref[...].astype(BF16), nt,
                                    preferred_element_type=F32).astype(BF16)


def _in_proj(x2, g, w_t, tm=1024, tn=512, strip=32):
    n = x2.shape[0]
    return pl.pallas_call(
        functools.partial(_in_proj_kernel, tm=tm, strip=strip),
        out_shape=(jax.ShapeDtypeStruct((n, D_PROJ_MAIN), BF16),
                   jax.ShapeDtypeStruct((n, LANES), F32)),
        grid=(n // tm, D_PROJ_MAIN // tn),
        in_specs=[pl.BlockSpec(memory_space=pl.ANY),
                  pl.BlockSpec((1, D_MODEL), lambda i, j: (0, 0)),
                  pl.BlockSpec((tn, D_MODEL), lambda i, j: (j, 0)),
                  pl.BlockSpec((N_GATE_COLS, D_MODEL), lambda i, j: (D_PROJ_MAIN // N_GATE_COLS, 0))],
        out_specs=(pl.BlockSpec((tm, tn), lambda i, j: (i, j)),
                   pl.BlockSpec((tm, LANES), lambda i, j: (i, 0))),
        scratch_shapes=[pltpu.VMEM((tm, D_MODEL), F32),
                        pltpu.VMEM((tm, D_MODEL), BF16),
                        pltpu.SemaphoreType.DMA(())],
        compiler_params=_cparams(("arbitrary", "arbitrary")),
        name="in_proj",
    )(x2, g, w_t, w_t)


def _gmlp_kernel(u_ref, v_ref, gv_ref, w_ref, bt_ref, y_ref, wm_sc, *, rows):
    c = GMLP_CHUNK

    @pl.when(pl.program_id(0) == 0)
    def _():
        r = lax.broadcasted_iota(I32, (c, c), 0)
        s = lax.broadcasted_iota(I32, (c, c), 1)
        for g in range(GMLP_GROUPS):
            wm_sc[g] = jnp.where(s <= r, w_ref[g], 0.0).astype(BF16)

    for ci in range(rows // c):
        rs = slice(ci * c, (ci + 1) * c)
        gu = jax.nn.gelu(u_ref[rs, :].astype(F32))
        gv = jax.nn.gelu(v_ref[rs, :].astype(F32))
        vn = _rms(gv, gv_ref[...]).astype(BF16)
        for g in range(GMLP_GROUPS):
            cs = slice(g * c, (g + 1) * c)
            z = jnp.dot(wm_sc[g], vn[:, cs], preferred_element_type=F32) + bt_ref[:, g:g + 1]
            y_ref[rs, cs] = (gu[:, cs] * z).astype(BF16)


def _gmlp(proj, gv, w_s, b_t, rows=512):
    n = proj.shape[0]
    return pl.pallas_call(
        functools.partial(_gmlp_kernel, rows=rows),
        out_shape=jax.ShapeDtypeStruct((n, D_GMLP), BF16),
        grid=(n // rows,),
        in_specs=[pl.BlockSpec((rows, D_GMLP), lambda i: (i, 0)),
                  pl.BlockSpec((rows, D_GMLP), lambda i: (i, 1)),
                  pl.BlockSpec((1, D_GMLP), lambda i: (0, 0)),
                  pl.BlockSpec((GMLP_GROUPS, GMLP_CHUNK, GMLP_CHUNK), lambda i: (0, 0, 0)),
                  pl.BlockSpec((GMLP_CHUNK, GMLP_GROUPS), lambda i: (0, 0))],
        out_specs=pl.BlockSpec((rows, D_GMLP), lambda i: (i, 0)),
        scratch_shapes=[pltpu.VMEM((GMLP_GROUPS, GMLP_CHUNK, GMLP_CHUNK), BF16)],
        compiler_params=_cparams(("arbitrary",)),
        name="gmlp",
    )(proj, proj, gv, w_s, b_t)


def _mlstm_kernel(q_ref, k_ref, v_ref, o_ref, gp_ref, gb_ref, og_ref, y_ref, c_sc, n_sc, m_sc):
    L, H, DQK, DV = MLSTM_L, MLSTM_HEADS, MLSTM_DQK, MLSTM_DV
    scale = DQK ** -0.5

    @pl.when(pl.program_id(1) == 0)
    def _():
        c_sc[...] = jnp.zeros_like(c_sc)
        n_sc[...] = jnp.zeros_like(n_sc)
        m_sc[...] = jnp.zeros_like(m_sc)

    lane = lax.broadcasted_iota(I32, (L, LANES), 1)
    z = gp_ref[...] + gb_ref[...]
    sc = GATE_SOFTCAP * jnp.tanh(z / GATE_SOFTCAP)
    logf = -(jnp.maximum(-sc, 0.0) + jnp.log1p(jnp.exp(-jnp.abs(sc))))
    row = lax.broadcasted_iota(I32, (L, L), 0)
    col = lax.broadcasted_iota(I32, (L, L), 1)
    causal = col <= row
    tri = jnp.where(causal, 1.0, 0.0).astype(BF16)
    bcum, rest = None, logf
    for _ in range(3):
        part = rest.astype(BF16)
        rest = rest - part.astype(F32)
        term = jnp.dot(tri, part, preferred_element_type=F32)
        bcum = term if bcum is None else bcum + term
    gcol = jnp.where(lane < H, sc, bcum)
    grow = gcol.T

    for h in range(H):
        ig_col, b_col = gcol[:, h:h + 1], gcol[:, H + h:H + h + 1]
        ig_row, b_row = grow[h:h + 1, :], grow[H + h:H + h + 1, :]
        m = m_sc[h, 0:1, 0:1]
        q = q_ref[:, h * DQK:(h + 1) * DQK]
        k = k_ref[:, h * DQK:(h + 1) * DQK]
        v = v_ref[:, h * DV:(h + 1) * DV]

        d_log = jnp.where(causal, b_col - b_row + ig_row, -jnp.inf)
        inter = b_col + m
        m_row = jnp.maximum(inter, jnp.max(d_log, axis=-1, keepdims=True))
        w_intra = jnp.exp(d_log - m_row)
        w_inter = jnp.exp(inter - m_row)
        s = lax.dot_general(q, k, (((1,), (1,)), ((), ())), preferred_element_type=F32)
        s = s * scale * w_intra
        c_old = c_sc[h]
        n_old = n_sc[h]
        num = (w_inter * jnp.dot(q, c_old.astype(BF16), preferred_element_type=F32)
               + jnp.dot(s.astype(BF16), v, preferred_element_type=F32))
        qn = jnp.sum(q.astype(F32) * n_old, axis=-1, keepdims=True)
        den = w_inter * qn + jnp.sum(s, axis=-1, keepdims=True)
        hh = num / jnp.maximum(jnp.abs(den), jnp.exp(-m_row))

        b_last = b_row[:, L - 1:L]
        tail = b_last - b_col + ig_col
        m_new = jnp.maximum(b_last + m, jnp.max(tail, axis=0, keepdims=True))
        w_tail = jnp.exp(tail - m_new)
        decay = jnp.exp(b_last + m - m_new)
        kw = k.astype(F32) * scale * w_tail
        c_sc[h] = decay * c_old + lax.dot_general(
            kw.astype(BF16), v, (((0,), (0,)), ((), ())), preferred_element_type=F32)
        n_sc[h] = decay * n_old + jnp.sum(kw, axis=0, keepdims=True)
        m_sc[h] = jnp.broadcast_to(m_new, m_sc.shape[1:])

        hn = hh * lax.rsqrt(jnp.mean(hh * hh, axis=-1, keepdims=True) + EPS)
        hn = hn * og_ref[:, h * DV:(h + 1) * DV]
        hn = hn * jax.nn.sigmoid(o_ref[:, h * DV:(h + 1) * DV].astype(F32))
        y_ref[:, h * DV:(h + 1) * DV] = hn.astype(BF16)


def _mlstm(proj, gate_pre, gate_b, out_g, batch, seq):
    n = proj.shape[0]
    L = MLSTM_L
    nc = seq // L
    qk_w = MLSTM_HEADS * MLSTM_DQK
    q_blk, k_blk = 2 * D_GMLP // qk_w, 2 * D_GMLP // qk_w + 1
    v_blk, o_blk = (2 * D_GMLP + 2 * qk_w) // D_MLSTM, (2 * D_GMLP + 2 * qk_w) // D_MLSTM + 1
    rmap = lambda b, c: b * nc + c
    return pl.pallas_call(
        _mlstm_kernel,
        out_shape=jax.ShapeDtypeStruct((n, D_MLSTM), BF16),
        grid=(batch, nc),
        in_specs=[pl.BlockSpec((L, qk_w), lambda b, c: (rmap(b, c), q_blk)),
                  pl.BlockSpec((L, qk_w), lambda b, c: (rmap(b, c), k_blk)),
                  pl.BlockSpec((L, D_MLSTM), lambda b, c: (rmap(b, c), v_blk)),
                  pl.BlockSpec((L, D_MLSTM), lambda b, c: (rmap(b, c), o_blk)),
                  pl.BlockSpec((L, LANES), lambda b, c: (rmap(b, c), 0)),
                  pl.BlockSpec((1, LANES), lambda b, c: (0, 0)),
                  pl.BlockSpec((1, D_MLSTM), lambda b, c: (0, 0))],
        out_specs=pl.BlockSpec((L, D_MLSTM), lambda b, c: (rmap(b, c), 0)),
        scratch_shapes=[pltpu.VMEM((MLSTM_HEADS, MLSTM_DQK, MLSTM_DV), F32),
                        pltpu.VMEM((MLSTM_HEADS, 1, MLSTM_DQK), F32),
                        pltpu.VMEM((MLSTM_HEADS, 8, LANES), F32)],
        compiler_params=_cparams(("parallel", "arbitrary")),
        name="mlstm",
    )(proj, proj, proj, proj, gate_pre, gate_b, out_g)


def _out_proj_kernel(ya_ref, yb_ref, wa_ref, wb_ref, x_ref, o_ref):
    acc = jnp.dot(ya_ref[...], wa_ref[...], preferred_element_type=F32)
    acc = acc + jnp.dot(yb_ref[...], wb_ref[...], preferred_element_type=F32)
    o_ref[...] = x_ref[...] + acc


def _out_proj(y_a, y_b, w_out, x2, tm=1024, tn=1024):
    n = x2.shape[0]
    return pl.pallas_call(
        _out_proj_kernel,
        out_shape=jax.ShapeDtypeStruct((n, D_MODEL), F32),
        grid=(n // tm, D_MODEL // tn),
        in_specs=[pl.BlockSpec((tm, D_GMLP), lambda i, j: (i, 0)),
                  pl.BlockSpec((tm, D_MLSTM), lambda i, j: (i, 0)),
                  pl.BlockSpec((D_GMLP, tn), lambda i, j: (0, j)),
                  pl.BlockSpec((D_MLSTM, tn), lambda i, j: (1, j)),
                  pl.BlockSpec((tm, tn), lambda i, j: (i, j))],
        out_specs=pl.BlockSpec((tm, tn), lambda i, j: (i, j)),
        compiler_params=_cparams(("parallel", "arbitrary")),
        name="out_proj",
    )(y_a, y_b, w_out, w_out, x2)


def _router_kernel(x_ref, g_ref, wr_ref, br_ref, xp_ref, meta_ref, wts_ref, cnt_ref, carry_sc, w2_sc,
                   *, tm):
    i = pl.program_id(0)

    @pl.when(i == 0)
    def _():
        carry_sc[...] = jnp.zeros_like(carry_sc)
        w_hi = wr_ref[...].astype(BF16)
        w2_sc[:, :LANES] = w_hi
        w2_sc[:, LANES:] = (wr_ref[...] - w_hi.astype(F32)).astype(BF16)

    xn = _rms(x_ref[...], g_ref[...])
    xp_ref[...] = xn

    x_hi = xn.astype(BF16)
    x_lo = (xn - x_hi.astype(F32)).astype(BF16)
    both = jnp.dot(x_hi, w2_sc[...], preferred_element_type=F32)
    logits = (both[:, :LANES] + both[:, LANES:]
              + jnp.dot(x_lo, w2_sc[:, :LANES], preferred_element_type=F32)) + br_ref[...]
    lane = lax.broadcasted_iota(I32, (tm, LANES), 1)
    lane_f = lane.astype(F32)
    big = float(LANES)
    is_g = (lane >= N_EXPERTS) & (lane < N_EXPERTS + N_GROUPS)
    gl = jnp.where(is_g, logits, -jnp.inf)
    gmax = jnp.max(gl, axis=-1, keepdims=True)
    g_lane = jnp.min(jnp.where(gl == gmax, lane_f, big), axis=-1, keepdims=True)
    g_sum = jnp.sum(jnp.where(is_g, jnp.exp(gl - gmax), 0.0), axis=-1, keepdims=True)
    g_prob = 1.0 / g_sum
    g_idx = g_lane.astype(I32) - N_EXPERTS
    in_grp = (lane < N_EXPERTS) & (lax.shift_right_logical(lane, 3) == g_idx)
    el = jnp.where(in_grp, logits, -jnp.inf)
    t1 = jnp.max(el, axis=-1, keepdims=True)
    i1 = jnp.min(jnp.where(el == t1, lane_f, big), axis=-1, keepdims=True)
    el2 = jnp.where(lane_f == i1, -jnp.inf, el)
    t2 = jnp.max(el2, axis=-1, keepdims=True)
    i2 = jnp.min(jnp.where(el2 == t2, lane_f, big), axis=-1, keepdims=True)
    e2 = jnp.exp(t2 - t1)
    w1 = g_prob * (1.0 / (1.0 + e2))
    w2 = g_prob * (e2 / (1.0 + e2))

    sel1 = lane_f == i1
    sel2 = lane_f == i2
    onehot = jnp.where(sel1 | sel2, 1.0, 0.0)
    r = lax.broadcasted_iota(I32, (tm, tm), 0)
    c = lax.broadcasted_iota(I32, (tm, tm), 1)
    strict = jnp.where(c < r, 1.0, 0.0).astype(BF16)
    prefix = jnp.dot(strict, onehot.astype(BF16), preferred_element_type=F32) + carry_sc[0:1, :]
    rank1 = jnp.sum(jnp.where(sel1, prefix, 0.0), axis=-1, keepdims=True)
    rank2 = jnp.sum(jnp.where(sel2, prefix, 0.0), axis=-1, keepdims=True)
    total = carry_sc[0:1, :] + jnp.sum(onehot, axis=0, keepdims=True)
    carry_sc[...] = jnp.broadcast_to(total, carry_sc.shape)
    cnt_ref[...] = jnp.broadcast_to(total, cnt_ref.shape).astype(I32)

    meta = jnp.where(lane == 0, i1, jnp.where(lane == 1, i2,
                     jnp.where(lane == 2, rank1, jnp.where(lane == 3, rank2, 0.0))))
    meta_ref[...] = meta.astype(I32)
    wts_ref[...] = jnp.where(lane == 0, w1, jnp.where(lane == 1, w2, 0.0))


def _router(x1, g, wr, br, tm=512):
    n = x1.shape[0]
    return pl.pallas_call(
        functools.partial(_router_kernel, tm=tm),
        out_shape=(jax.ShapeDtypeStruct((n, D_MODEL), F32),
                   jax.ShapeDtypeStruct((n, LANES), I32),
                   jax.ShapeDtypeStruct((n, LANES), F32),
                   jax.ShapeDtypeStruct((8, LANES), I32)),
        grid=(n // tm,),
        in_specs=[pl.BlockSpec((tm, D_MODEL), lambda i: (i, 0)),
                  pl.BlockSpec((1, D_MODEL), lambda i: (0, 0)),
                  pl.BlockSpec((D_MODEL, LANES), lambda i: (0, 0)),
                  pl.BlockSpec((1, LANES), lambda i: (0, 0))],
        out_specs=(pl.BlockSpec((tm, D_MODEL), lambda i: (i, 0)),
                   pl.BlockSpec((tm, LANES), lambda i: (i, 0)),
                   pl.BlockSpec((tm, LANES), lambda i: (i, 0)),
                   pl.BlockSpec((8, LANES), lambda i: (0, 0))),
        scratch_shapes=[pltpu.VMEM((8, LANES), F32),
                        pltpu.VMEM((D_MODEL, 2 * LANES), BF16)],
        compiler_params=_cparams(("arbitrary",)),
        name="router",
    )(x1, g, wr, br)


def _dispatch_kernel(e_ref, rk_ref, row0_ref, cnt_ref, misc_ref, xp_ref, xs_ref, zbuf, sem, zsem, *, tm):
    base = pl.program_id(0) * (tm * TOP_K)

    @pl.when(pl.program_id(0) == 0)
    def _():
        zbuf[...] = jnp.zeros_like(zbuf)

        def pad_fill(e, carry, *, wait):
            cnt = cnt_ref[e]
            start = row0_ref[e] + cnt
            head = (-cnt) & (SUBLANES - 1)
            for j in range(SUBLANES - 1):
                @pl.when(j < head)
                def _(j=j):
                    cp = pltpu.make_async_copy(zbuf.at[pl.ds(0, 1), :], xs_ref.at[pl.ds(start + j, 1), :], zsem)
                    cp.wait() if wait else cp.start()
            off = start + head
            rest = (-(cnt + head)) & (R_SUB - 1)
            bit = R_SUB // 2
            while bit >= SUBLANES:
                @pl.when((rest & bit) != 0)
                def _(off=off, bit=bit):
                    dst = xs_ref.at[pl.ds(pl.multiple_of(off, SUBLANES), bit), :]
                    cp = pltpu.make_async_copy(zbuf.at[pl.ds(0, bit), :], dst, zsem)
                    cp.wait() if wait else cp.start()
                off = off + (rest & bit)
                bit //= 2
            return carry

        def tail_fill(sb, carry, *, wait):
            r = pl.multiple_of(sb * R_SUB, R_SUB)
            cp = pltpu.make_async_copy(zbuf, xs_ref.at[pl.ds(r, R_SUB), :], zsem)
            cp.wait() if wait else cp.start()
            return carry

        for wait in (False, True):
            lax.fori_loop(0, N_EXPERTS, functools.partial(pad_fill, wait=wait), 0)
            lax.fori_loop(misc_ref[0], NSB_MAX, functools.partial(tail_fill, wait=wait), 0)

    def row_copy(r, kk):
        a = base + TOP_K * r + kk
        p = row0_ref[e_ref[a]] + rk_ref[a]
        return pltpu.make_async_copy(xp_ref.at[pl.ds(r, 1), :], xs_ref.at[pl.ds(p, 1), :], sem)

    def issue(r, carry):
        for kk in range(TOP_K):
            row_copy(r, kk).start()
        return carry

    lax.fori_loop(0, tm, issue, 0, unroll=8)
    for _ in range(TOP_K):
        pltpu.make_async_copy(xp_ref, xs_ref.at[pl.ds(0, tm), :], sem).wait()


def _dispatch(e_flat, rk_flat, row0, cnt, misc, xp, tm=512):
    n = xp.shape[0]
    return pl.pallas_call(
        functools.partial(_dispatch_kernel, tm=tm),
        out_shape=jax.ShapeDtypeStruct((NSB_MAX * R_SUB, D_MODEL), F32),
        grid_spec=pltpu.PrefetchScalarGridSpec(
            num_scalar_prefetch=5,
            grid=(n // tm,),
            in_specs=[pl.BlockSpec((tm, D_MODEL), lambda i, *_: (i, 0))],
            out_specs=pl.BlockSpec(memory_space=pl.ANY),
            scratch_shapes=[pltpu.VMEM((R_SUB, D_MODEL), F32),
                            pltpu.SemaphoreType.DMA(()),
                            pltpu.SemaphoreType.DMA(())]),
        compiler_params=_cparams(("arbitrary",)),
        name="dispatch",
    )(e_flat, rk_flat, row0, cnt, misc, xp)


def _expert_kernel(ee_ref, sb0_ref, nsb_ref, misc_ref, xs_hbm, wg_hbm, wu_hbm, wd_hbm, os_hbm,
                   xbuf, hbuf, obuf, zbuf, gbuf, ubuf, dbuf, xsem, osem, zsem, gsem, dsem):
    s = pl.program_id(0)
    n_ent = pl.num_programs(0)
    nsb = nsb_ref[s]
    nxt = jnp.minimum(s + 1, n_ent - 1)
    has_next = (s + 1 < n_ent) & (nsb_ref[nxt] > 0)

    def sub_row(ent, sb):
        return pl.multiple_of((sb0_ref[ent] + sb) * R_SUB, R_SUB)

    def x_copy(ent, sb, wait):
        cp = pltpu.make_async_copy(xs_hbm.at[pl.ds(sub_row(ent, sb), R_SUB), :], xbuf.at[sb], xsem.at[sb])
        cp.wait() if wait else cp.start()

    def x_copies(ent, wait):
        for sb in range(N_SUB):
            @pl.when(sb < nsb_ref[ent])
            def _(sb=sb):
                x_copy(ent, sb, wait)

    def gu_slot(ent, c):
        return lax.rem(ent * N_A + c, GU_SLOTS)

    def gu_copies(ent, c, wait):
        e, sl = ee_ref[ent], gu_slot(ent, c)
        for w_hbm, buf in ((wg_hbm, gbuf), (wu_hbm, ubuf)):
            cp = pltpu.make_async_copy(w_hbm.at[e, :, pl.ds(c * GU_CHUNK, GU_CHUNK)], buf.at[sl], gsem.at[sl])
            cp.wait() if wait else cp.start()

    def dn_copy(ent, nb, wait):
        cp = pltpu.make_async_copy(wd_hbm.at[ee_ref[ent], :, pl.ds(nb * DN_CHUNK, DN_CHUNK)],
                                   dbuf.at[nb], dsem.at[nb])
        cp.wait() if wait else cp.start()

    def o_copy(ent, sb, nb, wait):
        cp = pltpu.make_async_copy(
            obuf.at[nb % 2, sb],
            os_hbm.at[pl.ds(sub_row(ent, sb), R_SUB), pl.ds(nb * DN_CHUNK, DN_CHUNK)], osem.at[nb % 2])
        cp.wait() if wait else cp.start()

    def o_waits(ent, nb):
        for sb in range(N_SUB):
            @pl.when(sb < nsb_ref[ent])
            def _(sb=sb):
                o_copy(ent, sb, nb, True)

    @pl.when(s == 0)
    def _():
        x_copies(0, False)
        for c in range(N_A):
            gu_copies(0, c, False)
        for nb in range(N_B):
            dn_copy(0, nb, False)

    @pl.when(nsb == 0)
    def _():
        @pl.when(s >= 1)
        def _():
            for nb in range(N_B - 2, N_B):
                o_waits(s - 1, nb)

    @pl.when(nsb > 0)
    def _():
        x_copies(s, True)

        for c in range(N_A):
            gu_copies(s, c, True)

            @pl.when(has_next)
            def _(c=c):
                gu_copies(nxt, c, False)
            for sb in range(N_SUB):
                @pl.when(sb < nsb)
                def _(sb=sb, c=c):
                    xb = xbuf[sb].astype(BF16)
                    sl = gu_slot(s, c)
                    gate = jnp.dot(xb, gbuf[sl].astype(BF16), preferred_element_type=F32)
                    up = jnp.dot(xb, ubuf[sl].astype(BF16), preferred_element_type=F32)
                    hbuf[c, sb] = (jax.nn.silu(gate) * up).astype(BF16)

                if c == N_A - 1:
                    @pl.when(has_next & (sb < nsb_ref[nxt]))
                    def _(sb=sb):
                        x_copy(nxt, sb, False)

        for nb in range(N_B):
            if nb >= 2:
                o_waits(s, nb - 2)
            else:
                @pl.when(s >= 1)
                def _(nb=nb):
                    o_waits(s - 1, nb + N_B - 2)
            dn_copy(s, nb, True)
            for sb in range(N_SUB):
                @pl.when(sb < nsb)
                def _(sb=sb, nb=nb):
                    wd = dbuf.at[nb]
                    acc = jnp.dot(hbuf[0, sb], wd[0:GU_CHUNK, :].astype(BF16), preferred_element_type=F32)
                    for ca in range(1, N_A):
                        acc = acc + jnp.dot(hbuf[ca, sb],
                                            wd[ca * GU_CHUNK:(ca + 1) * GU_CHUNK, :].astype(BF16),
                                            preferred_element_type=F32)
                    obuf[nb % 2, sb] = acc
                    o_copy(s, sb, nb, False)

            @pl.when(has_next)
            def _(nb=nb):
                dn_copy(nxt, nb, False)

    @pl.when(s == n_ent - 1)
    def _():
        @pl.when(nsb > 0)
        def _():
            for nb in range(N_B - 2, N_B):
                o_waits(s, nb)
        zbuf[...] = jnp.zeros_like(zbuf)

        def tail_fill(sb, carry, *, wait):
            r = pl.multiple_of(sb * R_SUB, R_SUB)
            for nb in range(N_B):
                cp = pltpu.make_async_copy(
                    zbuf, os_hbm.at[pl.ds(r, R_SUB), pl.ds(nb * DN_CHUNK, DN_CHUNK)], zsem)
                cp.wait() if wait else cp.start()
            return carry

        for wait in (False, True):
            lax.fori_loop(misc_ref[0], NSB_MAX, functools.partial(tail_fill, wait=wait), 0)


def _experts(ent_e, ent_sb0, ent_nsb, misc, xs, w_gate, w_up, w_down):
    assert N_B % 2 == 0, "output column chunks alternate between two staging sets"
    any_spec = pl.BlockSpec(memory_space=pl.ANY)
    return pl.pallas_call(
        _expert_kernel,
        out_shape=jax.ShapeDtypeStruct((NSB_MAX * R_SUB, D_MODEL), F32),
        grid_spec=pltpu.PrefetchScalarGridSpec(
            num_scalar_prefetch=4,
            grid=(NE_MAX,),
            in_specs=[any_spec, any_spec, any_spec, any_spec],
            out_specs=any_spec,
            scratch_shapes=[pltpu.VMEM((N_SUB, R_SUB, D_MODEL), F32),
                            pltpu.VMEM((N_A, N_SUB, R_SUB, GU_CHUNK), BF16),
                            pltpu.VMEM((2, N_SUB, R_SUB, DN_CHUNK), F32),
                            pltpu.VMEM((R_SUB, DN_CHUNK), F32),
                            pltpu.VMEM((GU_SLOTS, D_MODEL, GU_CHUNK), F32),
                            pltpu.VMEM((GU_SLOTS, D_MODEL, GU_CHUNK), F32),
                            pltpu.VMEM((N_B, D_EXPERT, DN_CHUNK), F32),
                            pltpu.SemaphoreType.DMA((N_SUB,)),
                            pltpu.SemaphoreType.DMA((2,)),
                            pltpu.SemaphoreType.DMA(()),
                            pltpu.SemaphoreType.DMA((GU_SLOTS,)),
                            pltpu.SemaphoreType.DMA((N_B,))]),
        compiler_params=_cparams(("arbitrary",)),
        name="experts",
    )(ent_e, ent_sb0, ent_nsb, misc, xs, w_gate, w_up, w_down)


def _combine_kernel(e_ref, rk_ref, row0_ref, x_ref, wts_ref, g_ref, os_ref, y_ref, gbuf, sem, *, tm):
    i = pl.program_id(0)
    slot = lax.rem(i, 2)

    def issue_tile(tile, slot_):
        base = tile * (tm * TOP_K)

        def issue(r, carry):
            for kk in range(TOP_K):
                a = base + TOP_K * r + kk
                p = row0_ref[e_ref[a]] + rk_ref[a]
                pltpu.make_async_copy(os_ref.at[pl.ds(p, 1), :], gbuf.at[slot_, kk, pl.ds(r, 1), :],
                                      sem.at[slot_]).start()
            return carry

        lax.fori_loop(0, tm, issue, 0, unroll=8)

    @pl.when(i == 0)
    def _():
        issue_tile(0, 0)

    @pl.when(i + 1 < pl.num_programs(0))
    def _():
        issue_tile(i + 1, 1 - slot)

    for kk in range(TOP_K):
        pltpu.make_async_copy(os_ref.at[pl.ds(0, tm), :], gbuf.at[slot, kk], sem.at[slot]).wait()
    moe = wts_ref[:, 0:1] * gbuf[slot, 0] + wts_ref[:, 1:2] * gbuf[slot, 1]
    y_ref[...] = _rms(x_ref[...] + moe, g_ref[...])


def _combine(e_flat, rk_flat, row0, x1, wts, g, out_sorted, tm=256):
    n = x1.shape[0]
    return pl.pallas_call(
        functools.partial(_combine_kernel, tm=tm),
        out_shape=jax.ShapeDtypeStruct((n, D_MODEL), F32),
        grid_spec=pltpu.PrefetchScalarGridSpec(
            num_scalar_prefetch=3,
            grid=(n // tm,),
            in_specs=[pl.BlockSpec((tm, D_MODEL), lambda i, *_: (i, 0)),
                      pl.BlockSpec((tm, LANES), lambda i, *_: (i, 0)),
                      pl.BlockSpec((1, D_MODEL), lambda i, *_: (0, 0)),
                      pl.BlockSpec(memory_space=pl.ANY)],
            out_specs=pl.BlockSpec((tm, D_MODEL), lambda i, *_: (i, 0)),
            scratch_shapes=[pltpu.VMEM((2, TOP_K, tm, D_MODEL), F32),
                            pltpu.SemaphoreType.DMA((2,))]),
        compiler_params=_cparams(("arbitrary",)),
        name="combine",
    )(e_flat, rk_flat, row0, x1, wts, g, out_sorted)


def _moe(x1, norm_g, wr_g, br_g, wr_e, br_e, w_gate, w_up, w_down, final_g):
    n = x1.shape[0]
    pad = LANES - N_EXPERTS - N_GROUPS
    wr = jnp.concatenate([wr_e, wr_g, jnp.zeros((D_MODEL, pad), F32)], axis=1)
    br = jnp.concatenate([br_e, br_g, jnp.zeros((pad,), F32)])[None]
    xp, meta, wts, counts = _router(x1, norm_g[None], wr, br)

    assert n * TOP_K == N_ASSIGN
    cnt = counts[0, :N_EXPERTS]
    nsb = (cnt + R_SUB - 1) // R_SUB
    sb_end = jnp.cumsum(nsb)
    sb_start = sb_end - nsb
    nent = (nsb + N_SUB - 1) // N_SUB
    ent_end = jnp.cumsum(nent)
    ent_start = ent_end - nent
    ids = jnp.arange(NE_MAX, dtype=I32)
    valid = ids < ent_end[-1]
    ent_e = jnp.minimum(jnp.sum(ids[:, None] >= ent_end[None, :], axis=1), N_EXPERTS - 1)
    ent_e = jnp.where(valid, ent_e, ent_e[ent_end[-1] - 1])
    within = (ids - ent_start[ent_e]) * N_SUB
    ent_sb0 = jnp.where(valid, sb_start[ent_e] + within, 0)
    ent_nsb = jnp.where(valid, jnp.clip(nsb[ent_e] - within, 0, N_SUB), 0)
    row0 = (sb_start * R_SUB).astype(I32)
    misc = sb_end[-1:].astype(I32)
    e_flat = meta[:, 0:TOP_K].reshape(-1)
    rk_flat = meta[:, TOP_K:2 * TOP_K].reshape(-1)

    xs = _dispatch(e_flat, rk_flat, row0, cnt, misc, xp)
    out_sorted = _experts(ent_e.astype(I32), ent_sb0.astype(I32), ent_nsb.astype(I32), misc, xs,
                          w_gate, w_up, w_down)
    return _combine(e_flat, rk_flat, row0, x1, wts, final_g[None], out_sorted)


def kernel(x, norm_mix_g, w_in, gmlp_v_norm_g, gmlp_spatial_w, gmlp_spatial_b, mlstm_igate_b,
           mlstm_fgate_b, mlstm_out_norm_g, w_out, norm_ffn_g, router_group_w, router_group_b,
           router_expert_w, router_expert_b, expert_w_gate, expert_w_up, expert_w_down, final_norm_g):
    batch, seq, d = x.shape
    n = batch * seq
    assert w_in.shape[0] == 1, "the final rmsnorm is fused after the single layer's MoE"
    l = 0
    x2 = x.reshape(n, d)
    proj, gate_pre = _in_proj(x2, norm_mix_g[l][None], w_in[l].T)
    y_a = _gmlp(proj, gmlp_v_norm_g[l][None], gmlp_spatial_w[l], gmlp_spatial_b[l].T)
    gate_b = jnp.concatenate([mlstm_igate_b[l], mlstm_fgate_b[l],
                              jnp.zeros((LANES - N_GATE_COLS,), F32)])[None]
    y_b = _mlstm(proj, gate_pre, gate_b, mlstm_out_norm_g[l][None], batch, seq)
    x1 = _out_proj(y_a, y_b, w_out[l].astype(BF16), x2)
    out = _moe(x1, norm_ffn_g[l], router_group_w[l], router_group_b[l], router_expert_w[l],
               router_expert_b[l], expert_w_gate[l], expert_w_up[l], expert_w_down[l], final_norm_g)
    return out.reshape(batch, seq, d)
```

```python
import functools

import jax
import jax.numpy as jnp
from jax import lax
from jax.experimental import pallas as pl
from jax.experimental.pallas import tpu as pltpu

F32 = jnp.float32
BF16 = jnp.bfloat16
I32 = jnp.int32

D_MODEL = 4096
D_GMLP = 2048
GMLP_CHUNK = 128
GMLP_GROUPS = 16
D_MLSTM = 2048
MLSTM_HEADS = 4
MLSTM_DV = 512
MLSTM_DQK = 256
GATE_SOFTCAP = 15.0
N_GROUPS = 8
EXPERTS_PER_GROUP = 8
N_EXPERTS = 64
TOP_K = 2
D_EXPERT = 512
EPS = 1e-6
D_PROJ_MAIN = 10240
N_GATE_COLS = 8

LANES = 128
SUBLANES = 8
VMEM_LIMIT = 56 * 1024 * 1024
MLSTM_L = 256
N_ASSIGN = 4 * 4096 * TOP_K
R_SUB = 256
N_SUB = 3
NSB_MAX = -(-(N_ASSIGN + N_EXPERTS * (R_SUB - 1)) // R_SUB)
NE_MAX = N_EXPERTS + N_ASSIGN // (R_SUB * N_SUB)
GU_CHUNK = 256
N_A = D_EXPERT // GU_CHUNK
DN_CHUNK = 1024
N_B = D_MODEL // DN_CHUNK
GU_SLOTS = N_A + 1


def _cparams(sem):
    return pltpu.CompilerParams(dimension_semantics=sem, vmem_limit_bytes=VMEM_LIMIT)


def _rms(x, g):
    return x * lax.rsqrt(jnp.mean(x * x, axis=-1, keepdims=True) + EPS) * g


def _in_proj_kernel(x_hbm, g_ref, w_ref, wg_ref, proj_ref, gate_ref, x_buf, xn_sc, sem, *, tm, strip):
    i = pl.program_id(0)
    nt = (((1,), (1,)), ((), ()))

    def x_copy(tile):
        return pltpu.make_async_copy(x_hbm.at[pl.ds(pl.multiple_of(tile * tm, tm), tm), :], x_buf, sem)

    @pl.when(pl.program_id(1) == 0)
    def _():
        @pl.when(i == 0)
        def _():
            x_copy(0).start()

        x_copy(i).wait()
        for r in range(0, tm, strip):
            rows = slice(r, r + strip)
            xn_sc[rows, :] = _rms(x_buf[rows, :], g_ref[...]).astype(BF16)

        @pl.when(i + 1 < pl.num_programs(0))
        def _():
            x_copy(i + 1).start()

        wg = jnp.concatenate([wg_ref[...].astype(BF16),
                              jnp.zeros((LANES - N_GATE_COLS, D_MODEL), BF16)], axis=0)
        gate_ref[...] = lax.dot_general(xn_sc[...], wg, nt, preferred_element_type=F32)

    proj_ref[...] = lax.dot_general(xn_sc[...], w_ref[...].astype(BF16), nt,
                                    preferred_element_type=F32).astype(BF16)


def _in_proj(x2, g, w_t, tm=1024, tn=512, strip=32):
    n = x2.shape[0]
    return pl.pallas_call(
        functools.partial(_in_proj_kernel, tm=tm, strip=strip),
        out_shape=(jax.ShapeDtypeStruct((n, D_PROJ_MAIN), BF16),
                   jax.ShapeDtypeStruct((n, LANES), F32)),
        grid=(n // tm, D_PROJ_MAIN // tn),
        in_specs=[pl.BlockSpec(memory_space=pl.ANY),
                  pl.BlockSpec((1, D_MODEL), lambda i, j: (0, 0)),
                  pl.BlockSpec((tn, D_MODEL), lambda i, j: (j, 0)),
                  pl.BlockSpec((N_GATE_COLS, D_MODEL), lambda i, j: (D_PROJ_MAIN // N_GATE_COLS, 0))],
        out_specs=(pl.BlockSpec((tm, tn), lambda i, j: (i, j)),
                   pl.BlockSpec((tm, LANES), lambda i, j: (i, 0))),
        scratch_shapes=[pltpu.VMEM((tm, D_MODEL), F32),
                        pltpu.VMEM((tm, D_MODEL), BF16),
                        pltpu.SemaphoreType.DMA(())],
        compiler_params=_cparams(("arbitrary", "arbitrary")),
        name="in_proj",
    )(x2, g, w_t, w_t)


def _gmlp_kernel(u_ref, v_ref, gv_ref, w_ref, bt_ref, y_ref, wm_sc, *, rows):
    c = GMLP_CHUNK

    @pl.when(pl.program_id(0) == 0)
    def _():
        r = lax.broadcasted_iota(I32, (c, c), 0)
        s = lax.broadcasted_iota(I32, (c, c), 1)
        for g in range(GMLP_GROUPS):
            wm_sc[g] = jnp.where(s <= r, w_ref[g], 0.0).astype(BF16)

    for ci in range(rows // c):
        rs = slice(ci * c, (ci + 1) * c)
        gu = jax.nn.gelu(u_ref[rs, :].astype(F32))
        gv = jax.nn.gelu(v_ref[rs, :].astype(F32))
        vn = _rms(gv, gv_ref[...]).astype(BF16)
        for g in range(GMLP_GROUPS):
            cs = slice(g * c, (g + 1) * c)
            z = jnp.dot(wm_sc[g], vn[:, cs], preferred_element_type=F32) + bt_ref[:, g:g + 1]
            y_ref[rs, cs] = (gu[:, cs] * z).astype(BF16)


def _gmlp(proj, gv, w_s, b_t, rows=512):
    n = proj.shape[0]
    return pl.pallas_call(
        functools.partial(_gmlp_kernel, rows=rows),
        out_shape=jax.ShapeDtypeStruct((n, D_GMLP), BF16),
        grid=(n // rows,),
        in_specs=[pl.BlockSpec((rows, D_GMLP), lambda i: (i, 0)),
                  pl.BlockSpec((rows, D_GMLP), lambda i: (i, 1)),
                  pl.BlockSpec((1, D_GMLP), lambda i: (0, 0)),
                  pl.BlockSpec((GMLP_GROUPS, GMLP_CHUNK, GMLP_CHUNK), lambda i: (0, 0, 0)),
                  pl.BlockSpec((GMLP_CHUNK, GMLP_GROUPS), lambda i: (0, 0))],
        out_specs=pl.BlockSpec((rows, D_GMLP), lambda i: (i, 0)),
        scratch_shapes=[pltpu.VMEM((GMLP_GROUPS, GMLP_CHUNK, GMLP_CHUNK), BF16)],
        compiler_params=_cparams(("arbitrary",)),
        name="gmlp",
    )(proj, proj, gv, w_s, b_t)


def _mlstm_kernel(q_ref, k_ref, v_ref, o_ref, gp_ref, gb_ref, og_ref, y_ref, c_sc, n_sc, m_sc):
    L, H, DQK, DV = MLSTM_L, MLSTM_HEADS, MLSTM_DQK, MLSTM_DV
    scale = DQK ** -0.5

    @pl.when(pl.program_id(1) == 0)
    def _():
        c_sc[...] = jnp.zeros_like(c_sc)
        n_sc[...] = jnp.zeros_like(n_sc)
        m_sc[...] = jnp.zeros_like(m_sc)

    lane = lax.broadcasted_iota(I32, (L, LANES), 1)
    z = gp_ref[...] + gb_ref[...]
    sc = GATE_SOFTCAP * jnp.tanh(z / GATE_SOFTCAP)
    logf = -(jnp.maximum(-sc, 0.0) + jnp.log1p(jnp.exp(-jnp.abs(sc))))
    row = lax.broadcasted_iota(I32, (L, L), 0)
    col = lax.broadcasted_iota(I32, (L, L), 1)
    causal = col <= row
    tri = jnp.where(causal, 1.0, 0.0).astype(BF16)
    bcum, rest = None, logf
    for _ in range(3):
        part = rest.astype(BF16)
        rest = rest - part.astype(F32)
        term = jnp.dot(tri, part, preferred_element_type=F32)
        bcum = term if bcum is None else bcum + term
    gcol = jnp.where(lane < H, sc, bcum)
    grow = gcol.T

    for h in range(H):
        ig_col, b_col = gcol[:, h:h + 1], gcol[:, H + h:H + h + 1]
        ig_row, b_row = grow[h:h + 1, :], grow[H + h:H + h + 1, :]
        m = m_sc[h, 0:1, 0:1]
        q = q_ref[:, h * DQK:(h + 1) * DQK]
        k = k_ref[:, h * DQK:(h + 1) * DQK]
        v = v_ref[:, h * DV:(h + 1) * DV]

        d_log = jnp.where(causal, b_col - b_row + ig_row, -jnp.inf)
        inter = b_col + m
        m_row = jnp.maximum(inter, jnp.max(d_log, axis=-1, keepdims=True))
        w_intra = jnp.exp(d_log - m_row)
        w_inter = jnp.exp(inter - m_row)
        s = lax.dot_general(q, k, (((1,), (1,)), ((), ())), preferred_element_type=F32)
        s = s * scale * w_intra
        c_old = c_sc[h]
        n_old = n_sc[h]
        num = (w_inter * jnp.dot(q, c_old.astype(BF16), preferred_element_type=F32)
               + jnp.dot(s.astype(BF16), v, preferred_element_type=F32))
        qn = jnp.sum(q.astype(F32) * n_old, axis=-1, keepdims=True)
        den = w_inter * qn + jnp.sum(s, axis=-1, keepdims=True)
        hh = num / jnp.maximum(jnp.abs(den), jnp.exp(-m_row))

        b_last = b_row[:, L - 1:L]
        tail = b_last - b_col + ig_col
        m_new = jnp.maximum(b_last + m, jnp.max(tail, axis=0, keepdims=True))
        w_tail = jnp.exp(tail - m_new)
        decay = jnp.exp(b_last + m - m_new)
        kw = k.astype(F32) * scale * w_tail
        c_sc[h] = decay * c_old + lax.dot_general(
            kw.astype(BF16), v, (((0,), (0,)), ((), ())), preferred_element_type=F32)
        n_sc[h] = decay * n_old + jnp.sum(kw, axis=0, keepdims=True)
        m_sc[h] = jnp.broadcast_to(m_new, m_sc.shape[1:])

        hn = hh * lax.rsqrt(jnp.mean(hh * hh, axis=-1, keepdims=True) + EPS)
        hn = hn * og_ref[:, h * DV:(h + 1) * DV]
        hn = hn * jax.nn.sigmoid(o_ref[:, h * DV:(h + 1) * DV].astype(F32))
        y_ref[:, h * DV:(h + 1) * DV] = hn.astype(BF16)


def _mlstm(proj, gate_pre, gate_b, out_g, batch, seq):
    n = proj.shape[0]
    L = MLSTM_L
    nc = seq // L
    qk_w = MLSTM_HEADS * MLSTM_DQK
    q_blk, k_blk = 2 * D_GMLP // qk_w, 2 * D_GMLP // qk_w + 1
    v_blk, o_blk = (2 * D_GMLP + 2 * qk_w) // D_MLSTM, (2 * D_GMLP + 2 * qk_w) // D_MLSTM + 1
    rmap = lambda b, c: b * nc + c
    return pl.pallas_call(
        _mlstm_kernel,
        out_shape=jax.ShapeDtypeStruct((n, D_MLSTM), BF16),
        grid=(batch, nc),
        in_specs=[pl.BlockSpec((L, qk_w), lambda b, c: (rmap(b, c), q_blk)),
                  pl.BlockSpec((L, qk_w), lambda b, c: (rmap(b, c), k_blk)),
                  pl.BlockSpec((L, D_MLSTM), lambda b, c: (rmap(b, c), v_blk)),
                  pl.BlockSpec((L, D_MLSTM), lambda b, c: (rmap(b, c), o_blk)),
                  pl.BlockSpec((L, LANES), lambda b, c: (rmap(b, c), 0)),
                  pl.BlockSpec((1, LANES), lambda b, c: (0, 0)),
                  pl.BlockSpec((1, D_MLSTM), lambda b, c: (0, 0))],
        out_specs=pl.BlockSpec((L, D_MLSTM), lambda b, c: (rmap(b, c), 0)),
        scratch_shapes=[pltpu.VMEM((MLSTM_HEADS, MLSTM_DQK, MLSTM_DV), F32),
                        pltpu.VMEM((MLSTM_HEADS, 1, MLSTM_DQK), F32),
                        pltpu.VMEM((MLSTM_HEADS, 8, LANES), F32)],
        compiler_params=_cparams(("parallel", "arbitrary")),
        name="mlstm",
    )(proj, proj, proj, proj, gate_pre, gate_b, out_g)


def _out_proj_kernel(ya_ref, yb_ref, wa_ref, wb_ref, x_ref, o_ref):
    acc = jnp.dot(ya_ref[...], wa_ref[...], preferred_element_type=F32)
    acc = acc + jnp.dot(yb_ref[...], wb_ref[...], preferred_element_type=F32)
    o_ref[...] = x_ref[...] + acc


def _out_proj(y_a, y_b, w_out, x2, tm=1024, tn=1024):
    n = x2.shape[0]
    return pl.pallas_call(
        _out_proj_kernel,
        out_shape=jax.ShapeDtypeStruct((n, D_MODEL), F32),
        grid=(n // tm, D_MODEL // tn),
        in_specs=[pl.BlockSpec((tm, D_GMLP), lambda i, j: (i, 0)),
                  pl.BlockSpec((tm, D_MLSTM), lambda i, j: (i, 0)),
                  pl.BlockSpec((D_GMLP, tn), lambda i, j: (0, j)),
                  pl.BlockSpec((D_MLSTM, tn), lambda i, j: (1, j)),
                  pl.BlockSpec((tm, tn), lambda i, j: (i, j))],
        out_specs=pl.BlockSpec((tm, tn), lambda i, j: (i, j)),
        compiler_params=_cparams(("parallel", "arbitrary")),
        name="out_proj",
    )(y_a, y_b, w_out, w_out, x2)


def _router_kernel(x_ref, g_ref, wr_ref, br_ref, xp_ref, meta_ref, wts_ref, cnt_ref, carry_sc, w2_sc,
                   *, tm):
    i = pl.program_id(0)

    @pl.when(i == 0)
    def _():
        carry_sc[...] = jnp.zeros_like(carry_sc)
        w_hi = wr_ref[...].astype(BF16)
        w2_sc[:, :LANES] = w_hi
        w2_sc[:, LANES:] = (wr_ref[...] - w_hi.astype(F32)).astype(BF16)

    xn = _rms(x_ref[...], g_ref[...])
    xp_ref[...] = xn

    x_hi = xn.astype(BF16)
    x_lo = (xn - x_hi.astype(F32)).astype(BF16)
    both = jnp.dot(x_hi, w2_sc[...], preferred_element_type=F32)
    logits = (both[:, :LANES] + both[:, LANES:]
              + jnp.dot(x_lo, w2_sc[:, :LANES], preferred_element_type=F32)) + br_ref[...]
    lane = lax.broadcasted_iota(I32, (tm, LANES), 1)
    lane_f = lane.astype(F32)
    big = float(LANES)
    is_g = (lane >= N_EXPERTS) & (lane < N_EXPERTS + N_GROUPS)
    gl = jnp.where(is_g, logits, -jnp.inf)
    gmax = jnp.max(gl, axis=-1, keepdims=True)
    g_lane = jnp.min(jnp.where(gl == gmax, lane_f, big), axis=-1, keepdims=True)
    g_sum = jnp.sum(jnp.where(is_g, jnp.exp(gl - gmax), 0.0), axis=-1, keepdims=True)
    g_prob = 1.0 / g_sum
    g_idx = g_lane.astype(I32) - N_EXPERTS
    in_grp = (lane < N_EXPERTS) & (lax.shift_right_logical(lane, 3) == g_idx)
    el = jnp.where(in_grp, logits, -jnp.inf)
    t1 = jnp.max(el, axis=-1, keepdims=True)
    i1 = jnp.min(jnp.where(el == t1, lane_f, big), axis=-1, keepdims=True)
    el2 = jnp.where(lane_f == i1, -jnp.inf, el)
    t2 = jnp.max(el2, axis=-1, keepdims=True)
    i2 = jnp.min(jnp.where(el2 == t2, lane_f, big), axis=-1, keepdims=True)
    e2 = jnp.exp(t2 - t1)
    w1 = g_prob * (1.0 / (1.0 + e2))
    w2 = g_prob * (e2 / (1.0 + e2))

    sel1 = lane_f == i1
    sel2 = lane_f == i2
    onehot = jnp.where(sel1 | sel2, 1.0, 0.0)
    r = lax.broadcasted_iota(I32, (tm, tm), 0)
    c = lax.broadcasted_iota(I32, (tm, tm), 1)
    strict = jnp.where(c < r, 1.0, 0.0).astype(BF16)
    prefix = jnp.dot(strict, onehot.astype(BF16), preferred_element_type=F32) + carry_sc[0:1, :]
    rank1 = jnp.sum(jnp.where(sel1, prefix, 0.0), axis=-1, keepdims=True)
    rank2 = jnp.sum(jnp.where(sel2, prefix, 0.0), axis=-1, keepdims=True)
    total = carry_sc[0:1, :] + jnp.sum(onehot, axis=0, keepdims=True)
    carry_sc[...] = jnp.broadcast_to(total, carry_sc.shape)
    cnt_ref[...] = jnp.broadcast_to(total, cnt_ref.shape).astype(I32)

    meta = jnp.where(lane == 0, i1, jnp.where(lane == 1, i2,
                     jnp.where(lane == 2, rank1, jnp.where(lane == 3, rank2, 0.0))))
    meta_ref[...] = meta.astype(I32)
    wts_ref[...] = jnp.where(lane == 0, w1, jnp.where(lane == 1, w2, 0.0))


def _router(x1, g, wr, br, tm=512):
    n = x1.shape[0]
    return pl.pallas_call(
        functools.partial(_router_kernel, tm=tm),
        out_shape=(jax.ShapeDtypeStruct((n, D_MODEL), F32),
                   jax.ShapeDtypeStruct((n, LANES), I32),
                   jax.ShapeDtypeStruct((n, LANES), F32),
                   jax.ShapeDtypeStruct((8, LANES), I32)),
        grid=(n // tm,),
        in_specs=[pl.BlockSpec((tm, D_MODEL), lambda i: (i, 0)),
                  pl.BlockSpec((1, D_MODEL), lambda i: (0, 0)),
                  pl.BlockSpec((D_MODEL, LANES), lambda i: (0, 0)),
                  pl.BlockSpec((1, LANES), lambda i: (0, 0))],
        out_specs=(pl.BlockSpec((tm, D_MODEL), lambda i: (i, 0)),
                   pl.BlockSpec((tm, LANES), lambda i: (i, 0)),
                   pl.BlockSpec((tm, LANES), lambda i: (i, 0)),
                   pl.BlockSpec((8, LANES), lambda i: (0, 0))),
        scratch_shapes=[pltpu.VMEM((8, LANES), F32),
                        pltpu.VMEM((D_MODEL, 2 * LANES), BF16)],
        compiler_params=_cparams(("arbitrary",)),
        name="router",
    )(x1, g, wr, br)


def _dispatch_kernel(e_ref, rk_ref, row0_ref, cnt_ref, misc_ref, xp_ref, xs_ref, zbuf, sem, zsem, *, tm):
    base = pl.program_id(0) * (tm * TOP_K)

    @pl.when(pl.program_id(0) == 0)
    def _():
        zbuf[...] = jnp.zeros_like(zbuf)

        def pad_fill(e, carry, *, wait):
            cnt = cnt_ref[e]
            start = row0_ref[e] + cnt
            head = (-cnt) & (SUBLANES - 1)
            for j in range(SUBLANES - 1):
                @pl.when(j < head)
                def _(j=j):
                    cp = pltpu.make_async_copy(zbuf.at[pl.ds(0, 1), :], xs_ref.at[pl.ds(start + j, 1), :], zsem)
                    cp.wait() if wait else cp.start()
            off = start + head
            rest = (-(cnt + head)) & (R_SUB - 1)
            bit = R_SUB // 2
            while bit >= SUBLANES:
                @pl.when((rest & bit) != 0)
                def _(off=off, bit=bit):
                    dst = xs_ref.at[pl.ds(pl.multiple_of(off, SUBLANES), bit), :]
                    cp = pltpu.make_async_copy(zbuf.at[pl.ds(0, bit), :], dst, zsem)
                    cp.wait() if wait else cp.start()
                off = off + (rest & bit)
                bit //= 2
            return carry

        def tail_fill(sb, carry, *, wait):
            r = pl.multiple_of(sb * R_SUB, R_SUB)
            cp = pltpu.make_async_copy(zbuf, xs_ref.at[pl.ds(r, R_SUB), :], zsem)
            cp.wait() if wait else cp.start()
            return carry

        for wait in (False, True):
            lax.fori_loop(0, N_EXPERTS, functools.partial(pad_fill, wait=wait), 0)
            lax.fori_loop(misc_ref[0], NSB_MAX, functools.partial(tail_fill, wait=wait), 0)

    tok0 = pl.program_id(0) * tm

    def row_copy(r, kk):
        a = base + TOP_K * r + kk
        p = row0_ref[e_ref[a]] + rk_ref[a]
        return pltpu.make_async_copy(xp_ref.at[pl.ds(tok0 + r, 1), :], xs_ref.at[pl.ds(p, 1), :], sem)

    def issue(r, carry):
        for kk in range(TOP_K):
            row_copy(r, kk).start()
        return carry

    def drain_one_step():
        for _ in range(TOP_K):
            pltpu.make_async_copy(xp_ref.at[pl.ds(0, tm), :], xs_ref.at[pl.ds(0, tm), :], sem).wait()

    lax.fori_loop(0, tm, issue, 0, unroll=8)

    @pl.when(pl.program_id(0) > 0)
    def _():
        drain_one_step()

    @pl.when(pl.program_id(0) == pl.num_programs(0) - 1)
    def _():
        drain_one_step()


def _dispatch(e_flat, rk_flat, row0, cnt, misc, xp, tm=512):
    n = xp.shape[0]
    return pl.pallas_call(
        functools.partial(_dispatch_kernel, tm=tm),
        out_shape=jax.ShapeDtypeStruct((NSB_MAX * R_SUB, D_MODEL), F32),
        grid_spec=pltpu.PrefetchScalarGridSpec(
            num_scalar_prefetch=5,
            grid=(n // tm,),
            in_specs=[pl.BlockSpec(memory_space=pl.ANY)],
            out_specs=pl.BlockSpec(memory_space=pl.ANY),
            scratch_shapes=[pltpu.VMEM((R_SUB, D_MODEL), F32),
                            pltpu.SemaphoreType.DMA(()),
                            pltpu.SemaphoreType.DMA(())]),
        compiler_params=_cparams(("arbitrary",)),
        name="dispatch",
    )(e_flat, rk_flat, row0, cnt, misc, xp)


def _expert_kernel(ee_ref, sb0_ref, nsb_ref, misc_ref, xs_hbm, wg_hbm, wu_hbm, wd_hbm, os_hbm,
                   xbuf, hbuf, obuf, zbuf, gbuf, ubuf, dbuf, xsem, osem, zsem, gsem, dsem):
    s = pl.program_id(0)
    n_ent = pl.num_programs(0)
    nsb = nsb_ref[s]
    nxt = jnp.minimum(s + 1, n_ent - 1)
    has_next = (s + 1 < n_ent) & (nsb_ref[nxt] > 0)

    def sub_row(ent, sb):
        return pl.multiple_of((sb0_ref[ent] + sb) * R_SUB, R_SUB)

    def x_copy(ent, sb, wait):
        cp = pltpu.make_async_copy(xs_hbm.at[pl.ds(sub_row(ent, sb), R_SUB), :], xbuf.at[sb], xsem.at[sb])
        cp.wait() if wait else cp.start()

    def x_copies(ent, wait):
        for sb in range(N_SUB):
            @pl.when(sb < nsb_ref[ent])
            def _(sb=sb):
                x_copy(ent, sb, wait)

    def gu_slot(ent, c):
        return lax.rem(ent * N_A + c, GU_SLOTS)

    def gu_copies(ent, c, wait):
        e, sl = ee_ref[ent], gu_slot(ent, c)
        for w_hbm, buf in ((wg_hbm, gbuf), (wu_hbm, ubuf)):
            cp = pltpu.make_async_copy(w_hbm.at[e, :, pl.ds(c * GU_CHUNK, GU_CHUNK)], buf.at[sl], gsem.at[sl])
            cp.wait() if wait else cp.start()

    def dn_copy(ent, nb, wait):
        cp = pltpu.make_async_copy(wd_hbm.at[ee_ref[ent], :, pl.ds(nb * DN_CHUNK, DN_CHUNK)],
                                   dbuf.at[nb], dsem.at[nb])
        cp.wait() if wait else cp.start()

    def o_copy(ent, sb, nb, wait):
        cp = pltpu.make_async_copy(
            obuf.at[nb % 2, sb],
            os_hbm.at[pl.ds(sub_row(ent, sb), R_SUB), pl.ds(nb * DN_CHUNK, DN_CHUNK)], osem.at[nb % 2])
        cp.wait() if wait else cp.start()

    def o_waits(ent, nb):
        for sb in range(N_SUB):
            @pl.when(sb < nsb_ref[ent])
            def _(sb=sb):
                o_copy(ent, sb, nb, True)

    @pl.when(s == 0)
    def _():
        x_copies(0, False)
        for c in range(N_A):
            gu_copies(0, c, False)
        for nb in range(N_B):
            dn_copy(0, nb, False)

    @pl.when(nsb == 0)
    def _():
        @pl.when(s >= 1)
        def _():
            for nb in range(N_B - 2, N_B):
                o_waits(s - 1, nb)

    @pl.when(nsb > 0)
    def _():
        x_copies(s, True)

        for c in range(N_A):
            gu_copies(s, c, True)

            @pl.when(has_next)
            def _(c=c):
                gu_copies(nxt, c, False)
            for sb in range(N_SUB):
                @pl.when(sb < nsb)
                def _(sb=sb, c=c):
                    xb = xbuf[sb].astype(BF16)
                    sl = gu_slot(s, c)
                    gate = jnp.dot(xb, gbuf[sl].astype(BF16), preferred_element_type=F32)
                    up = jnp.dot(xb, ubuf[sl].astype(BF16), preferred_element_type=F32)
                    hbuf[c, sb] = (jax.nn.silu(gate) * up).astype(BF16)

                if c == N_A - 1:
                    @pl.when(has_next & (sb < nsb_ref[nxt]))
                    def _(sb=sb):
                        x_copy(nxt, sb, False)

        for nb in range(N_B):
            if nb >= 2:
                o_waits(s, nb - 2)
            else:
                @pl.when(s >= 1)
                def _(nb=nb):
                    o_waits(s - 1, nb + N_B - 2)
            dn_copy(s, nb, True)
            for sb in range(N_SUB):
                @pl.when(sb < nsb)
                def _(sb=sb, nb=nb):
                    wd = dbuf.at[nb]
                    acc = jnp.dot(hbuf[0, sb], wd[0:GU_CHUNK, :].astype(BF16), preferred_element_type=F32)
                    for ca in range(1, N_A):
                        acc = acc + jnp.dot(hbuf[ca, sb],
                                            wd[ca * GU_CHUNK:(ca + 1) * GU_CHUNK, :].astype(BF16),
                                            preferred_element_type=F32)
                    obuf[nb % 2, sb] = acc
                    o_copy(s, sb, nb, False)

            @pl.when(has_next)
            def _(nb=nb):
                dn_copy(nxt, nb, False)

    @pl.when(s == n_ent - 1)
    def _():
        @pl.when(nsb > 0)
        def _():
            for nb in range(N_B - 2, N_B):
                o_waits(s, nb)
        zbuf[...] = jnp.zeros_like(zbuf)

        def tail_fill(sb, carry, *, wait):
            r = pl.multiple_of(sb * R_SUB, R_SUB)
            for nb in range(N_B):
                cp = pltpu.make_async_copy(
                    zbuf, os_hbm.at[pl.ds(r, R_SUB), pl.ds(nb * DN_CHUNK, DN_CHUNK)], zsem)
                cp.wait() if wait else cp.start()
            return carry

        for wait in (False, True):
            lax.fori_loop(misc_ref[0], NSB_MAX, functools.partial(tail_fill, wait=wait), 0)


def _experts(ent_e, ent_sb0, ent_nsb, misc, xs, w_gate, w_up, w_down):
    assert N_B % 2 == 0, "output column chunks alternate between two staging sets"
    any_spec = pl.BlockSpec(memory_space=pl.ANY)
    return pl.pallas_call(
        _expert_kernel,
        out_shape=jax.ShapeDtypeStruct((NSB_MAX * R_SUB, D_MODEL), F32),
        grid_spec=pltpu.PrefetchScalarGridSpec(
            num_scalar_prefetch=4,
            grid=(NE_MAX,),
            in_specs=[any_spec, any_spec, any_spec, any_spec],
            out_specs=any_spec,
            scratch_shapes=[pltpu.VMEM((N_SUB, R_SUB, D_MODEL), F32),
                            pltpu.VMEM((N_A, N_SUB, R_SUB, GU_CHUNK), BF16),
                            pltpu.VMEM((2, N_SUB, R_SUB, DN_CHUNK), F32),
                            pltpu.VMEM((R_SUB, DN_CHUNK), F32),
                            pltpu.VMEM((GU_SLOTS, D_MODEL, GU_CHUNK), F32),
                            pltpu.VMEM((GU_SLOTS, D_MODEL, GU_CHUNK), F32),
                            pltpu.VMEM((N_B, D_EXPERT, DN_CHUNK), F32),
                            pltpu.SemaphoreType.DMA((N_SUB,)),
                            pltpu.SemaphoreType.DMA((2,)),
                            pltpu.SemaphoreType.DMA(()),
                            pltpu.SemaphoreType.DMA((GU_SLOTS,)),
                            pltpu.SemaphoreType.DMA((N_B,))]),
        compiler_params=_cparams(("arbitrary",)),
        name="experts",
    )(ent_e, ent_sb0, ent_nsb, misc, xs, w_gate, w_up, w_down)


def _combine_kernel(e_ref, rk_ref, row0_ref, x_ref, wts_ref, g_ref, os_ref, y_ref, gbuf, sem, *, tm):
    i = pl.program_id(0)
    slot = lax.rem(i, 2)

    def issue_tile(tile, slot_):
        base = tile * (tm * TOP_K)

        def issue(r, carry):
            for kk in range(TOP_K):
                a = base + TOP_K * r + kk
                p = row0_ref[e_ref[a]] + rk_ref[a]
                pltpu.make_async_copy(os_ref.at[pl.ds(p, 1), :], gbuf.at[slot_, kk, pl.ds(r, 1), :],
                                      sem.at[slot_]).start()
            return carry

        lax.fori_loop(0, tm, issue, 0, unroll=8)

    @pl.when(i == 0)
    def _():
        issue_tile(0, 0)

    @pl.when(i + 1 < pl.num_programs(0))
    def _():
        issue_tile(i + 1, 1 - slot)

    for kk in range(TOP_K):
        pltpu.make_async_copy(os_ref.at[pl.ds(0, tm), :], gbuf.at[slot, kk], sem.at[slot]).wait()
    moe = wts_ref[:, 0:1] * gbuf[slot, 0] + wts_ref[:, 1:2] * gbuf[slot, 1]
    y_ref[...] = _rms(x_ref[...] + moe, g_ref[...])


def _combine(e_flat, rk_flat, row0, x1, wts, g, out_sorted, tm=256):
    n = x1.shape[0]
    return pl.pallas_call(
        functools.partial(_combine_kernel, tm=tm),
        out_shape=jax.ShapeDtypeStruct((n, D_MODEL), F32),
        grid_spec=pltpu.PrefetchScalarGridSpec(
            num_scalar_prefetch=3,
            grid=(n // tm,),
            in_specs=[pl.BlockSpec((tm, D_MODEL), lambda i, *_: (i, 0)),
                      pl.BlockSpec((tm, LANES), lambda i, *_: (i, 0)),
                      pl.BlockSpec((1, D_MODEL), lambda i, *_: (0, 0)),
                      pl.BlockSpec(memory_space=pl.ANY)],
            out_specs=pl.BlockSpec((tm, D_MODEL), lambda i, *_: (i, 0)),
            scratch_shapes=[pltpu.VMEM((2, TOP_K, tm, D_MODEL), F32),
                            pltpu.SemaphoreType.DMA((2,))]),
        compiler_params=_cparams(("arbitrary",)),
        name="combine",
    )(e_flat, rk_flat, row0, x1, wts, g, out_sorted)


def _moe(x1, norm_g, wr_g, br_g, wr_e, br_e, w_gate, w_up, w_down, final_g):
    n = x1.shape[0]
    pad = LANES - N_EXPERTS - N_GROUPS
    wr = jnp.concatenate([wr_e, wr_g, jnp.zeros((D_MODEL, pad), F32)], axis=1)
    br = jnp.concatenate([br_e, br_g, jnp.zeros((pad,), F32)])[None]
    xp, meta, wts, counts = _router(x1, norm_g[None], wr, br)

    assert n * TOP_K == N_ASSIGN
    cnt = counts[0, :N_EXPERTS]
    nsb = (cnt + R_SUB - 1) // R_SUB
    sb_end = jnp.cumsum(nsb)
    sb_start = sb_end - nsb
    nent = (nsb + N_SUB - 1) // N_SUB
    ent_end = jnp.cumsum(nent)
    ent_start = ent_end - nent
    ids = jnp.arange(NE_MAX, dtype=I32)
    valid = ids < ent_end[-1]
    ent_e = jnp.minimum(jnp.sum(ids[:, None] >= ent_end[None, :], axis=1), N_EXPERTS - 1)
    ent_e = jnp.where(valid, ent_e, ent_e[ent_end[-1] - 1])
    within = (ids - ent_start[ent_e]) * N_SUB
    ent_sb0 = jnp.where(valid, sb_start[ent_e] + within, 0)
    ent_nsb = jnp.where(valid, jnp.clip(nsb[ent_e] - within, 0, N_SUB), 0)
    row0 = (sb_start * R_SUB).astype(I32)
    misc = sb_end[-1:].astype(I32)
    e_flat = meta[:, 0:TOP_K].reshape(-1)
    rk_flat = meta[:, TOP_K:2 * TOP_K].reshape(-1)

    xs = _dispatch(e_flat, rk_flat, row0, cnt, misc, xp)
    out_sorted = _experts(ent_e.astype(I32), ent_sb0.astype(I32), ent_nsb.astype(I32), misc, xs,
                          w_gate, w_up, w_down)
    return _combine(e_flat, rk_flat, row0, x1, wts, final_g[None], out_sorted)


def kernel(x, norm_mix_g, w_in, gmlp_v_norm_g, gmlp_spatial_w, gmlp_spatial_b, mlstm_igate_b,
           mlstm_fgate_b, mlstm_out_norm_g, w_out, norm_ffn_g, router_group_w, router_group_b,
           router_expert_w, router_expert_b, expert_w_gate, expert_w_up, expert_w_down, final_norm_g):
    batch, seq, d = x.shape
    n = batch * seq
    assert w_in.shape[0] == 1, "the final rmsnorm is fused after the single layer's MoE"
    l = 0
    x2 = x.reshape(n, d)
    proj, gate_pre = _in_proj(x2, norm_mix_g[l][None], w_in[l].T)
    y_a = _gmlp(proj, gmlp_v_norm_g[l][None], gmlp_spatial_w[l], gmlp_spatial_b[l].T)
    gate_b = jnp.concatenate([mlstm_igate_b[l], mlstm_fgate_b[l],
                              jnp.zeros((LANES - N_GATE_COLS,), F32)])[None]
    y_b = _mlstm(proj, gate_pre, gate_b, mlstm_out_norm_g[l][None], batch, seq)
    x1 = _out_proj(y_a, y_b, w_out[l].astype(BF16), x2)
    out = _moe(x1, norm_ffn_g[l], router_group_w[l], router_group_b[l], router_expert_w[l],
               router_expert_b[l], expert_w_gate[l], expert_w_up[l], expert_w_down[l], final_norm_g)
    return out.reshape(batch, seq, d)
```

```python
import functools

import jax
import jax.numpy as jnp
from jax import lax
from jax.experimental import pallas as pl
from jax.experimental.pallas import tpu as pltpu

F32 = jnp.float32
BF16 = jnp.bfloat16
I32 = jnp.int32

D_MODEL = 4096
D_GMLP = 2048
GMLP_CHUNK = 128
GMLP_GROUPS = 16
D_MLSTM = 2048
MLSTM_HEADS = 4
MLSTM_DV = 512
MLSTM_DQK = 256
GATE_SOFTCAP = 15.0
N_GROUPS = 8
EXPERTS_PER_GROUP = 8
N_EXPERTS = 64
TOP_K = 2
D_EXPERT = 512
EPS = 1e-6
D_PROJ_MAIN = 10240
N_GATE_COLS = 8

LANES = 128
SUBLANES = 8
VMEM_LIMIT = 56 * 1024 * 1024
MLSTM_L = 256
N_ASSIGN = 4 * 4096 * TOP_K
R_SUB = 256
N_SUB = 3
NSB_MAX = -(-(N_ASSIGN + N_EXPERTS * (R_SUB - 1)) // R_SUB)
NE_MAX = N_EXPERTS + N_ASSIGN // (R_SUB * N_SUB)
GU_CHUNK = 256
N_A = D_EXPERT // GU_CHUNK
DN_CHUNK = 1024
N_B = D_MODEL // DN_CHUNK
GU_SLOTS = N_A + 1


def _cparams(sem):
    return pltpu.CompilerParams(dimension_semantics=sem, vmem_limit_bytes=VMEM_LIMIT)


def _rms(x, g):
    return x * lax.rsqrt(jnp.mean(x * x, axis=-1, keepdims=True) + EPS) * g


def _in_proj_kernel(x_hbm, g_ref, w_ref, wg_ref, proj_ref, gate_ref, x_buf, xn_sc, sem, *, tm, strip):
    i = pl.program_id(0)
    nt = (((1,), (1,)), ((), ()))

    def x_copy(tile):
        return pltpu.make_async_copy(x_hbm.at[pl.ds(pl.multiple_of(tile * tm, tm), tm), :], x_buf, sem)

    @pl.when(pl.program_id(1) == 0)
    def _():
        @pl.when(i == 0)
        def _():
            x_copy(0).start()

        x_copy(i).wait()
        for r in range(0, tm, strip):
            rows = slice(r, r + strip)
            xn_sc[rows, :] = _rms(x_buf[rows, :], g_ref[...]).astype(BF16)

        @pl.when(i + 1 < pl.num_programs(0))
        def _():
            x_copy(i + 1).start()

        wg = jnp.concatenate([wg_ref[...].astype(BF16),
                              jnp.zeros((LANES - N_GATE_COLS, D_MODEL), BF16)], axis=0)
        gate_ref[...] = lax.dot_general(xn_sc[...], wg, nt, preferred_element_type=F32)

    proj_ref[...] = lax.dot_general(xn_sc[...], w_ref[...].astype(BF16), nt,
                                    preferred_element_type=F32).astype(BF16)


def _in_proj(x2, g, w_t, tm=1024, tn=512, strip=32):
    n = x2.shape[0]
    return pl.pallas_call(
        functools.partial(_in_proj_kernel, tm=tm, strip=strip),
        out_shape=(jax.ShapeDtypeStruct((n, D_PROJ_MAIN), BF16),
                   jax.ShapeDtypeStruct((n, LANES), F32)),
        grid=(n // tm, D_PROJ_MAIN // tn),
        in_specs=[pl.BlockSpec(memory_space=pl.ANY),
                  pl.BlockSpec((1, D_MODEL), lambda i, j: (0, 0)),
                  pl.BlockSpec((tn, D_MODEL), lambda i, j: (j, 0)),
                  pl.BlockSpec((N_GATE_COLS, D_MODEL), lambda i, j: (D_PROJ_MAIN // N_GATE_COLS, 0))],
        out_specs=(pl.BlockSpec((tm, tn), lambda i, j: (i, j)),
                   pl.BlockSpec((tm, LANES), lambda i, j: (i, 0))),
        scratch_shapes=[pltpu.VMEM((tm, D_MODEL), F32),
                        pltpu.VMEM((tm, D_MODEL), BF16),
                        pltpu.SemaphoreType.DMA(())],
        compiler_params=_cparams(("arbitrary", "arbitrary")),
        name="in_proj",
    )(x2, g, w_t, w_t)


def _gmlp_kernel(u_ref, v_ref, gv_ref, w_ref, bt_ref, y_ref, wm_sc, *, rows):
    c = GMLP_CHUNK

    @pl.when(pl.program_id(0) == 0)
    def _():
        r = lax.broadcasted_iota(I32, (c, c), 0)
        s = lax.broadcasted_iota(I32, (c, c), 1)
        for g in range(GMLP_GROUPS):
            wm_sc[g] = jnp.where(s <= r, w_ref[g], 0.0).astype(BF16)

    for ci in range(rows // c):
        rs = slice(ci * c, (ci + 1) * c)
        gu = jax.nn.gelu(u_ref[rs, :].astype(F32))
        gv = jax.nn.gelu(v_ref[rs, :].astype(F32))
        vn = _rms(gv, gv_ref[...]).astype(BF16)
        for g in range(GMLP_GROUPS):
            cs = slice(g * c, (g + 1) * c)
            z = jnp.dot(wm_sc[g], vn[:, cs], preferred_element_type=F32) + bt_ref[:, g:g + 1]
            y_ref[rs, cs] = (gu[:, cs] * z).astype(BF16)


def _gmlp(proj, gv, w_s, b_t, rows=512):
    n = proj.shape[0]
    return pl.pallas_call(
        functools.partial(_gmlp_kernel, rows=rows),
        out_shape=jax.ShapeDtypeStruct((n, D_GMLP), BF16),
        grid=(n // rows,),
        in_specs=[pl.BlockSpec((rows, D_GMLP), lambda i: (i, 0)),
                  pl.BlockSpec((rows, D_GMLP), lambda i: (i, 1)),
                  pl.BlockSpec((1, D_GMLP), lambda i: (0, 0)),
                  pl.BlockSpec((GMLP_GROUPS, GMLP_CHUNK, GMLP_CHUNK), lambda i: (0, 0, 0)),
                  pl.BlockSpec((GMLP_CHUNK, GMLP_GROUPS), lambda i: (0, 0))],
        out_specs=pl.BlockSpec((rows, D_GMLP), lambda i: (i, 0)),
        scratch_shapes=[pltpu.VMEM((GMLP_GROUPS, GMLP_CHUNK, GMLP_CHUNK), BF16)],
        compiler_params=_cparams(("arbitrary",)),
        name="gmlp",
    )(proj, proj, gv, w_s, b_t)


def _mlstm_kernel(q_ref, k_ref, v_ref, o_ref, gp_ref, gb_ref, og_ref, y_ref, c_sc, n_sc, m_sc):
    L, H, DQK, DV = MLSTM_L, MLSTM_HEADS, MLSTM_DQK, MLSTM_DV
    scale = DQK ** -0.5

    @pl.when(pl.program_id(1) == 0)
    def _():
        c_sc[...] = jnp.zeros_like(c_sc)
        n_sc[...] = jnp.zeros_like(n_sc)
        m_sc[...] = jnp.zeros_like(m_sc)

    lane = lax.broadcasted_iota(I32, (L, LANES), 1)
    z = gp_ref[...] + gb_ref[...]
    sc = GATE_SOFTCAP * jnp.tanh(z / GATE_SOFTCAP)
    logf = -(jnp.maximum(-sc, 0.0) + jnp.log1p(jnp.exp(-jnp.abs(sc))))
    row = lax.broadcasted_iota(I32, (L, L), 0)
    col = lax.broadcasted_iota(I32, (L, L), 1)
    causal = col <= row
    tri = jnp.where(causal, 1.0, 0.0).astype(BF16)
    bcum, rest = None, logf
    for _ in range(3):
        part = rest.astype(BF16)
        rest = rest - part.astype(F32)
        term = jnp.dot(tri, part, preferred_element_type=F32)
        bcum = term if bcum is None else bcum + term
    gcol = jnp.where(lane < H, sc, bcum)
    grow = gcol.T

    for h in range(H):
        ig_col, b_col = gcol[:, h:h + 1], gcol[:, H + h:H + h + 1]
        ig_row, b_row = grow[h:h + 1, :], grow[H + h:H + h + 1, :]
        m = m_sc[h, 0:1, 0:1]
        q = q_ref[:, h * DQK:(h + 1) * DQK]
        k = k_ref[:, h * DQK:(h + 1) * DQK]
        v = v_ref[:, h * DV:(h + 1) * DV]

        d_log = jnp.where(causal, b_col - b_row + ig_row, -jnp.inf)
        inter = b_col + m
        m_row = jnp.maximum(inter, jnp.max(d_log, axis=-1, keepdims=True))
        w_intra = jnp.exp(d_log - m_row)
        w_inter = jnp.exp(inter - m_row)
        s = lax.dot_general(q, k, (((1,), (1,)), ((), ())), preferred_element_type=F32)
        s = s * scale * w_intra
        c_old = c_sc[h]
        n_old = n_sc[h]
        num = (w_inter * jnp.dot(q, c_old.astype(BF16), preferred_element_type=F32)
               + jnp.dot(s.astype(BF16), v, preferred_element_type=F32))
        qn = jnp.sum(q.astype(F32) * n_old, axis=-1, keepdims=True)
        den = w_inter * qn + jnp.sum(s, axis=-1, keepdims=True)
        hh = num / jnp.maximum(jnp.abs(den), jnp.exp(-m_row))

        b_last = b_row[:, L - 1:L]
        tail = b_last - b_col + ig_col
        m_new = jnp.maximum(b_last + m, jnp.max(tail, axis=0, keepdims=True))
        w_tail = jnp.exp(tail - m_new)
        decay = jnp.exp(b_last + m - m_new)
        kw = k.astype(F32) * scale * w_tail
        c_sc[h] = decay * c_old + lax.dot_general(
            kw.astype(BF16), v, (((0,), (0,)), ((), ())), preferred_element_type=F32)
        n_sc[h] = decay * n_old + jnp.sum(kw, axis=0, keepdims=True)
        m_sc[h] = jnp.broadcast_to(m_new, m_sc.shape[1:])

        hn = hh * lax.rsqrt(jnp.mean(hh * hh, axis=-1, keepdims=True) + EPS)
        hn = hn * og_ref[:, h * DV:(h + 1) * DV]
        hn = hn * jax.nn.sigmoid(o_ref[:, h * DV:(h + 1) * DV].astype(F32))
        y_ref[:, h * DV:(h + 1) * DV] = hn.astype(BF16)


def _mlstm(proj, gate_pre, gate_b, out_g, batch, seq):
    n = proj.shape[0]
    L = MLSTM_L
    nc = seq // L
    qk_w = MLSTM_HEADS * MLSTM_DQK
    q_blk, k_blk = 2 * D_GMLP // qk_w, 2 * D_GMLP // qk_w + 1
    v_blk, o_blk = (2 * D_GMLP + 2 * qk_w) // D_MLSTM, (2 * D_GMLP + 2 * qk_w) // D_MLSTM + 1
    rmap = lambda b, c: b * nc + c
    return pl.pallas_call(
        _mlstm_kernel,
        out_shape=jax.ShapeDtypeStruct((n, D_MLSTM), BF16),
        grid=(batch, nc),
        in_specs=[pl.BlockSpec((L, qk_w), lambda b, c: (rmap(b, c), q_blk)),
                  pl.BlockSpec((L, qk_w), lambda b, c: (rmap(b, c), k_blk)),
                  pl.BlockSpec((L, D_MLSTM), lambda b, c: (rmap(b, c), v_blk)),
                  pl.BlockSpec((L, D_MLSTM), lambda b, c: (rmap(b, c), o_blk)),
                  pl.BlockSpec((L, LANES), lambda b, c: (rmap(b, c), 0)),
                  pl.BlockSpec((1, LANES), lambda b, c: (0, 0)),
                  pl.BlockSpec((1, D_MLSTM), lambda b, c: (0, 0))],
        out_specs=pl.BlockSpec((L, D_MLSTM), lambda b, c: (rmap(b, c), 0)),
        scratch_shapes=[pltpu.VMEM((MLSTM_HEADS, MLSTM_DQK, MLSTM_DV), F32),
                        pltpu.VMEM((MLSTM_HEADS, 1, MLSTM_DQK), F32),
                        pltpu.VMEM((MLSTM_HEADS, 8, LANES), F32)],
        compiler_params=_cparams(("parallel", "arbitrary")),
        name="mlstm",
    )(proj, proj, proj, proj, gate_pre, gate_b, out_g)


def _out_proj_kernel(ya_ref, yb_ref, wa_ref, wb_ref, x_ref, o_ref):
    acc = jnp.dot(ya_ref[...], wa_ref[...], preferred_element_type=F32)
    acc = acc + jnp.dot(yb_ref[...], wb_ref[...], preferred_element_type=F32)
    o_ref[...] = x_ref[...] + acc


def _out_proj(y_a, y_b, w_out, x2, tm=1024, tn=1024):
    n = x2.shape[0]
    return pl.pallas_call(
        _out_proj_kernel,
        out_shape=jax.ShapeDtypeStruct((n, D_MODEL), F32),
        grid=(n // tm, D_MODEL // tn),
        in_specs=[pl.BlockSpec((tm, D_GMLP), lambda i, j: (i, 0)),
                  pl.BlockSpec((tm, D_MLSTM), lambda i, j: (i, 0)),
                  pl.BlockSpec((D_GMLP, tn), lambda i, j: (0, j)),
                  pl.BlockSpec((D_MLSTM, tn), lambda i, j: (1, j)),
                  pl.BlockSpec((tm, tn), lambda i, j: (i, j))],
        out_specs=pl.BlockSpec((tm, tn), lambda i, j: (i, j)),
        compiler_params=_cparams(("parallel", "arbitrary")),
        name="out_proj",
    )(y_a, y_b, w_out, w_out, x2)


def _router_kernel(x_ref, g_ref, wr_ref, br_ref, xp_ref, meta_ref, wts_ref, cnt_ref, carry_sc, w2_sc,
                   *, tm):
    i = pl.program_id(0)

    @pl.when(i == 0)
    def _():
        carry_sc[...] = jnp.zeros_like(carry_sc)
        w_hi = wr_ref[...].astype(BF16)
        w2_sc[:, :LANES] = w_hi
        w2_sc[:, LANES:] = (wr_ref[...] - w_hi.astype(F32)).astype(BF16)

    xn = _rms(x_ref[...], g_ref[...])
    xp_ref[...] = xn

    x_hi = xn.astype(BF16)
    x_lo = (xn - x_hi.astype(F32)).astype(BF16)
    both = jnp.dot(x_hi, w2_sc[...], preferred_element_type=F32)
    logits = (both[:, :LANES] + both[:, LANES:]
              + jnp.dot(x_lo, w2_sc[:, :LANES], preferred_element_type=F32)) + br_ref[...]
    lane = lax.broadcasted_iota(I32, (tm, LANES), 1)
    lane_f = lane.astype(F32)
    big = float(LANES)
    is_g = (lane >= N_EXPERTS) & (lane < N_EXPERTS + N_GROUPS)
    gl = jnp.where(is_g, logits, -jnp.inf)
    gmax = jnp.max(gl, axis=-1, keepdims=True)
    g_lane = jnp.min(jnp.where(gl == gmax, lane_f, big), axis=-1, keepdims=True)
    g_sum = jnp.sum(jnp.where(is_g, jnp.exp(gl - gmax), 0.0), axis=-1, keepdims=True)
    g_prob = 1.0 / g_sum
    g_idx = g_lane.astype(I32) - N_EXPERTS
    in_grp = (lane < N_EXPERTS) & (lax.shift_right_logical(lane, 3) == g_idx)
    el = jnp.where(in_grp, logits, -jnp.inf)
    t1 = jnp.max(el, axis=-1, keepdims=True)
    i1 = jnp.min(jnp.where(el == t1, lane_f, big), axis=-1, keepdims=True)
    el2 = jnp.where(lane_f == i1, -jnp.inf, el)
    t2 = jnp.max(el2, axis=-1, keepdims=True)
    i2 = jnp.min(jnp.where(el2 == t2, lane_f, big), axis=-1, keepdims=True)
    e2 = jnp.exp(t2 - t1)
    w1 = g_prob * (1.0 / (1.0 + e2))
    w2 = g_prob * (e2 / (1.0 + e2))

    sel1 = lane_f == i1
    sel2 = lane_f == i2
    onehot = jnp.where(sel1 | sel2, 1.0, 0.0)
    r = lax.broadcasted_iota(I32, (tm, tm), 0)
    c = lax.broadcasted_iota(I32, (tm, tm), 1)
    strict = jnp.where(c < r, 1.0, 0.0).astype(BF16)
    prefix = jnp.dot(strict, onehot.astype(BF16), preferred_element_type=F32) + carry_sc[0:1, :]
    rank1 = jnp.sum(jnp.where(sel1, prefix, 0.0), axis=-1, keepdims=True)
    rank2 = jnp.sum(jnp.where(sel2, prefix, 0.0), axis=-1, keepdims=True)
    total = carry_sc[0:1, :] + jnp.sum(onehot, axis=0, keepdims=True)
    carry_sc[...] = jnp.broadcast_to(total, carry_sc.shape)
    cnt_ref[...] = jnp.broadcast_to(total, cnt_ref.shape).astype(I32)

    meta = jnp.where(lane == 0, i1, jnp.where(lane == 1, i2,
                     jnp.where(lane == 2, rank1, jnp.where(lane == 3, rank2, 0.0))))
    meta_ref[...] = meta.astype(I32)
    wts_ref[...] = jnp.where(lane == 0, w1, jnp.where(lane == 1, w2, 0.0))


def _router(x1, g, wr, br, tm=512):
    n = x1.shape[0]
    return pl.pallas_call(
        functools.partial(_router_kernel, tm=tm),
        out_shape=(jax.ShapeDtypeStruct((n, D_MODEL), F32),
                   jax.ShapeDtypeStruct((n, LANES), I32),
                   jax.ShapeDtypeStruct((n, LANES), F32),
                   jax.ShapeDtypeStruct((8, LANES), I32)),
        grid=(n // tm,),
        in_specs=[pl.BlockSpec((tm, D_MODEL), lambda i: (i, 0)),
                  pl.BlockSpec((1, D_MODEL), lambda i: (0, 0)),
                  pl.BlockSpec((D_MODEL, LANES), lambda i: (0, 0)),
                  pl.BlockSpec((1, LANES), lambda i: (0, 0))],
        out_specs=(pl.BlockSpec((tm, D_MODEL), lambda i: (i, 0)),
                   pl.BlockSpec((tm, LANES), lambda i: (i, 0)),
                   pl.BlockSpec((tm, LANES), lambda i: (i, 0)),
                   pl.BlockSpec((8, LANES), lambda i: (0, 0))),
        scratch_shapes=[pltpu.VMEM((8, LANES), F32),
                        pltpu.VMEM((D_MODEL, 2 * LANES), BF16)],
        compiler_params=_cparams(("arbitrary",)),
        name="router",
    )(x1, g, wr, br)


def _dispatch_kernel(e_ref, rk_ref, row0_ref, cnt_ref, misc_ref, xp_ref, xs_ref, zbuf, sem, zsem, *, tm):
    base = pl.program_id(0) * (tm * TOP_K)

    def zero_fills(wait):
        def pad_fill(e, carry):
            cnt = cnt_ref[e]
            start = row0_ref[e] + cnt
            head = (-cnt) & (SUBLANES - 1)
            for j in range(SUBLANES - 1):
                @pl.when(j < head)
                def _(j=j):
                    cp = pltpu.make_async_copy(zbuf.at[pl.ds(0, 1), :], xs_ref.at[pl.ds(start + j, 1), :], zsem)
                    cp.wait() if wait else cp.start()
            off = start + head
            rest = (-(cnt + head)) & (R_SUB - 1)
            bit = R_SUB // 2
            while bit >= SUBLANES:
                @pl.when((rest & bit) != 0)
                def _(off=off, bit=bit):
                    dst = xs_ref.at[pl.ds(pl.multiple_of(off, SUBLANES), bit), :]
                    cp = pltpu.make_async_copy(zbuf.at[pl.ds(0, bit), :], dst, zsem)
                    cp.wait() if wait else cp.start()
                off = off + (rest & bit)
                bit //= 2
            return carry

        def tail_fill(sb, carry):
            r = pl.multiple_of(sb * R_SUB, R_SUB)
            cp = pltpu.make_async_copy(zbuf, xs_ref.at[pl.ds(r, R_SUB), :], zsem)
            cp.wait() if wait else cp.start()
            return carry

        lax.fori_loop(0, N_EXPERTS, pad_fill, 0)
        lax.fori_loop(misc_ref[0], NSB_MAX, tail_fill, 0)

    @pl.when(pl.program_id(0) == 0)
    def _():
        zbuf[...] = jnp.zeros_like(zbuf)
        zero_fills(False)

    def row_copy(r, kk):
        a = base + TOP_K * r + kk
        p = row0_ref[e_ref[a]] + rk_ref[a]
        return pltpu.make_async_copy(xp_ref.at[pl.ds(r, 1), :], xs_ref.at[pl.ds(p, 1), :], sem)

    def issue(r, carry):
        for kk in range(TOP_K):
            row_copy(r, kk).start()
        return carry

    lax.fori_loop(0, tm, issue, 0, unroll=8)
    for _ in range(TOP_K):
        pltpu.make_async_copy(xp_ref, xs_ref.at[pl.ds(0, tm), :], sem).wait()

    @pl.when(pl.program_id(0) == pl.num_programs(0) - 1)
    def _():
        zero_fills(True)


def _dispatch(e_flat, rk_flat, row0, cnt, misc, xp, tm=512):
    n = xp.shape[0]
    return pl.pallas_call(
        functools.partial(_dispatch_kernel, tm=tm),
        out_shape=jax.ShapeDtypeStruct((NSB_MAX * R_SUB, D_MODEL), F32),
        grid_spec=pltpu.PrefetchScalarGridSpec(
            num_scalar_prefetch=5,
            grid=(n // tm,),
            in_specs=[pl.BlockSpec((tm, D_MODEL), lambda i, *_: (i, 0))],
            out_specs=pl.BlockSpec(memory_space=pl.ANY),
            scratch_shapes=[pltpu.VMEM((R_SUB, D_MODEL), F32),
                            pltpu.SemaphoreType.DMA(()),
                            pltpu.SemaphoreType.DMA(())]),
        compiler_params=_cparams(("arbitrary",)),
        name="dispatch",
    )(e_flat, rk_flat, row0, cnt, misc, xp)


def _expert_kernel(ee_ref, sb0_ref, nsb_ref, misc_ref, xs_hbm, wg_hbm, wu_hbm, wd_hbm, os_hbm,
                   xbuf, hbuf, obuf, zbuf, gbuf, ubuf, dbuf, xsem, osem, zsem, gsem, dsem):
    s = pl.program_id(0)
    n_ent = pl.num_programs(0)
    nsb = nsb_ref[s]
    nxt = jnp.minimum(s + 1, n_ent - 1)
    has_next = (s + 1 < n_ent) & (nsb_ref[nxt] > 0)

    def sub_row(ent, sb):
        return pl.multiple_of((sb0_ref[ent] + sb) * R_SUB, R_SUB)

    def x_copy(ent, sb, wait):
        cp = pltpu.make_async_copy(xs_hbm.at[pl.ds(sub_row(ent, sb), R_SUB), :], xbuf.at[sb], xsem.at[sb])
        cp.wait() if wait else cp.start()

    def x_copies(ent, wait):
        for sb in range(N_SUB):
            @pl.when(sb < nsb_ref[ent])
            def _(sb=sb):
                x_copy(ent, sb, wait)

    def gu_slot(ent, c):
        return lax.rem(ent * N_A + c, GU_SLOTS)

    def gu_copies(ent, c, wait):
        e, sl = ee_ref[ent], gu_slot(ent, c)
        for w_hbm, buf in ((wg_hbm, gbuf), (wu_hbm, ubuf)):
            cp = pltpu.make_async_copy(w_hbm.at[e, :, pl.ds(c * GU_CHUNK, GU_CHUNK)], buf.at[sl], gsem.at[sl])
            cp.wait() if wait else cp.start()

    def dn_copy(ent, nb, wait):
        cp = pltpu.make_async_copy(wd_hbm.at[ee_ref[ent], :, pl.ds(nb * DN_CHUNK, DN_CHUNK)],
                                   dbuf.at[nb], dsem.at[nb])
        cp.wait() if wait else cp.start()

    def o_copy(ent, sb, nb, wait):
        cp = pltpu.make_async_copy(
            obuf.at[nb % 2, sb],
            os_hbm.at[pl.ds(sub_row(ent, sb), R_SUB), pl.ds(nb * DN_CHUNK, DN_CHUNK)], osem.at[nb % 2])
        cp.wait() if wait else cp.start()

    def o_waits(ent, nb):
        for sb in range(N_SUB):
            @pl.when(sb < nsb_ref[ent])
            def _(sb=sb):
                o_copy(ent, sb, nb, True)

    def tail_fill(wait):
        def fill(sb, carry):
            r = pl.multiple_of(sb * R_SUB, R_SUB)
            for nb in range(N_B):
                cp = pltpu.make_async_copy(
                    zbuf, os_hbm.at[pl.ds(r, R_SUB), pl.ds(nb * DN_CHUNK, DN_CHUNK)], zsem)
                cp.wait() if wait else cp.start()
            return carry

        lax.fori_loop(misc_ref[0], NSB_MAX, fill, 0)

    @pl.when(s == 0)
    def _():
        zbuf[...] = jnp.zeros_like(zbuf)
        tail_fill(False)
        x_copies(0, False)
        for c in range(N_A):
            gu_copies(0, c, False)
        for nb in range(N_B):
            dn_copy(0, nb, False)

    @pl.when(nsb == 0)
    def _():
        @pl.when(s >= 1)
        def _():
            for nb in range(N_B - 2, N_B):
                o_waits(s - 1, nb)

    @pl.when(nsb > 0)
    def _():
        x_copies(s, True)

        for c in range(N_A):
            gu_copies(s, c, True)

            @pl.when(has_next)
            def _(c=c):
                gu_copies(nxt, c, False)
            for sb in range(N_SUB):
                @pl.when(sb < nsb)
                def _(sb=sb, c=c):
                    xb = xbuf[sb].astype(BF16)
                    sl = gu_slot(s, c)
                    gate = jnp.dot(xb, gbuf[sl].astype(BF16), preferred_element_type=F32)
                    up = jnp.dot(xb, ubuf[sl].astype(BF16), preferred_element_type=F32)
                    hbuf[c, sb] = (jax.nn.silu(gate) * up).astype(BF16)

                if c == N_A - 1:
                    @pl.when(has_next & (sb < nsb_ref[nxt]))
                    def _(sb=sb):
                        x_copy(nxt, sb, False)

        for nb in range(N_B):
            if nb >= 2:
                o_waits(s, nb - 2)
            else:
                @pl.when(s >= 1)
                def _(nb=nb):
                    o_waits(s - 1, nb + N_B - 2)
            dn_copy(s, nb, True)
            for sb in range(N_SUB):
                @pl.when(sb < nsb)
                def _(sb=sb, nb=nb):
                    wd = dbuf.at[nb]
                    acc = jnp.dot(hbuf[0, sb], wd[0:GU_CHUNK, :].astype(BF16), preferred_element_type=F32)
                    for ca in range(1, N_A):
                        acc = acc + jnp.dot(hbuf[ca, sb],
                                            wd[ca * GU_CHUNK:(ca + 1) * GU_CHUNK, :].astype(BF16),
                                            preferred_element_type=F32)
                    obuf[nb % 2, sb] = acc
                    o_copy(s, sb, nb, False)

            @pl.when(has_next)
            def _(nb=nb):
                dn_copy(nxt, nb, False)

    @pl.when(s == n_ent - 1)
    def _():
        @pl.when(nsb > 0)
        def _():
            for nb in range(N_B - 2, N_B):
                o_waits(s, nb)
        tail_fill(True)


def _experts(ent_e, ent_sb0, ent_nsb, misc, xs, w_gate, w_up, w_down):
    assert N_B % 2 == 0, "output column chunks alternate between two staging sets"
    any_spec = pl.BlockSpec(memory_space=pl.ANY)
    return pl.pallas_call(
        _expert_kernel,
        out_shape=jax.ShapeDtypeStruct((NSB_MAX * R_SUB, D_MODEL), F32),
        grid_spec=pltpu.PrefetchScalarGridSpec(
            num_scalar_prefetch=4,
            grid=(NE_MAX,),
            in_specs=[any_spec, any_spec, any_spec, any_spec],
            out_specs=any_spec,
            scratch_shapes=[pltpu.VMEM((N_SUB, R_SUB, D_MODEL), F32),
                            pltpu.VMEM((N_A, N_SUB, R_SUB, GU_CHUNK), BF16),
                            pltpu.VMEM((2, N_SUB, R_SUB, DN_CHUNK), F32),
                            pltpu.VMEM((R_SUB, DN_CHUNK), F32),
                            pltpu.VMEM((GU_SLOTS, D_MODEL, GU_CHUNK), F32),
                            pltpu.VMEM((GU_SLOTS, D_MODEL, GU_CHUNK), F32),
                            pltpu.VMEM((N_B, D_EXPERT, DN_CHUNK), F32),
                            pltpu.SemaphoreType.DMA((N_SUB,)),
                            pltpu.SemaphoreType.DMA((2,)),
                            pltpu.SemaphoreType.DMA(()),
                            pltpu.SemaphoreType.DMA((GU_SLOTS,)),
                            pltpu.SemaphoreType.DMA((N_B,))]),
        compiler_params=_cparams(("arbitrary",)),
        name="experts",
    )(ent_e, ent_sb0, ent_nsb, misc, xs, w_gate, w_up, w_down)


def _combine_kernel(e_ref, rk_ref, row0_ref, x_ref, wts_ref, g_ref, os_ref, y_ref, gbuf, sem, *, tm):
    i = pl.program_id(0)
    slot = lax.rem(i, 2)

    def issue_tile(tile, slot_):
        base = tile * (tm * TOP_K)

        def issue(r, carry):
            for kk in range(TOP_K):
                a = base + TOP_K * r + kk
                p = row0_ref[e_ref[a]] + rk_ref[a]
                pltpu.make_async_copy(os_ref.at[pl.ds(p, 1), :], gbuf.at[slot_, kk, pl.ds(r, 1), :],
                                      sem.at[slot_]).start()
            return carry

        lax.fori_loop(0, tm, issue, 0, unroll=8)

    @pl.when(i == 0)
    def _():
        issue_tile(0, 0)

    @pl.when(i + 1 < pl.num_programs(0))
    def _():
        issue_tile(i + 1, 1 - slot)

    for kk in range(TOP_K):
        pltpu.make_async_copy(os_ref.at[pl.ds(0, tm), :], gbuf.at[slot, kk], sem.at[slot]).wait()
    moe = wts_ref[:, 0:1] * gbuf[slot, 0] + wts_ref[:, 1:2] * gbuf[slot, 1]
    y_ref[...] = _rms(x_ref[...] + moe, g_ref[...])


def _combine(e_flat, rk_flat, row0, x1, wts, g, out_sorted, tm=256):
    n = x1.shape[0]
    return pl.pallas_call(
        functools.partial(_combine_kernel, tm=tm),
        out_shape=jax.ShapeDtypeStruct((n, D_MODEL), F32),
        grid_spec=pltpu.PrefetchScalarGridSpec(
            num_scalar_prefetch=3,
            grid=(n // tm,),
            in_specs=[pl.BlockSpec((tm, D_MODEL), lambda i, *_: (i, 0)),
                      pl.BlockSpec((tm, LANES), lambda i, *_: (i, 0)),
                      pl.BlockSpec((1, D_MODEL), lambda i, *_: (0, 0)),
                      pl.BlockSpec(memory_space=pl.ANY)],
            out_specs=pl.BlockSpec((tm, D_MODEL), lambda i, *_: (i, 0)),
            scratch_shapes=[pltpu.VMEM((2, TOP_K, tm, D_MODEL), F32),
                            pltpu.SemaphoreType.DMA((2,))]),
        compiler_params=_cparams(("arbitrary",)),
        name="combine",
    )(e_flat, rk_flat, row0, x1, wts, g, out_sorted)


def _moe(x1, norm_g, wr_g, br_g, wr_e, br_e, w_gate, w_up, w_down, final_g):
    n = x1.shape[0]
    pad = LANES - N_EXPERTS - N_GROUPS
    wr = jnp.concatenate([wr_e, wr_g, jnp.zeros((D_MODEL, pad), F32)], axis=1)
    br = jnp.concatenate([br_e, br_g, jnp.zeros((pad,), F32)])[None]
    xp, meta, wts, counts = _router(x1, norm_g[None], wr, br)

    assert n * TOP_K == N_ASSIGN
    cnt = counts[0, :N_EXPERTS]
    nsb = (cnt + R_SUB - 1) // R_SUB
    sb_end = jnp.cumsum(nsb)
    sb_start = sb_end - nsb
    nent = (nsb + N_SUB - 1) // N_SUB
    ent_end = jnp.cumsum(nent)
    ent_start = ent_end - nent
    ids = jnp.arange(NE_MAX, dtype=I32)
    valid = ids < ent_end[-1]
    ent_e = jnp.minimum(jnp.sum(ids[:, None] >= ent_end[None, :], axis=1), N_EXPERTS - 1)
    ent_e = jnp.where(valid, ent_e, ent_e[ent_end[-1] - 1])
    within = (ids - ent_start[ent_e]) * N_SUB
    ent_sb0 = jnp.where(valid, sb_start[ent_e] + within, 0)
    ent_nsb = jnp.where(valid, jnp.clip(nsb[ent_e] - within, 0, N_SUB), 0)
    row0 = (sb_start * R_SUB).astype(I32)
    misc = sb_end[-1:].astype(I32)
    e_flat = meta[:, 0:TOP_K].reshape(-1)
    rk_flat = meta[:, TOP_K:2 * TOP_K].reshape(-1)

    xs = _dispatch(e_flat, rk_flat, row0, cnt, misc, xp)
    out_sorted = _experts(ent_e.astype(I32), ent_sb0.astype(I32), ent_nsb.astype(I32), misc, xs,
                          w_gate, w_up, w_down)
    return _combine(e_flat, rk_flat, row0, x1, wts, final_g[None], out_sorted)


def kernel(x, norm_mix_g, w_in, gmlp_v_norm_g, gmlp_spatial_w, gmlp_spatial_b, mlstm_igate_b,
           mlstm_fgate_b, mlstm_out_norm_g, w_out, norm_ffn_g, router_group_w, router_group_b,
           router_expert_w, router_expert_b, expert_w_gate, expert_w_up, expert_w_down, final_norm_g):
    batch, seq, d = x.shape
    n = batch * seq
    assert w_in.shape[0] == 1, "the final rmsnorm is fused after the single layer's MoE"
    l = 0
    x2 = x.reshape(n, d)
    proj, gate_pre = _in_proj(x2, norm_mix_g[l][None], w_in[l].T)
    y_a = _gmlp(proj, gmlp_v_norm_g[l][None], gmlp_spatial_w[l], gmlp_spatial_b[l].T)
    gate_b = jnp.concatenate([mlstm_igate_b[l], mlstm_fgate_b[l],
                              jnp.zeros((LANES - N_GATE_COLS,), F32)])[None]
    y_b = _mlstm(proj, gate_pre, gate_b, mlstm_out_norm_g[l][None], batch, seq)
    x1 = _out_proj(y_a, y_b, w_out[l].astype(BF16), x2)
    out = _moe(x1, norm_ffn_g[l], router_group_w[l], router_group_b[l], router_expert_w[l],
               router_expert_b[l], expert_w_gate[l], expert_w_up[l], expert_w_down[l], final_norm_g)
    return out.reshape(batch, seq, d)
```

```python
import functools

import jax
import jax.numpy as jnp
from jax import lax
from jax.experimental import pallas as pl
from jax.experimental.pallas import tpu as pltpu

F32 = jnp.float32
BF16 = jnp.bfloat16
I32 = jnp.int32

D_MODEL = 4096
D_GMLP = 2048
GMLP_CHUNK = 128
GMLP_GROUPS = 16
D_MLSTM = 2048
MLSTM_HEADS = 4
MLSTM_DV = 512
MLSTM_DQK = 256
GATE_SOFTCAP = 15.0
N_GROUPS = 8
EXPERTS_PER_GROUP = 8
N_EXPERTS = 64
TOP_K = 2
D_EXPERT = 512
EPS = 1e-6
D_PROJ_MAIN = 10240
N_GATE_COLS = 8

LANES = 128
SUBLANES = 8
VMEM_LIMIT = 56 * 1024 * 1024
MLSTM_L = 256
N_ASSIGN = 4 * 4096 * TOP_K
R_SUB = 192
N_SUB = 4
NSB_MAX = -(-(N_ASSIGN + N_EXPERTS * (R_SUB - 1)) // R_SUB)
NE_MAX = N_EXPERTS + N_ASSIGN // (R_SUB * N_SUB)
GU_CHUNK = 256
N_A = D_EXPERT // GU_CHUNK
DN_CHUNK = 1024
N_B = D_MODEL // DN_CHUNK
GU_SLOTS = N_A + 1


def _cparams(sem):
    return pltpu.CompilerParams(dimension_semantics=sem, vmem_limit_bytes=VMEM_LIMIT)


def _rms(x, g):
    return x * lax.rsqrt(jnp.mean(x * x, axis=-1, keepdims=True) + EPS) * g


def _in_proj_kernel(x_hbm, g_ref, w_ref, wg_ref, proj_ref, gate_ref, x_buf, xn_sc, sem, *, tm, strip):
    i = pl.program_id(0)
    nt = (((1,), (1,)), ((), ()))

    def x_copy(tile):
        return pltpu.make_async_copy(x_hbm.at[pl.ds(pl.multiple_of(tile * tm, tm), tm), :], x_buf, sem)

    @pl.when(pl.program_id(1) == 0)
    def _():
        @pl.when(i == 0)
        def _():
            x_copy(0).start()

        x_copy(i).wait()
        for r in range(0, tm, strip):
            rows = slice(r, r + strip)
            xn_sc[rows, :] = _rms(x_buf[rows, :], g_ref[...]).astype(BF16)

        @pl.when(i + 1 < pl.num_programs(0))
        def _():
            x_copy(i + 1).start()

        wg = jnp.concatenate([wg_ref[...].astype(BF16),
                              jnp.zeros((LANES - N_GATE_COLS, D_MODEL), BF16)], axis=0)
        gate_ref[...] = lax.dot_general(xn_sc[...], wg, nt, preferred_element_type=F32)

    proj_ref[...] = lax.dot_general(xn_sc[...], w_ref[...].astype(BF16), nt,
                                    preferred_element_type=F32).astype(BF16)


def _in_proj(x2, g, w_t, tm=1024, tn=512, strip=32):
    n = x2.shape[0]
    return pl.pallas_call(
        functools.partial(_in_proj_kernel, tm=tm, strip=strip),
        out_shape=(jax.ShapeDtypeStruct((n, D_PROJ_MAIN), BF16),
                   jax.ShapeDtypeStruct((n, LANES), F32)),
        grid=(n // tm, D_PROJ_MAIN // tn),
        in_specs=[pl.BlockSpec(memory_space=pl.ANY),
                  pl.BlockSpec((1, D_MODEL), lambda i, j: (0, 0)),
                  pl.BlockSpec((tn, D_MODEL), lambda i, j: (j, 0)),
                  pl.BlockSpec((N_GATE_COLS, D_MODEL), lambda i, j: (D_PROJ_MAIN // N_GATE_COLS, 0))],
        out_specs=(pl.BlockSpec((tm, tn), lambda i, j: (i, j)),
                   pl.BlockSpec((tm, LANES), lambda i, j: (i, 0))),
        scratch_shapes=[pltpu.VMEM((tm, D_MODEL), F32),
                        pltpu.VMEM((tm, D_MODEL), BF16),
                        pltpu.SemaphoreType.DMA(())],
        compiler_params=_cparams(("arbitrary", "arbitrary")),
        name="in_proj",
    )(x2, g, w_t, w_t)


def _gmlp_kernel(u_ref, v_ref, gv_ref, w_ref, bt_ref, y_ref, wm_sc, *, rows):
    c = GMLP_CHUNK

    @pl.when(pl.program_id(0) == 0)
    def _():
        r = lax.broadcasted_iota(I32, (c, c), 0)
        s = lax.broadcasted_iota(I32, (c, c), 1)
        for g in range(GMLP_GROUPS):
            wm_sc[g] = jnp.where(s <= r, w_ref[g], 0.0).astype(BF16)

    for ci in range(rows // c):
        rs = slice(ci * c, (ci + 1) * c)
        gu = jax.nn.gelu(u_ref[rs, :].astype(F32))
        gv = jax.nn.gelu(v_ref[rs, :].astype(F32))
        vn = _rms(gv, gv_ref[...]).astype(BF16)
        for g in range(GMLP_GROUPS):
            cs = slice(g * c, (g + 1) * c)
            z = jnp.dot(wm_sc[g], vn[:, cs], preferred_element_type=F32) + bt_ref[:, g:g + 1]
            y_ref[rs, cs] = (gu[:, cs] * z).astype(BF16)


def _gmlp(proj, gv, w_s, b_t, rows=512):
    n = proj.shape[0]
    return pl.pallas_call(
        functools.partial(_gmlp_kernel, rows=rows),
        out_shape=jax.ShapeDtypeStruct((n, D_GMLP), BF16),
        grid=(n // rows,),
        in_specs=[pl.BlockSpec((rows, D_GMLP), lambda i: (i, 0)),
                  pl.BlockSpec((rows, D_GMLP), lambda i: (i, 1)),
                  pl.BlockSpec((1, D_GMLP), lambda i: (0, 0)),
                  pl.BlockSpec((GMLP_GROUPS, GMLP_CHUNK, GMLP_CHUNK), lambda i: (0, 0, 0)),
                  pl.BlockSpec((GMLP_CHUNK, GMLP_GROUPS), lambda i: (0, 0))],
        out_specs=pl.BlockSpec((rows, D_GMLP), lambda i: (i, 0)),
        scratch_shapes=[pltpu.VMEM((GMLP_GROUPS, GMLP_CHUNK, GMLP_CHUNK), BF16)],
        compiler_params=_cparams(("arbitrary",)),
        name="gmlp",
    )(proj, proj, gv, w_s, b_t)


def _mlstm_kernel(q_ref, k_ref, v_ref, o_ref, gp_ref, gb_ref, og_ref, y_ref, c_sc, n_sc, m_sc):
    L, H, DQK, DV = MLSTM_L, MLSTM_HEADS, MLSTM_DQK, MLSTM_DV
    scale = DQK ** -0.5

    @pl.when(pl.program_id(1) == 0)
    def _():
        c_sc[...] = jnp.zeros_like(c_sc)
        n_sc[...] = jnp.zeros_like(n_sc)
        m_sc[...] = jnp.zeros_like(m_sc)

    lane = lax.broadcasted_iota(I32, (L, LANES), 1)
    z = gp_ref[...] + gb_ref[...]
    sc = GATE_SOFTCAP * jnp.tanh(z / GATE_SOFTCAP)
    logf = -(jnp.maximum(-sc, 0.0) + jnp.log1p(jnp.exp(-jnp.abs(sc))))
    row = lax.broadcasted_iota(I32, (L, L), 0)
    col = lax.broadcasted_iota(I32, (L, L), 1)
    causal = col <= row
    tri = jnp.where(causal, 1.0, 0.0).astype(BF16)
    bcum, rest = None, logf
    for _ in range(3):
        part = rest.astype(BF16)
        rest = rest - part.astype(F32)
        term = jnp.dot(tri, part, preferred_element_type=F32)
        bcum = term if bcum is None else bcum + term
    gcol = jnp.where(lane < H, sc, bcum)
    grow = gcol.T

    for h in range(H):
        ig_col, b_col = gcol[:, h:h + 1], gcol[:, H + h:H + h + 1]
        ig_row, b_row = grow[h:h + 1, :], grow[H + h:H + h + 1, :]
        m = m_sc[h, 0:1, 0:1]
        q = q_ref[:, h * DQK:(h + 1) * DQK]
        k = k_ref[:, h * DQK:(h + 1) * DQK]
        v = v_ref[:, h * DV:(h + 1) * DV]

        d_log = jnp.where(causal, b_col - b_row + ig_row, -jnp.inf)
        inter = b_col + m
        m_row = jnp.maximum(inter, jnp.max(d_log, axis=-1, keepdims=True))
        w_intra = jnp.exp(d_log - m_row)
        w_inter = jnp.exp(inter - m_row)
        s = lax.dot_general(q, k, (((1,), (1,)), ((), ())), preferred_element_type=F32)
        s = s * scale * w_intra
        c_old = c_sc[h]
        n_old = n_sc[h]
        num = (w_inter * jnp.dot(q, c_old.astype(BF16), preferred_element_type=F32)
               + jnp.dot(s.astype(BF16), v, preferred_element_type=F32))
        qn = jnp.sum(q.astype(F32) * n_old, axis=-1, keepdims=True)
        den = w_inter * qn + jnp.sum(s, axis=-1, keepdims=True)
        hh = num / jnp.maximum(jnp.abs(den), jnp.exp(-m_row))

        b_last = b_row[:, L - 1:L]
        tail = b_last - b_col + ig_col
        m_new = jnp.maximum(b_last + m, jnp.max(tail, axis=0, keepdims=True))
        w_tail = jnp.exp(tail - m_new)
        decay = jnp.exp(b_last + m - m_new)
        kw = k.astype(F32) * scale * w_tail
        c_sc[h] = decay * c_old + lax.dot_general(
            kw.astype(BF16), v, (((0,), (0,)), ((), ())), preferred_element_type=F32)
        n_sc[h] = decay * n_old + jnp.sum(kw, axis=0, keepdims=True)
        m_sc[h] = jnp.broadcast_to(m_new, m_sc.shape[1:])

        hn = hh * lax.rsqrt(jnp.mean(hh * hh, axis=-1, keepdims=True) + EPS)
        hn = hn * og_ref[:, h * DV:(h + 1) * DV]
        hn = hn * jax.nn.sigmoid(o_ref[:, h * DV:(h + 1) * DV].astype(F32))
        y_ref[:, h * DV:(h + 1) * DV] = hn.astype(BF16)


def _mlstm(proj, gate_pre, gate_b, out_g, batch, seq):
    n = proj.shape[0]
    L = MLSTM_L
    nc = seq // L
    qk_w = MLSTM_HEADS * MLSTM_DQK
    q_blk, k_blk = 2 * D_GMLP // qk_w, 2 * D_GMLP // qk_w + 1
    v_blk, o_blk = (2 * D_GMLP + 2 * qk_w) // D_MLSTM, (2 * D_GMLP + 2 * qk_w) // D_MLSTM + 1
    rmap = lambda b, c: b * nc + c
    return pl.pallas_call(
        _mlstm_kernel,
        out_shape=jax.ShapeDtypeStruct((n, D_MLSTM), BF16),
        grid=(batch, nc),
        in_specs=[pl.BlockSpec((L, qk_w), lambda b, c: (rmap(b, c), q_blk)),
                  pl.BlockSpec((L, qk_w), lambda b, c: (rmap(b, c), k_blk)),
                  pl.BlockSpec((L, D_MLSTM), lambda b, c: (rmap(b, c), v_blk)),
                  pl.BlockSpec((L, D_MLSTM), lambda b, c: (rmap(b, c), o_blk)),
                  pl.BlockSpec((L, LANES), lambda b, c: (rmap(b, c), 0)),
                  pl.BlockSpec((1, LANES), lambda b, c: (0, 0)),
                  pl.BlockSpec((1, D_MLSTM), lambda b, c: (0, 0))],
        out_specs=pl.BlockSpec((L, D_MLSTM), lambda b, c: (rmap(b, c), 0)),
        scratch_shapes=[pltpu.VMEM((MLSTM_HEADS, MLSTM_DQK, MLSTM_DV), F32),
                        pltpu.VMEM((MLSTM_HEADS, 1, MLSTM_DQK), F32),
                        pltpu.VMEM((MLSTM_HEADS, 8, LANES), F32)],
        compiler_params=_cparams(("parallel", "arbitrary")),
        name="mlstm",
    )(proj, proj, proj, proj, gate_pre, gate_b, out_g)


def _out_proj_kernel(ya_ref, yb_ref, wa_ref, wb_ref, x_ref, o_ref):
    acc = jnp.dot(ya_ref[...], wa_ref[...], preferred_element_type=F32)
    acc = acc + jnp.dot(yb_ref[...], wb_ref[...], preferred_element_type=F32)
    o_ref[...] = x_ref[...] + acc


def _out_proj(y_a, y_b, w_out, x2, tm=1024, tn=1024):
    n = x2.shape[0]
    return pl.pallas_call(
        _out_proj_kernel,
        out_shape=jax.ShapeDtypeStruct((n, D_MODEL), F32),
        grid=(n // tm, D_MODEL // tn),
        in_specs=[pl.BlockSpec((tm, D_GMLP), lambda i, j: (i, 0)),
                  pl.BlockSpec((tm, D_MLSTM), lambda i, j: (i, 0)),
                  pl.BlockSpec((D_GMLP, tn), lambda i, j: (0, j)),
                  pl.BlockSpec((D_MLSTM, tn), lambda i, j: (1, j)),
                  pl.BlockSpec((tm, tn), lambda i, j: (i, j))],
        out_specs=pl.BlockSpec((tm, tn), lambda i, j: (i, j)),
        compiler_params=_cparams(("parallel", "arbitrary")),
        name="out_proj",
    )(y_a, y_b, w_out, w_out, x2)


def _router_kernel(x_ref, g_ref, wr_ref, br_ref, xp_ref, meta_ref, wts_ref, cnt_ref, carry_sc, w2_sc,
                   *, tm):
    i = pl.program_id(0)

    @pl.when(i == 0)
    def _():
        carry_sc[...] = jnp.zeros_like(carry_sc)
        w_hi = wr_ref[...].astype(BF16)
        w2_sc[:, :LANES] = w_hi
        w2_sc[:, LANES:] = (wr_ref[...] - w_hi.astype(F32)).astype(BF16)

    xn = _rms(x_ref[...], g_ref[...])
    xp_ref[...] = xn

    x_hi = xn.astype(BF16)
    x_lo = (xn - x_hi.astype(F32)).astype(BF16)
    both = jnp.dot(x_hi, w2_sc[...], preferred_element_type=F32)
    logits = (both[:, :LANES] + both[:, LANES:]
              + jnp.dot(x_lo, w2_sc[:, :LANES], preferred_element_type=F32)) + br_ref[...]
    lane = lax.broadcasted_iota(I32, (tm, LANES), 1)
    lane_f = lane.astype(F32)
    big = float(LANES)
    is_g = (lane >= N_EXPERTS) & (lane < N_EXPERTS + N_GROUPS)
    gl = jnp.where(is_g, logits, -jnp.inf)
    gmax = jnp.max(gl, axis=-1, keepdims=True)
    g_lane = jnp.min(jnp.where(gl == gmax, lane_f, big), axis=-1, keepdims=True)
    g_sum = jnp.sum(jnp.where(is_g, jnp.exp(gl - gmax), 0.0), axis=-1, keepdims=True)
    g_prob = 1.0 / g_sum
    g_idx = g_lane.astype(I32) - N_EXPERTS
    in_grp = (lane < N_EXPERTS) & (lax.shift_right_logical(lane, 3) == g_idx)
    el = jnp.where(in_grp, logits, -jnp.inf)
    t1 = jnp.max(el, axis=-1, keepdims=True)
    i1 = jnp.min(jnp.where(el == t1, lane_f, big), axis=-1, keepdims=True)
    el2 = jnp.where(lane_f == i1, -jnp.inf, el)
    t2 = jnp.max(el2, axis=-1, keepdims=True)
    i2 = jnp.min(jnp.where(el2 == t2, lane_f, big), axis=-1, keepdims=True)
    e2 = jnp.exp(t2 - t1)
    w1 = g_prob * (1.0 / (1.0 + e2))
    w2 = g_prob * (e2 / (1.0 + e2))

    sel1 = lane_f == i1
    sel2 = lane_f == i2
    onehot = jnp.where(sel1 | sel2, 1.0, 0.0)
    r = lax.broadcasted_iota(I32, (tm, tm), 0)
    c = lax.broadcasted_iota(I32, (tm, tm), 1)
    strict = jnp.where(c < r, 1.0, 0.0).astype(BF16)
    prefix = jnp.dot(strict, onehot.astype(BF16), preferred_element_type=F32) + carry_sc[0:1, :]
    rank1 = jnp.sum(jnp.where(sel1, prefix, 0.0), axis=-1, keepdims=True)
    rank2 = jnp.sum(jnp.where(sel2, prefix, 0.0), axis=-1, keepdims=True)
    total = carry_sc[0:1, :] + jnp.sum(onehot, axis=0, keepdims=True)
    carry_sc[...] = jnp.broadcast_to(total, carry_sc.shape)
    cnt_ref[...] = jnp.broadcast_to(total, cnt_ref.shape).astype(I32)

    meta = jnp.where(lane == 0, i1, jnp.where(lane == 1, i2,
                     jnp.where(lane == 2, rank1, jnp.where(lane == 3, rank2, 0.0))))
    meta_ref[...] = meta.astype(I32)
    wts_ref[...] = jnp.where(lane == 0, w1, jnp.where(lane == 1, w2, 0.0))


def _router(x1, g, wr, br, tm=512):
    n = x1.shape[0]
    return pl.pallas_call(
        functools.partial(_router_kernel, tm=tm),
        out_shape=(jax.ShapeDtypeStruct((n, D_MODEL), F32),
                   jax.ShapeDtypeStruct((n, LANES), I32),
                   jax.ShapeDtypeStruct((n, LANES), F32),
                   jax.ShapeDtypeStruct((8, LANES), I32)),
        grid=(n // tm,),
        in_specs=[pl.BlockSpec((tm, D_MODEL), lambda i: (i, 0)),
                  pl.BlockSpec((1, D_MODEL), lambda i: (0, 0)),
                  pl.BlockSpec((D_MODEL, LANES), lambda i: (0, 0)),
                  pl.BlockSpec((1, LANES), lambda i: (0, 0))],
        out_specs=(pl.BlockSpec((tm, D_MODEL), lambda i: (i, 0)),
                   pl.BlockSpec((tm, LANES), lambda i: (i, 0)),
                   pl.BlockSpec((tm, LANES), lambda i: (i, 0)),
                   pl.BlockSpec((8, LANES), lambda i: (0, 0))),
        scratch_shapes=[pltpu.VMEM((8, LANES), F32),
                        pltpu.VMEM((D_MODEL, 2 * LANES), BF16)],
        compiler_params=_cparams(("arbitrary",)),
        name="router",
    )(x1, g, wr, br)


def _dispatch_kernel(e_ref, rk_ref, row0_ref, cnt_ref, misc_ref, xp_ref, xs_ref, zbuf, sem, zsem, *, tm):
    base = pl.program_id(0) * (tm * TOP_K)

    @pl.when(pl.program_id(0) == 0)
    def _():
        zbuf[...] = jnp.zeros_like(zbuf)

        def pad_fill(e, carry, *, wait):
            cnt = cnt_ref[e]
            start = row0_ref[e] + cnt
            head = (-cnt) & (SUBLANES - 1)
            for j in range(SUBLANES - 1):
                @pl.when(j < head)
                def _(j=j):
                    cp = pltpu.make_async_copy(zbuf.at[pl.ds(0, 1), :], xs_ref.at[pl.ds(start + j, 1), :], zsem)
                    cp.wait() if wait else cp.start()
            off = start + head
            rest = lax.rem(R_SUB - lax.rem(cnt, R_SUB), R_SUB) - head
            bit = 1 << (R_SUB.bit_length() - 1)
            while bit >= SUBLANES:
                @pl.when((rest & bit) != 0)
                def _(off=off, bit=bit):
                    dst = xs_ref.at[pl.ds(pl.multiple_of(off, SUBLANES), bit), :]
                    cp = pltpu.make_async_copy(zbuf.at[pl.ds(0, bit), :], dst, zsem)
                    cp.wait() if wait else cp.start()
                off = off + (rest & bit)
                bit //= 2
            return carry

        def tail_fill(sb, carry, *, wait):
            r = pl.multiple_of(sb * R_SUB, R_SUB)
            cp = pltpu.make_async_copy(zbuf, xs_ref.at[pl.ds(r, R_SUB), :], zsem)
            cp.wait() if wait else cp.start()
            return carry

        for wait in (False, True):
            lax.fori_loop(0, N_EXPERTS, functools.partial(pad_fill, wait=wait), 0)
            lax.fori_loop(misc_ref[0], NSB_MAX, functools.partial(tail_fill, wait=wait), 0)

    def row_copy(r, kk):
        a = base + TOP_K * r + kk
        p = row0_ref[e_ref[a]] + rk_ref[a]
        return pltpu.make_async_copy(xp_ref.at[pl.ds(r, 1), :], xs_ref.at[pl.ds(p, 1), :], sem)

    def issue(r, carry):
        for kk in range(TOP_K):
            row_copy(r, kk).start()
        return carry

    lax.fori_loop(0, tm, issue, 0, unroll=8)
    for _ in range(TOP_K):
        pltpu.make_async_copy(xp_ref, xs_ref.at[pl.ds(0, tm), :], sem).wait()


def _dispatch(e_flat, rk_flat, row0, cnt, misc, xp, tm=512):
    n = xp.shape[0]
    return pl.pallas_call(
        functools.partial(_dispatch_kernel, tm=tm),
        out_shape=jax.ShapeDtypeStruct((NSB_MAX * R_SUB, D_MODEL), F32),
        grid_spec=pltpu.PrefetchScalarGridSpec(
            num_scalar_prefetch=5,
            grid=(n // tm,),
            in_specs=[pl.BlockSpec((tm, D_MODEL), lambda i, *_: (i, 0))],
            out_specs=pl.BlockSpec(memory_space=pl.ANY),
            scratch_shapes=[pltpu.VMEM((R_SUB, D_MODEL), F32),
                            pltpu.SemaphoreType.DMA(()),
                            pltpu.SemaphoreType.DMA(())]),
        compiler_params=_cparams(("arbitrary",)),
        name="dispatch",
    )(e_flat, rk_flat, row0, cnt, misc, xp)


def _expert_kernel(ee_ref, sb0_ref, nsb_ref, misc_ref, xs_hbm, wg_hbm, wu_hbm, wd_hbm, os_hbm,
                   xbuf, hbuf, obuf, zbuf, gbuf, ubuf, dbuf, xsem, osem, zsem, gsem, dsem):
    s = pl.program_id(0)
    n_ent = pl.num_programs(0)
    nsb = nsb_ref[s]
    nxt = jnp.minimum(s + 1, n_ent - 1)
    has_next = (s + 1 < n_ent) & (nsb_ref[nxt] > 0)

    def sub_row(ent, sb):
        return pl.multiple_of((sb0_ref[ent] + sb) * R_SUB, R_SUB)

    def x_copy(ent, sb, wait):
        cp = pltpu.make_async_copy(xs_hbm.at[pl.ds(sub_row(ent, sb), R_SUB), :], xbuf.at[sb], xsem.at[sb])
        cp.wait() if wait else cp.start()

    def x_copies(ent, wait):
        for sb in range(N_SUB):
            @pl.when(sb < nsb_ref[ent])
            def _(sb=sb):
                x_copy(ent, sb, wait)

    def gu_slot(ent, c):
        return lax.rem(ent * N_A + c, GU_SLOTS)

    def gu_copies(ent, c, wait):
        e, sl = ee_ref[ent], gu_slot(ent, c)
        for w_hbm, buf in ((wg_hbm, gbuf), (wu_hbm, ubuf)):
            cp = pltpu.make_async_copy(w_hbm.at[e, :, pl.ds(c * GU_CHUNK, GU_CHUNK)], buf.at[sl], gsem.at[sl])
            cp.wait() if wait else cp.start()

    def dn_copy(ent, nb, wait):
        cp = pltpu.make_async_copy(wd_hbm.at[ee_ref[ent], :, pl.ds(nb * DN_CHUNK, DN_CHUNK)],
                                   dbuf.at[nb], dsem.at[nb])
        cp.wait() if wait else cp.start()

    def o_copy(ent, sb, nb, wait):
        cp = pltpu.make_async_copy(
            obuf.at[nb % 2, sb],
            os_hbm.at[pl.ds(sub_row(ent, sb), R_SUB), pl.ds(nb * DN_CHUNK, DN_CHUNK)], osem.at[nb % 2])
        cp.wait() if wait else cp.start()

    def o_waits(ent, nb):
        for sb in range(N_SUB):
            @pl.when(sb < nsb_ref[ent])
            def _(sb=sb):
                o_copy(ent, sb, nb, True)

    @pl.when(s == 0)
    def _():
        x_copies(0, False)
        for c in range(N_A):
            gu_copies(0, c, False)
        for nb in range(N_B):
            dn_copy(0, nb, False)

    @pl.when(nsb == 0)
    def _():
        @pl.when(s >= 1)
        def _():
            for nb in range(N_B - 2, N_B):
                o_waits(s - 1, nb)

    @pl.when(nsb > 0)
    def _():
        x_copies(s, True)

        for c in range(N_A):
            gu_copies(s, c, True)

            @pl.when(has_next)
            def _(c=c):
                gu_copies(nxt, c, False)
            for sb in range(N_SUB):
                @pl.when(sb < nsb)
                def _(sb=sb, c=c):
                    xb = xbuf[sb].astype(BF16)
                    sl = gu_slot(s, c)
                    gate = jnp.dot(xb, gbuf[sl].astype(BF16), preferred_element_type=F32)
                    up = jnp.dot(xb, ubuf[sl].astype(BF16), preferred_element_type=F32)
                    hbuf[c, sb] = (jax.nn.silu(gate) * up).astype(BF16)

                if c == N_A - 1:
                    @pl.when(has_next & (sb < nsb_ref[nxt]))
                    def _(sb=sb):
                        x_copy(nxt, sb, False)

        for nb in range(N_B):
            if nb >= 2:
                o_waits(s, nb - 2)
            else:
                @pl.when(s >= 1)
                def _(nb=nb):
                    o_waits(s - 1, nb + N_B - 2)
            dn_copy(s, nb, True)
            for sb in range(N_SUB):
                @pl.when(sb < nsb)
                def _(sb=sb, nb=nb):
                    wd = dbuf.at[nb]
                    acc = jnp.dot(hbuf[0, sb], wd[0:GU_CHUNK, :].astype(BF16), preferred_element_type=F32)
                    for ca in range(1, N_A):
                        acc = acc + jnp.dot(hbuf[ca, sb],
                                            wd[ca * GU_CHUNK:(ca + 1) * GU_CHUNK, :].astype(BF16),
                                            preferred_element_type=F32)
                    obuf[nb % 2, sb] = acc
                    o_copy(s, sb, nb, False)

            @pl.when(has_next)
            def _(nb=nb):
                dn_copy(nxt, nb, False)

    @pl.when(s == n_ent - 1)
    def _():
        @pl.when(nsb > 0)
        def _():
            for nb in range(N_B - 2, N_B):
                o_waits(s, nb)
        zbuf[...] = jnp.zeros_like(zbuf)

        def tail_fill(sb, carry, *, wait):
            r = pl.multiple_of(sb * R_SUB, R_SUB)
            for nb in range(N_B):
                cp = pltpu.make_async_copy(
                    zbuf, os_hbm.at[pl.ds(r, R_SUB), pl.ds(nb * DN_CHUNK, DN_CHUNK)], zsem)
                cp.wait() if wait else cp.start()
            return carry

        for wait in (False, True):
            lax.fori_loop(misc_ref[0], NSB_MAX, functools.partial(tail_fill, wait=wait), 0)


def _experts(ent_e, ent_sb0, ent_nsb, misc, xs, w_gate, w_up, w_down):
    assert N_B % 2 == 0, "output column chunks alternate between two staging sets"
    any_spec = pl.BlockSpec(memory_space=pl.ANY)
    return pl.pallas_call(
        _expert_kernel,
        out_shape=jax.ShapeDtypeStruct((NSB_MAX * R_SUB, D_MODEL), F32),
        grid_spec=pltpu.PrefetchScalarGridSpec(
            num_scalar_prefetch=4,
            grid=(NE_MAX,),
            in_specs=[any_spec, any_spec, any_spec, any_spec],
            out_specs=any_spec,
            scratch_shapes=[pltpu.VMEM((N_SUB, R_SUB, D_MODEL), F32),
                            pltpu.VMEM((N_A, N_SUB, R_SUB, GU_CHUNK), BF16),
                            pltpu.VMEM((2, N_SUB, R_SUB, DN_CHUNK), F32),
                            pltpu.VMEM((R_SUB, DN_CHUNK), F32),
                            pltpu.VMEM((GU_SLOTS, D_MODEL, GU_CHUNK), F32),
                            pltpu.VMEM((GU_SLOTS, D_MODEL, GU_CHUNK), F32),
                            pltpu.VMEM((N_B, D_EXPERT, DN_CHUNK), F32),
                            pltpu.SemaphoreType.DMA((N_SUB,)),
                            pltpu.SemaphoreType.DMA((2,)),
                            pltpu.SemaphoreType.DMA(()),
                            pltpu.SemaphoreType.DMA((GU_SLOTS,)),
                            pltpu.SemaphoreType.DMA((N_B,))]),
        compiler_params=_cparams(("arbitrary",)),
        name="experts",
    )(ent_e, ent_sb0, ent_nsb, misc, xs, w_gate, w_up, w_down)


def _combine_kernel(e_ref, rk_ref, row0_ref, x_ref, wts_ref, g_ref, os_ref, y_ref, gbuf, sem, *, tm):
    i = pl.program_id(0)
    slot = lax.rem(i, 2)

    def issue_tile(tile, slot_):
        base = tile * (tm * TOP_K)

        def issue(r, carry):
            for kk in range(TOP_K):
                a = base + TOP_K * r + kk
                p = row0_ref[e_ref[a]] + rk_ref[a]
                pltpu.make_async_copy(os_ref.at[pl.ds(p, 1), :], gbuf.at[slot_, kk, pl.ds(r, 1), :],
                                      sem.at[slot_]).start()
            return carry

        lax.fori_loop(0, tm, issue, 0, unroll=8)

    @pl.when(i == 0)
    def _():
        issue_tile(0, 0)

    @pl.when(i + 1 < pl.num_programs(0))
    def _():
        issue_tile(i + 1, 1 - slot)

    for kk in range(TOP_K):
        pltpu.make_async_copy(os_ref.at[pl.ds(0, tm), :], gbuf.at[slot, kk], sem.at[slot]).wait()
    moe = wts_ref[:, 0:1] * gbuf[slot, 0] + wts_ref[:, 1:2] * gbuf[slot, 1]
    y_ref[...] = _rms(x_ref[...] + moe, g_ref[...])


def _combine(e_flat, rk_flat, row0, x1, wts, g, out_sorted, tm=256):
    n = x1.shape[0]
    return pl.pallas_call(
        functools.partial(_combine_kernel, tm=tm),
        out_shape=jax.ShapeDtypeStruct((n, D_MODEL), F32),
        grid_spec=pltpu.PrefetchScalarGridSpec(
            num_scalar_prefetch=3,
            grid=(n // tm,),
            in_specs=[pl.BlockSpec((tm, D_MODEL), lambda i, *_: (i, 0)),
                      pl.BlockSpec((tm, LANES), lambda i, *_: (i, 0)),
                      pl.BlockSpec((1, D_MODEL), lambda i, *_: (0, 0)),
                      pl.BlockSpec(memory_space=pl.ANY)],
            out_specs=pl.BlockSpec((tm, D_MODEL), lambda i, *_: (i, 0)),
            scratch_shapes=[pltpu.VMEM((2, TOP_K, tm, D_MODEL), F32),
                            pltpu.SemaphoreType.DMA((2,))]),
        compiler_params=_cparams(("arbitrary",)),
        name="combine",
    )(e_flat, rk_flat, row0, x1, wts, g, out_sorted)


def _moe(x1, norm_g, wr_g, br_g, wr_e, br_e, w_gate, w_up, w_down, final_g):
    n = x1.shape[0]
    pad = LANES - N_EXPERTS - N_GROUPS
    wr = jnp.concatenate([wr_e, wr_g, jnp.zeros((D_MODEL, pad), F32)], axis=1)
    br = jnp.concatenate([br_e, br_g, jnp.zeros((pad,), F32)])[None]
    xp, meta, wts, counts = _router(x1, norm_g[None], wr, br)

    assert n * TOP_K == N_ASSIGN
    cnt = counts[0, :N_EXPERTS]
    nsb = (cnt + R_SUB - 1) // R_SUB
    sb_end = jnp.cumsum(nsb)
    sb_start = sb_end - nsb
    nent = (nsb + N_SUB - 1) // N_SUB
    ent_end = jnp.cumsum(nent)
    ent_start = ent_end - nent
    ids = jnp.arange(NE_MAX, dtype=I32)
    valid = ids < ent_end[-1]
    ent_e = jnp.minimum(jnp.sum(ids[:, None] >= ent_end[None, :], axis=1), N_EXPERTS - 1)
    ent_e = jnp.where(valid, ent_e, ent_e[ent_end[-1] - 1])
    within = (ids - ent_start[ent_e]) * N_SUB
    ent_sb0 = jnp.where(valid, sb_start[ent_e] + within, 0)
    ent_nsb = jnp.where(valid, jnp.clip(nsb[ent_e] - within, 0, N_SUB), 0)
    row0 = (sb_start * R_SUB).astype(I32)
    misc = sb_end[-1:].astype(I32)
    e_flat = meta[:, 0:TOP_K].reshape(-1)
    rk_flat = meta[:, TOP_K:2 * TOP_K].reshape(-1)

    xs = _dispatch(e_flat, rk_flat, row0, cnt, misc, xp)
    out_sorted = _experts(ent_e.astype(I32), ent_sb0.astype(I32), ent_nsb.astype(I32), misc, xs,
                          w_gate, w_up, w_down)
    return _combine(e_flat, rk_flat, row0, x1, wts, final_g[None], out_sorted)


def kernel(x, norm_mix_g, w_in, gmlp_v_norm_g, gmlp_spatial_w, gmlp_spatial_b, mlstm_igate_b,
           mlstm_fgate_b, mlstm_out_norm_g, w_out, norm_ffn_g, router_group_w, router_group_b,
           router_expert_w, router_expert_b, expert_w_gate, expert_w_up, expert_w_down, final_norm_g):
    batch, seq, d = x.shape
    n = batch * seq
    assert w_in.shape[0] == 1, "the final rmsnorm is fused after the single layer's MoE"
    l = 0
    x2 = x.reshape(n, d)
    proj, gate_pre = _in_proj(x2, norm_mix_g[l][None], w_in[l].T)
    y_a = _gmlp(proj, gmlp_v_norm_g[l][None], gmlp_spatial_w[l], gmlp_spatial_b[l].T)
    gate_b = jnp.concatenate([mlstm_igate_b[l], mlstm_fgate_b[l],
                              jnp.zeros((LANES - N_GATE_COLS,), F32)])[None]
    y_b = _mlstm(proj, gate_pre, gate_b, mlstm_out_norm_g[l][None], batch, seq)
    x1 = _out_proj(y_a, y_b, w_out[l].astype(BF16), x2)
    out = _moe(x1, norm_ffn_g[l], router_group_w[l], router_group_b[l], router_expert_w[l],
               router_expert_b[l], expert_w_gate[l], expert_w_up[l], expert_w_down[l], final_norm_g)
    return out.reshape(batch, seq, d)
```

```python
import functools

import jax
import jax.numpy as jnp
from jax import lax
from jax.experimental import pallas as pl
from jax.experimental.pallas import tpu as pltpu

F32 = jnp.float32
BF16 = jnp.bfloat16
I32 = jnp.int32

D_MODEL = 4096
D_GMLP = 2048
GMLP_CHUNK = 128
GMLP_GROUPS = 16
D_MLSTM = 2048
MLSTM_HEADS = 4
MLSTM_DV = 512
MLSTM_DQK = 256
GATE_SOFTCAP = 15.0
N_GROUPS = 8
EXPERTS_PER_GROUP = 8
N_EXPERTS = 64
TOP_K = 2
D_EXPERT = 512
EPS = 1e-6
D_PROJ_MAIN = 10240
N_GATE_COLS = 8

LANES = 128
SUBLANES = 8
VMEM_LIMIT = 56 * 1024 * 1024
MLSTM_L = 256
N_ASSIGN = 4 * 4096 * TOP_K
R_SUB = 192
N_SUB = 4
NSB_MAX = -(-(N_ASSIGN + N_EXPERTS * (R_SUB - 1)) // R_SUB)
NE_MAX = N_EXPERTS + N_ASSIGN // (R_SUB * N_SUB)
N_A = 2
DN_CHUNK = 1024
N_B = D_MODEL // DN_CHUNK
GU_SLOTS = N_A + 1


def _cparams(sem):
    return pltpu.CompilerParams(dimension_semantics=sem, vmem_limit_bytes=VMEM_LIMIT)


def _rms(x, g):
    return x * lax.rsqrt(jnp.mean(x * x, axis=-1, keepdims=True) + EPS) * g


def _in_proj_kernel(x_hbm, g_ref, w_ref, wg_ref, proj_ref, gate_ref, x_buf, xn_sc, sem, *, tm, strip):
    i = pl.program_id(0)
    nt = (((1,), (1,)), ((), ()))

    def x_copy(tile):
        return pltpu.make_async_copy(x_hbm.at[pl.ds(pl.multiple_of(tile * tm, tm), tm), :], x_buf, sem)

    @pl.when(pl.program_id(1) == 0)
    def _():
        @pl.when(i == 0)
        def _():
            x_copy(0).start()

        x_copy(i).wait()
        for r in range(0, tm, strip):
            rows = slice(r, r + strip)
            xn_sc[rows, :] = _rms(x_buf[rows, :], g_ref[...]).astype(BF16)

        @pl.when(i + 1 < pl.num_programs(0))
        def _():
            x_copy(i + 1).start()

        wg = jnp.concatenate([wg_ref[...].astype(BF16),
                              jnp.zeros((LANES - N_GATE_COLS, D_MODEL), BF16)], axis=0)
        gate_ref[...] = lax.dot_general(xn_sc[...], wg, nt, preferred_element_type=F32)

    proj_ref[...] = lax.dot_general(xn_sc[...], w_ref[...].astype(BF16), nt,
                                    preferred_element_type=F32).astype(BF16)


def _in_proj(x2, g, w_t, tm=1024, tn=512, strip=32):
    n = x2.shape[0]
    return pl.pallas_call(
        functools.partial(_in_proj_kernel, tm=tm, strip=strip),
        out_shape=(jax.ShapeDtypeStruct((n, D_PROJ_MAIN), BF16),
                   jax.ShapeDtypeStruct((n, LANES), F32)),
        grid=(n // tm, D_PROJ_MAIN // tn),
        in_specs=[pl.BlockSpec(memory_space=pl.ANY),
                  pl.BlockSpec((1, D_MODEL), lambda i, j: (0, 0)),
                  pl.BlockSpec((tn, D_MODEL), lambda i, j: (j, 0)),
                  pl.BlockSpec((N_GATE_COLS, D_MODEL), lambda i, j: (D_PROJ_MAIN // N_GATE_COLS, 0))],
        out_specs=(pl.BlockSpec((tm, tn), lambda i, j: (i, j)),
                   pl.BlockSpec((tm, LANES), lambda i, j: (i, 0))),
        scratch_shapes=[pltpu.VMEM((tm, D_MODEL), F32),
                        pltpu.VMEM((tm, D_MODEL), BF16),
                        pltpu.SemaphoreType.DMA(())],
        compiler_params=_cparams(("arbitrary", "arbitrary")),
        name="in_proj",
    )(x2, g, w_t, w_t)


def _gmlp_kernel(u_ref, v_ref, gv_ref, w_ref, bt_ref, y_ref, wm_sc, *, rows):
    c = GMLP_CHUNK

    @pl.when(pl.program_id(0) == 0)
    def _():
        r = lax.broadcasted_iota(I32, (c, c), 0)
        s = lax.broadcasted_iota(I32, (c, c), 1)
        for g in range(GMLP_GROUPS):
            wm_sc[g] = jnp.where(s <= r, w_ref[g], 0.0).astype(BF16)

    for ci in range(rows // c):
        rs = slice(ci * c, (ci + 1) * c)
        gu = jax.nn.gelu(u_ref[rs, :].astype(F32))
        gv = jax.nn.gelu(v_ref[rs, :].astype(F32))
        vn = _rms(gv, gv_ref[...]).astype(BF16)
        for g in range(GMLP_GROUPS):
            cs = slice(g * c, (g + 1) * c)
            z = jnp.dot(wm_sc[g], vn[:, cs], preferred_element_type=F32) + bt_ref[:, g:g + 1]
            y_ref[rs, cs] = (gu[:, cs] * z).astype(BF16)


def _gmlp(proj, gv, w_s, b_t, rows=512):
    n = proj.shape[0]
    return pl.pallas_call(
        functools.partial(_gmlp_kernel, rows=rows),
        out_shape=jax.ShapeDtypeStruct((n, D_GMLP), BF16),
        grid=(n // rows,),
        in_specs=[pl.BlockSpec((rows, D_GMLP), lambda i: (i, 0)),
                  pl.BlockSpec((rows, D_GMLP), lambda i: (i, 1)),
                  pl.BlockSpec((1, D_GMLP), lambda i: (0, 0)),
                  pl.BlockSpec((GMLP_GROUPS, GMLP_CHUNK, GMLP_CHUNK), lambda i: (0, 0, 0)),
                  pl.BlockSpec((GMLP_CHUNK, GMLP_GROUPS), lambda i: (0, 0))],
        out_specs=pl.BlockSpec((rows, D_GMLP), lambda i: (i, 0)),
        scratch_shapes=[pltpu.VMEM((GMLP_GROUPS, GMLP_CHUNK, GMLP_CHUNK), BF16)],
        compiler_params=_cparams(("arbitrary",)),
        name="gmlp",
    )(proj, proj, gv, w_s, b_t)


def _mlstm_kernel(q_ref, k_ref, v_ref, o_ref, gp_ref, gb_ref, og_ref, y_ref, c_sc, n_sc, m_sc):
    L, H, DQK, DV = MLSTM_L, MLSTM_HEADS, MLSTM_DQK, MLSTM_DV
    scale = DQK ** -0.5

    @pl.when(pl.program_id(1) == 0)
    def _():
        c_sc[...] = jnp.zeros_like(c_sc)
        n_sc[...] = jnp.zeros_like(n_sc)
        m_sc[...] = jnp.zeros_like(m_sc)

    lane = lax.broadcasted_iota(I32, (L, LANES), 1)
    z = gp_ref[...] + gb_ref[...]
    sc = GATE_SOFTCAP * jnp.tanh(z / GATE_SOFTCAP)
    logf = -(jnp.maximum(-sc, 0.0) + jnp.log1p(jnp.exp(-jnp.abs(sc))))
    row = lax.broadcasted_iota(I32, (L, L), 0)
    col = lax.broadcasted_iota(I32, (L, L), 1)
    causal = col <= row
    tri = jnp.where(causal, 1.0, 0.0).astype(BF16)
    bcum, rest = None, logf
    for _ in range(3):
        part = rest.astype(BF16)
        rest = rest - part.astype(F32)
        term = jnp.dot(tri, part, preferred_element_type=F32)
        bcum = term if bcum is None else bcum + term
    gcol = jnp.where(lane < H, sc, bcum)
    grow = gcol.T

    for h in range(H):
        ig_col, b_col = gcol[:, h:h + 1], gcol[:, H + h:H + h + 1]
        ig_row, b_row = grow[h:h + 1, :], grow[H + h:H + h + 1, :]
        m = m_sc[h, 0:1, 0:1]
        q = q_ref[:, h * DQK:(h + 1) * DQK]
        k = k_ref[:, h * DQK:(h + 1) * DQK]
        v = v_ref[:, h * DV:(h + 1) * DV]

        d_log = jnp.where(causal, b_col - b_row + ig_row, -jnp.inf)
        inter = b_col + m
        m_row = jnp.maximum(inter, jnp.max(d_log, axis=-1, keepdims=True))
        w_intra = jnp.exp(d_log - m_row)
        w_inter = jnp.exp(inter - m_row)
        s = lax.dot_general(q, k, (((1,), (1,)), ((), ())), preferred_element_type=F32)
        s = s * scale * w_intra
        c_old = c_sc[h]
        n_old = n_sc[h]
        num = (w_inter * jnp.dot(q, c_old.astype(BF16), preferred_element_type=F32)
               + jnp.dot(s.astype(BF16), v, preferred_element_type=F32))
        qn = jnp.sum(q.astype(F32) * n_old, axis=-1, keepdims=True)
        den = w_inter * qn + jnp.sum(s, axis=-1, keepdims=True)
        hh = num / jnp.maximum(jnp.abs(den), jnp.exp(-m_row))

        b_last = b_row[:, L - 1:L]
        tail = b_last - b_col + ig_col
        m_new = jnp.maximum(b_last + m, jnp.max(tail, axis=0, keepdims=True))
        w_tail = jnp.exp(tail - m_new)
        decay = jnp.exp(b_last + m - m_new)
        kw = k.astype(F32) * scale * w_tail
        c_sc[h] = decay * c_old + lax.dot_general(
            kw.astype(BF16), v, (((0,), (0,)), ((), ())), preferred_element_type=F32)
        n_sc[h] = decay * n_old + jnp.sum(kw, axis=0, keepdims=True)
        m_sc[h] = jnp.broadcast_to(m_new, m_sc.shape[1:])

        hn = hh * lax.rsqrt(jnp.mean(hh * hh, axis=-1, keepdims=True) + EPS)
        hn = hn * og_ref[:, h * DV:(h + 1) * DV]
        hn = hn * jax.nn.sigmoid(o_ref[:, h * DV:(h + 1) * DV].astype(F32))
        y_ref[:, h * DV:(h + 1) * DV] = hn.astype(BF16)


def _mlstm(proj, gate_pre, gate_b, out_g, batch, seq):
    n = proj.shape[0]
    L = MLSTM_L
    nc = seq // L
    qk_w = MLSTM_HEADS * MLSTM_DQK
    q_blk, k_blk = 2 * D_GMLP // qk_w, 2 * D_GMLP // qk_w + 1
    v_blk, o_blk = (2 * D_GMLP + 2 * qk_w) // D_MLSTM, (2 * D_GMLP + 2 * qk_w) // D_MLSTM + 1
    rmap = lambda b, c: b * nc + c
    return pl.pallas_call(
        _mlstm_kernel,
        out_shape=jax.ShapeDtypeStruct((n, D_MLSTM), BF16),
        grid=(batch, nc),
        in_specs=[pl.BlockSpec((L, qk_w), lambda b, c: (rmap(b, c), q_blk)),
                  pl.BlockSpec((L, qk_w), lambda b, c: (rmap(b, c), k_blk)),
                  pl.BlockSpec((L, D_MLSTM), lambda b, c: (rmap(b, c), v_blk)),
                  pl.BlockSpec((L, D_MLSTM), lambda b, c: (rmap(b, c), o_blk)),
                  pl.BlockSpec((L, LANES), lambda b, c: (rmap(b, c), 0)),
                  pl.BlockSpec((1, LANES), lambda b, c: (0, 0)),
                  pl.BlockSpec((1, D_MLSTM), lambda b, c: (0, 0))],
        out_specs=pl.BlockSpec((L, D_MLSTM), lambda b, c: (rmap(b, c), 0)),
        scratch_shapes=[pltpu.VMEM((MLSTM_HEADS, MLSTM_DQK, MLSTM_DV), F32),
                        pltpu.VMEM((MLSTM_HEADS, 1, MLSTM_DQK), F32),
                        pltpu.VMEM((MLSTM_HEADS, 8, LANES), F32)],
        compiler_params=_cparams(("parallel", "arbitrary")),
        name="mlstm",
    )(proj, proj, proj, proj, gate_pre, gate_b, out_g)


def _out_proj_kernel(ya_ref, yb_ref, wa_ref, wb_ref, x_ref, o_ref):
    acc = jnp.dot(ya_ref[...], wa_ref[...], preferred_element_type=F32)
    acc = acc + jnp.dot(yb_ref[...], wb_ref[...], preferred_element_type=F32)
    o_ref[...] = x_ref[...] + acc


def _out_proj(y_a, y_b, w_out, x2, tm=1024, tn=1024):
    n = x2.shape[0]
    return pl.pallas_call(
        _out_proj_kernel,
        out_shape=jax.ShapeDtypeStruct((n, D_MODEL), F32),
        grid=(n // tm, D_MODEL // tn),
        in_specs=[pl.BlockSpec((tm, D_GMLP), lambda i, j: (i, 0)),
                  pl.BlockSpec((tm, D_MLSTM), lambda i, j: (i, 0)),
                  pl.BlockSpec((D_GMLP, tn), lambda i, j: (0, j)),
                  pl.BlockSpec((D_MLSTM, tn), lambda i, j: (1, j)),
                  pl.BlockSpec((tm, tn), lambda i, j: (i, j))],
        out_specs=pl.BlockSpec((tm, tn), lambda i, j: (i, j)),
        compiler_params=_cparams(("parallel", "arbitrary")),
        name="out_proj",
    )(y_a, y_b, w_out, w_out, x2)


def _router_kernel(x_ref, g_ref, wr_ref, br_ref, xp_ref, meta_ref, wts_ref, cnt_ref, carry_sc, w2_sc,
                   *, tm):
    i = pl.program_id(0)

    @pl.when(i == 0)
    def _():
        carry_sc[...] = jnp.zeros_like(carry_sc)
        w_hi = wr_ref[...].astype(BF16)
        w2_sc[:, :LANES] = w_hi
        w2_sc[:, LANES:] = (wr_ref[...] - w_hi.astype(F32)).astype(BF16)

    xn = _rms(x_ref[...], g_ref[...])
    xp_ref[...] = xn

    x_hi = xn.astype(BF16)
    x_lo = (xn - x_hi.astype(F32)).astype(BF16)
    both = jnp.dot(x_hi, w2_sc[...], preferred_element_type=F32)
    logits = (both[:, :LANES] + both[:, LANES:]
              + jnp.dot(x_lo, w2_sc[:, :LANES], preferred_element_type=F32)) + br_ref[...]
    lane = lax.broadcasted_iota(I32, (tm, LANES), 1)
    lane_f = lane.astype(F32)
    big = float(LANES)
    is_g = (lane >= N_EXPERTS) & (lane < N_EXPERTS + N_GROUPS)
    gl = jnp.where(is_g, logits, -jnp.inf)
    gmax = jnp.max(gl, axis=-1, keepdims=True)
    g_lane = jnp.min(jnp.where(gl == gmax, lane_f, big), axis=-1, keepdims=True)
    g_sum = jnp.sum(jnp.where(is_g, jnp.exp(gl - gmax), 0.0), axis=-1, keepdims=True)
    g_prob = 1.0 / g_sum
    g_idx = g_lane.astype(I32) - N_EXPERTS
    in_grp = (lane < N_EXPERTS) & (lax.shift_right_logical(lane, 3) == g_idx)
    el = jnp.where(in_grp, logits, -jnp.inf)
    t1 = jnp.max(el, axis=-1, keepdims=True)
    i1 = jnp.min(jnp.where(el == t1, lane_f, big), axis=-1, keepdims=True)
    el2 = jnp.where(lane_f == i1, -jnp.inf, el)
    t2 = jnp.max(el2, axis=-1, keepdims=True)
    i2 = jnp.min(jnp.where(el2 == t2, lane_f, big), axis=-1, keepdims=True)
    e2 = jnp.exp(t2 - t1)
    w1 = g_prob * (1.0 / (1.0 + e2))
    w2 = g_prob * (e2 / (1.0 + e2))

    sel1 = lane_f == i1
    sel2 = lane_f == i2
    onehot = jnp.where(sel1 | sel2, 1.0, 0.0)
    r = lax.broadcasted_iota(I32, (tm, tm), 0)
    c = lax.broadcasted_iota(I32, (tm, tm), 1)
    strict = jnp.where(c < r, 1.0, 0.0).astype(BF16)
    prefix = jnp.dot(strict, onehot.astype(BF16), preferred_element_type=F32) + carry_sc[0:1, :]
    rank1 = jnp.sum(jnp.where(sel1, prefix, 0.0), axis=-1, keepdims=True)
    rank2 = jnp.sum(jnp.where(sel2, prefix, 0.0), axis=-1, keepdims=True)
    total = carry_sc[0:1, :] + jnp.sum(onehot, axis=0, keepdims=True)
    carry_sc[...] = jnp.broadcast_to(total, carry_sc.shape)
    cnt_ref[...] = jnp.broadcast_to(total, cnt_ref.shape).astype(I32)

    meta = jnp.where(lane == 0, i1, jnp.where(lane == 1, i2,
                     jnp.where(lane == 2, rank1, jnp.where(lane == 3, rank2, 0.0))))
    meta_ref[...] = meta.astype(I32)
    wts_ref[...] = jnp.where(lane == 0, w1, jnp.where(lane == 1, w2, 0.0))


def _router(x1, g, wr, br, tm=512):
    n = x1.shape[0]
    return pl.pallas_call(
        functools.partial(_router_kernel, tm=tm),
        out_shape=(jax.ShapeDtypeStruct((n, D_MODEL), F32),
                   jax.ShapeDtypeStruct((n, LANES), I32),
                   jax.ShapeDtypeStruct((n, LANES), F32),
                   jax.ShapeDtypeStruct((8, LANES), I32)),
        grid=(n // tm,),
        in_specs=[pl.BlockSpec((tm, D_MODEL), lambda i: (i, 0)),
                  pl.BlockSpec((1, D_MODEL), lambda i: (0, 0)),
                  pl.BlockSpec((D_MODEL, LANES), lambda i: (0, 0)),
                  pl.BlockSpec((1, LANES), lambda i: (0, 0))],
        out_specs=(pl.BlockSpec((tm, D_MODEL), lambda i: (i, 0)),
                   pl.BlockSpec((tm, LANES), lambda i: (i, 0)),
                   pl.BlockSpec((tm, LANES), lambda i: (i, 0)),
                   pl.BlockSpec((8, LANES), lambda i: (0, 0))),
        scratch_shapes=[pltpu.VMEM((8, LANES), F32),
                        pltpu.VMEM((D_MODEL, 2 * LANES), BF16)],
        compiler_params=_cparams(("arbitrary",)),
        name="router",
    )(x1, g, wr, br)


def _dispatch_kernel(e_ref, rk_ref, row0_ref, cnt_ref, misc_ref, xp_ref, xs_ref, zbuf, sem, zsem, *, tm):
    base = pl.program_id(0) * (tm * TOP_K)

    @pl.when(pl.program_id(0) == 0)
    def _():
        zbuf[...] = jnp.zeros_like(zbuf)

        def pad_fill(e, carry, *, wait):
            cnt = cnt_ref[e]
            start = row0_ref[e] + cnt
            head = (-cnt) & (SUBLANES - 1)
            for j in range(SUBLANES - 1):
                @pl.when(j < head)
                def _(j=j):
                    cp = pltpu.make_async_copy(zbuf.at[pl.ds(0, 1), :], xs_ref.at[pl.ds(start + j, 1), :], zsem)
                    cp.wait() if wait else cp.start()
            off = start + head
            rest = lax.rem(R_SUB - lax.rem(cnt, R_SUB), R_SUB) - head
            bit = 1 << (R_SUB.bit_length() - 1)
            while bit >= SUBLANES:
                @pl.when((rest & bit) != 0)
                def _(off=off, bit=bit):
                    dst = xs_ref.at[pl.ds(pl.multiple_of(off, SUBLANES), bit), :]
                    cp = pltpu.make_async_copy(zbuf.at[pl.ds(0, bit), :], dst, zsem)
                    cp.wait() if wait else cp.start()
                off = off + (rest & bit)
                bit //= 2
            return carry

        def tail_fill(sb, carry, *, wait):
            r = pl.multiple_of(sb * R_SUB, R_SUB)
            cp = pltpu.make_async_copy(zbuf, xs_ref.at[pl.ds(r, R_SUB), :], zsem)
            cp.wait() if wait else cp.start()
            return carry

        for wait in (False, True):
            lax.fori_loop(0, N_EXPERTS, functools.partial(pad_fill, wait=wait), 0)
            lax.fori_loop(misc_ref[0], NSB_MAX, functools.partial(tail_fill, wait=wait), 0)

    def row_copy(r, kk):
        a = base + TOP_K * r + kk
        p = row0_ref[e_ref[a]] + rk_ref[a]
        return pltpu.make_async_copy(xp_ref.at[pl.ds(r, 1), :], xs_ref.at[pl.ds(p, 1), :], sem)

    def issue(r, carry):
        for kk in range(TOP_K):
            row_copy(r, kk).start()
        return carry

    lax.fori_loop(0, tm, issue, 0, unroll=8)
    for _ in range(TOP_K):
        pltpu.make_async_copy(xp_ref, xs_ref.at[pl.ds(0, tm), :], sem).wait()


def _dispatch(e_flat, rk_flat, row0, cnt, misc, xp, tm=512):
    n = xp.shape[0]
    return pl.pallas_call(
        functools.partial(_dispatch_kernel, tm=tm),
        out_shape=jax.ShapeDtypeStruct((NSB_MAX * R_SUB, D_MODEL), F32),
        grid_spec=pltpu.PrefetchScalarGridSpec(
            num_scalar_prefetch=5,
            grid=(n // tm,),
            in_specs=[pl.BlockSpec((tm, D_MODEL), lambda i, *_: (i, 0))],
            out_specs=pl.BlockSpec(memory_space=pl.ANY),
            scratch_shapes=[pltpu.VMEM((R_SUB, D_MODEL), F32),
                            pltpu.SemaphoreType.DMA(()),
                            pltpu.SemaphoreType.DMA(())]),
        compiler_params=_cparams(("arbitrary",)),
        name="dispatch",
    )(e_flat, rk_flat, row0, cnt, misc, xp)


def _expert_kernel(ee_ref, sb0_ref, nsb_ref, misc_ref, xs_hbm, wg_hbm, wu_hbm, wd_hbm, os_hbm,
                   xbuf, hbuf, obuf, zbuf, gbuf, gpre, dbuf, xsem, osem, zsem, gsem, dsem):
    s = pl.program_id(0)
    n_ent = pl.num_programs(0)
    nsb = nsb_ref[s]
    nxt = jnp.minimum(s + 1, n_ent - 1)
    has_next = (s + 1 < n_ent) & (nsb_ref[nxt] > 0)

    def sub_row(ent, sb):
        return pl.multiple_of((sb0_ref[ent] + sb) * R_SUB, R_SUB)

    def x_copy(ent, sb, wait):
        cp = pltpu.make_async_copy(xs_hbm.at[pl.ds(sub_row(ent, sb), R_SUB), :], xbuf.at[sb], xsem.at[sb])
        cp.wait() if wait else cp.start()

    def x_copies(ent, wait):
        for sb in range(N_SUB):
            @pl.when(sb < nsb_ref[ent])
            def _(sb=sb):
                x_copy(ent, sb, wait)

    def gu_slot(ent, c):
        return lax.rem(ent * N_A + c, GU_SLOTS)

    def gu_copies(ent, c, wait):
        sl = gu_slot(ent, c)
        w_hbm = wg_hbm if c == 0 else wu_hbm
        cp = pltpu.make_async_copy(w_hbm.at[ee_ref[ent]], gbuf.at[sl], gsem.at[sl])
        cp.wait() if wait else cp.start()

    def dn_copy(ent, nb, wait):
        cp = pltpu.make_async_copy(wd_hbm.at[ee_ref[ent], :, pl.ds(nb * DN_CHUNK, DN_CHUNK)],
                                   dbuf.at[nb], dsem.at[nb])
        cp.wait() if wait else cp.start()

    def o_copy(ent, sb, nb, wait):
        cp = pltpu.make_async_copy(
            obuf.at[nb % 2, sb],
            os_hbm.at[pl.ds(sub_row(ent, sb), R_SUB), pl.ds(nb * DN_CHUNK, DN_CHUNK)], osem.at[nb % 2])
        cp.wait() if wait else cp.start()

    def o_waits(ent, nb):
        for sb in range(N_SUB):
            @pl.when(sb < nsb_ref[ent])
            def _(sb=sb):
                o_copy(ent, sb, nb, True)

    @pl.when(s == 0)
    def _():
        x_copies(0, False)
        for c in range(N_A):
            gu_copies(0, c, False)
        for nb in range(N_B):
            dn_copy(0, nb, False)

    @pl.when(nsb == 0)
    def _():
        @pl.when(s >= 1)
        def _():
            for nb in range(N_B - 2, N_B):
                o_waits(s - 1, nb)

    @pl.when(nsb > 0)
    def _():
        x_copies(s, True)

        for c in range(N_A):
            gu_copies(s, c, True)

            @pl.when(has_next)
            def _(c=c):
                gu_copies(nxt, c, False)
            for sb in range(N_SUB):
                @pl.when(sb < nsb)
                def _(sb=sb, c=c):
                    pre = jnp.dot(xbuf[sb].astype(BF16), gbuf[gu_slot(s, c)].astype(BF16),
                                  preferred_element_type=F32)
                    if c == 0:
                        gpre[sb] = pre
                    else:
                        hbuf[sb] = (jax.nn.silu(gpre[sb]) * pre).astype(BF16)

                if c == N_A - 1:
                    @pl.when(has_next & (sb < nsb_ref[nxt]))
                    def _(sb=sb):
                        x_copy(nxt, sb, False)

        for nb in range(N_B):
            if nb >= 2:
                o_waits(s, nb - 2)
            else:
                @pl.when(s >= 1)
                def _(nb=nb):
                    o_waits(s - 1, nb + N_B - 2)
            dn_copy(s, nb, True)
            for sb in range(N_SUB):
                @pl.when(sb < nsb)
                def _(sb=sb, nb=nb):
                    obuf[nb % 2, sb] = jnp.dot(hbuf[sb], dbuf[nb].astype(BF16), preferred_element_type=F32)
                    o_copy(s, sb, nb, False)

            @pl.when(has_next)
            def _(nb=nb):
                dn_copy(nxt, nb, False)

    @pl.when(s == n_ent - 1)
    def _():
        @pl.when(nsb > 0)
        def _():
            for nb in range(N_B - 2, N_B):
                o_waits(s, nb)
        zbuf[...] = jnp.zeros_like(zbuf)

        def tail_fill(sb, carry, *, wait):
            r = pl.multiple_of(sb * R_SUB, R_SUB)
            for nb in range(N_B):
                cp = pltpu.make_async_copy(
                    zbuf, os_hbm.at[pl.ds(r, R_SUB), pl.ds(nb * DN_CHUNK, DN_CHUNK)], zsem)
                cp.wait() if wait else cp.start()
            return carry

        for wait in (False, True):
            lax.fori_loop(misc_ref[0], NSB_MAX, functools.partial(tail_fill, wait=wait), 0)


def _experts(ent_e, ent_sb0, ent_nsb, misc, xs, w_gate, w_up, w_down):
    assert N_B % 2 == 0, "output column chunks alternate between two staging sets"
    any_spec = pl.BlockSpec(memory_space=pl.ANY)
    return pl.pallas_call(
        _expert_kernel,
        out_shape=jax.ShapeDtypeStruct((NSB_MAX * R_SUB, D_MODEL), F32),
        grid_spec=pltpu.PrefetchScalarGridSpec(
            num_scalar_prefetch=4,
            grid=(NE_MAX,),
            in_specs=[any_spec, any_spec, any_spec, any_spec],
            out_specs=any_spec,
            scratch_shapes=[pltpu.VMEM((N_SUB, R_SUB, D_MODEL), F32),
                            pltpu.VMEM((N_SUB, R_SUB, D_EXPERT), BF16),
                            pltpu.VMEM((2, N_SUB, R_SUB, DN_CHUNK), F32),
                            pltpu.VMEM((R_SUB, DN_CHUNK), F32),
                            pltpu.VMEM((GU_SLOTS, D_MODEL, D_EXPERT), F32),
                            pltpu.VMEM((N_SUB, R_SUB, D_EXPERT), F32),
                            pltpu.VMEM((N_B, D_EXPERT, DN_CHUNK), F32),
                            pltpu.SemaphoreType.DMA((N_SUB,)),
                            pltpu.SemaphoreType.DMA((2,)),
                            pltpu.SemaphoreType.DMA(()),
                            pltpu.SemaphoreType.DMA((GU_SLOTS,)),
                            pltpu.SemaphoreType.DMA((N_B,))]),
        compiler_params=_cparams(("arbitrary",)),
        name="experts",
    )(ent_e, ent_sb0, ent_nsb, misc, xs, w_gate, w_up, w_down)


def _combine_kernel(e_ref, rk_ref, row0_ref, x_ref, wts_ref, g_ref, os_ref, y_ref, gbuf, sem, *, tm):
    i = pl.program_id(0)
    slot = lax.rem(i, 2)

    def issue_tile(tile, slot_):
        base = tile * (tm * TOP_K)

        def issue(r, carry):
            for kk in range(TOP_K):
                a = base + TOP_K * r + kk
                p = row0_ref[e_ref[a]] + rk_ref[a]
                pltpu.make_async_copy(os_ref.at[pl.ds(p, 1), :], gbuf.at[slot_, kk, pl.ds(r, 1), :],
                                      sem.at[slot_]).start()
            return carry

        lax.fori_loop(0, tm, issue, 0, unroll=8)

    @pl.when(i == 0)
    def _():
        issue_tile(0, 0)

    @pl.when(i + 1 < pl.num_programs(0))
    def _():
        issue_tile(i + 1, 1 - slot)

    for kk in range(TOP_K):
        pltpu.make_async_copy(os_ref.at[pl.ds(0, tm), :], gbuf.at[slot, kk], sem.at[slot]).wait()
    moe = wts_ref[:, 0:1] * gbuf[slot, 0] + wts_ref[:, 1:2] * gbuf[slot, 1]
    y_ref[...] = _rms(x_ref[...] + moe, g_ref[...])


def _combine(e_flat, rk_flat, row0, x1, wts, g, out_sorted, tm=256):
    n = x1.shape[0]
    return pl.pallas_call(
        functools.partial(_combine_kernel, tm=tm),
        out_shape=jax.ShapeDtypeStruct((n, D_MODEL), F32),
        grid_spec=pltpu.PrefetchScalarGridSpec(
            num_scalar_prefetch=3,
            grid=(n // tm,),
            in_specs=[pl.BlockSpec((tm, D_MODEL), lambda i, *_: (i, 0)),
                      pl.BlockSpec((tm, LANES), lambda i, *_: (i, 0)),
                      pl.BlockSpec((1, D_MODEL), lambda i, *_: (0, 0)),
                      pl.BlockSpec(memory_space=pl.ANY)],
            out_specs=pl.BlockSpec((tm, D_MODEL), lambda i, *_: (i, 0)),
            scratch_shapes=[pltpu.VMEM((2, TOP_K, tm, D_MODEL), F32),
                            pltpu.SemaphoreType.DMA((2,))]),
        compiler_params=_cparams(("arbitrary",)),
        name="combine",
    )(e_flat, rk_flat, row0, x1, wts, g, out_sorted)


def _moe(x1, norm_g, wr_g, br_g, wr_e, br_e, w_gate, w_up, w_down, final_g):
    n = x1.shape[0]
    pad = LANES - N_EXPERTS - N_GROUPS
    wr = jnp.concatenate([wr_e, wr_g, jnp.zeros((D_MODEL, pad), F32)], axis=1)
    br = jnp.concatenate([br_e, br_g, jnp.zeros((pad,), F32)])[None]
    xp, meta, wts, counts = _router(x1, norm_g[None], wr, br)

    assert n * TOP_K == N_ASSIGN
    cnt = counts[0, :N_EXPERTS]
    nsb = (cnt + R_SUB - 1) // R_SUB
    sb_end = jnp.cumsum(nsb)
    sb_start = sb_end - nsb
    nent = (nsb + N_SUB - 1) // N_SUB
    ent_end = jnp.cumsum(nent)
    ent_start = ent_end - nent
    ids = jnp.arange(NE_MAX, dtype=I32)
    valid = ids < ent_end[-1]
    ent_e = jnp.minimum(jnp.sum(ids[:, None] >= ent_end[None, :], axis=1), N_EXPERTS - 1)
    ent_e = jnp.where(valid, ent_e, ent_e[ent_end[-1] - 1])
    within = (ids - ent_start[ent_e]) * N_SUB
    ent_sb0 = jnp.where(valid, sb_start[ent_e] + within, 0)
    ent_nsb = jnp.where(valid, jnp.clip(nsb[ent_e] - within, 0, N_SUB), 0)
    row0 = (sb_start * R_SUB).astype(I32)
    misc = sb_end[-1:].astype(I32)
    e_flat = meta[:, 0:TOP_K].reshape(-1)
    rk_flat = meta[:, TOP_K:2 * TOP_K].reshape(-1)

    xs = _dispatch(e_flat, rk_flat, row0, cnt, misc, xp)
    out_sorted = _experts(ent_e.astype(I32), ent_sb0.astype(I32), ent_nsb.astype(I32), misc, xs,
                          w_gate, w_up, w_down)
    return _combine(e_flat, rk_flat, row0, x1, wts, final_g[None], out_sorted)


def kernel(x, norm_mix_g, w_in, gmlp_v_norm_g, gmlp_spatial_w, gmlp_spatial_b, mlstm_igate_b,
           mlstm_fgate_b, mlstm_out_norm_g, w_out, norm_ffn_g, router_group_w, router_group_b,
           router_expert_w, router_expert_b, expert_w_gate, expert_w_up, expert_w_down, final_norm_g):
    batch, seq, d = x.shape
    n = batch * seq
    assert w_in.shape[0] == 1, "the final rmsnorm is fused after the single layer's MoE"
    l = 0
    x2 = x.reshape(n, d)
    proj, gate_pre = _in_proj(x2, norm_mix_g[l][None], w_in[l].T)
    y_a = _gmlp(proj, gmlp_v_norm_g[l][None], gmlp_spatial_w[l], gmlp_spatial_b[l].T)
    gate_b = jnp.concatenate([mlstm_igate_b[l], mlstm_fgate_b[l],
                              jnp.zeros((LANES - N_GATE_COLS,), F32)])[None]
    y_b = _mlstm(proj, gate_pre, gate_b, mlstm_out_norm_g[l][None], batch, seq)
    x1 = _out_proj(y_a, y_b, w_out[l].astype(BF16), x2)
    out = _moe(x1, norm_ffn_g[l], router_group_w[l], router_group_b[l], router_expert_w[l],
               router_expert_b[l], expert_w_gate[l], expert_w_up[l], expert_w_down[l], final_norm_g)
    return out.reshape(batch, seq, d)
```
